```python
import math
import jax, jax.numpy as jnp
from jax import lax
import numpy as np

D_MODEL = 1024
BATCH = 4
SEQ = 4096
DEPTH = 2

SSM_WIDTH = D_MODEL // 2
SSM_GROUP = 16
SSM_GROUPS = SSM_WIDTH // SSM_GROUP
SSM_STATE = 64
SSM_DT_MIN = 1e-3
SSM_DT_MAX = 1e-1

RET_HEADS = 4
RET_QK_DIM = D_MODEL // 16
RET_V_DIM = 2 * RET_QK_DIM
RET_QK_WIDTH = RET_HEADS * RET_QK_DIM
RET_V_WIDTH = RET_HEADS * RET_V_DIM
RET_CHUNK = 128
RET_THETA = 10000.0

ATT_HEADS = 8
ATT_HEAD_DIM = 64
ATT_WIDTH = ATT_HEADS * ATT_HEAD_DIM
MOBA_BLOCK = 256
MOBA_TOPK = 3
ATT_Q_CHUNK = 128
ROPE_THETA = 500000.0
ROPE_DIM = ATT_HEAD_DIM // 4

N_BRANCH = 3
FFN_HIDDEN = 2816
CONV_WIDTH = 3
NORM_EPS = 1e-6
NEG_INF = -1e30

IN_SPLITS = [SSM_WIDTH, RET_QK_WIDTH, RET_QK_WIDTH, RET_V_WIDTH, RET_V_WIDTH,
             ATT_WIDTH, ATT_WIDTH, ATT_WIDTH, N_BRANCH * D_MODEL]
N_IN = sum(IN_SPLITS)

kernel_name = "hybrid_s5_retention_moba_convffn"


def rmsnorm(x, g):
    xf = x.astype(jnp.float32)
    r = xf * lax.rsqrt(jnp.mean(xf * xf, axis=-1, keepdims=True) + NORM_EPS)
    return (r * g.astype(jnp.float32)).astype(x.dtype)


def rotary(x, inv_freq):
    L = x.shape[1]
    half = inv_freq.shape[0]
    ang = jnp.arange(L, dtype=jnp.float32)[:, None] * inv_freq[None, :]
    cos = jnp.cos(ang)[None, :, None, :].astype(x.dtype)
    sin = jnp.sin(ang)[None, :, None, :].astype(x.dtype)
    x1 = x[..., :half]
    x2 = x[..., half:2 * half]
    rest = x[..., 2 * half:]
    return jnp.concatenate([x1 * cos - x2 * sin, x1 * sin + x2 * cos, rest], axis=-1)


def s5_branch(u, lam_re, lam_im, log_dt, b_re, b_im, c_re, c_im, d_skip, w_glu):
    bsz, L, _ = u.shape
    f32 = jnp.float32
    uf = u.astype(f32).reshape(bsz, L, SSM_GROUPS, SSM_GROUP)
    lam = lax.complex(lam_re.astype(f32), lam_im.astype(f32))
    dt = jnp.exp(log_dt.astype(f32))[:, None]
    lam_bar = jnp.exp(lam * dt)
    b = lax.complex(b_re.astype(f32), b_im.astype(f32))
    b_bar = ((lam_bar - 1.0) / lam)[..., None] * b
    bu = jnp.einsum('gph,blgh->blgp', b_bar, uf.astype(jnp.complex64))
    a = jnp.broadcast_to(lam_bar, bu.shape)

    def combine(e1, e2):
        a1, s1 = e1
        a2, s2 = e2
        return a1 * a2, a2 * s1 + s2

    _, states = lax.associative_scan(combine, (a, bu), axis=1)
    c = lax.complex(c_re.astype(f32), c_im.astype(f32))
    y = jnp.einsum('ghp,blgp->blgh', c, states).real \
        + d_skip.astype(f32).reshape(SSM_GROUPS, SSM_GROUP) * uf
    y = jax.nn.gelu(y.reshape(bsz, L, SSM_WIDTH))
    out = y * jax.nn.sigmoid(y @ w_glu.astype(f32))
    return out.astype(u.dtype)


def retention_branch(q, k, v, g, gn_g):
    bsz, L = q.shape[:2]
    f32 = jnp.float32
    C = RET_CHUNK
    n_chunks = L // C
    half = RET_QK_DIM // 2
    inv = 1.0 / (RET_THETA ** jnp.linspace(0.0, 1.0, half, dtype=f32))
    q = rotary(q, inv).astype(f32)
    k = rotary(k, inv).astype(f32) * (RET_QK_DIM ** -0.5)
    qc = q.reshape(bsz, n_chunks, C, RET_HEADS, RET_QK_DIM)
    kc = k.reshape(bsz, n_chunks, C, RET_HEADS, RET_QK_DIM)
    vc = v.astype(f32).reshape(bsz, n_chunks, C, RET_HEADS, RET_V_DIM)

    log_gamma = jnp.log1p(-jnp.exp2(-5.0 - jnp.arange(RET_HEADS, dtype=f32)))
    pos = jnp.arange(C, dtype=f32)
    rel = pos[:, None] - pos[None, :]
    decay = jnp.where(rel >= 0,
                      jnp.exp(log_gamma[:, None, None] * jnp.maximum(rel, 0.0)[None]),
                      0.0)
    s = jnp.einsum('bnchd,bnmhd->bnhcm', qc, kc) * decay
    o_intra = jnp.einsum('bnhcm,bnmhe->bnche', s, vc)
    zeta = jnp.exp(log_gamma[:, None] * (C - 1.0 - pos)[None, :])
    kv = jnp.einsum('bnmhd,bnmhe,hm->nbhde', kc, vc, zeta)
    chunk_decay = jnp.exp(log_gamma * C)[None, :, None, None]

    def step(state, kv_n):
        return chunk_decay * state + kv_n, state

    _, state_prev = lax.scan(step, jnp.zeros_like(kv[0]), kv)
    xi = jnp.exp(log_gamma[:, None] * (pos + 1.0)[None, :])
    o_cross = jnp.einsum('bnchd,nbhde,hc->bnche', qc, state_prev, xi)
    o = (o_intra + o_cross).reshape(bsz, L, RET_HEADS, RET_V_DIM)
    mu = jnp.mean(o, axis=-1, keepdims=True)
    var = jnp.mean(jnp.square(o - mu), axis=-1, keepdims=True)
    o = ((o - mu) * lax.rsqrt(var + NORM_EPS)).reshape(bsz, L, RET_V_WIDTH) * gn_g.astype(f32)
    return (jax.nn.silu(g.astype(f32)) * o).astype(g.dtype)


def moba_branch(q, k, v, qn_g, kn_g):
    bsz, L = q.shape[:2]
    out_dtype = q.dtype
    f32 = jnp.float32
    H, hd, BLK, QC = ATT_HEADS, ATT_HEAD_DIM, MOBA_BLOCK, ATT_Q_CHUNK
    q = rmsnorm(q, qn_g)
    k = rmsnorm(k, kn_g)
    rh = ROPE_DIM // 2
    inv = ROPE_THETA ** (-jnp.arange(rh, dtype=f32) / rh)
    q = rotary(q, inv)
    k = rotary(k, inv)

    nb = -(-L // BLK)
    Lp = nb * BLK
    pad = ((0, 0), (0, Lp - L), (0, 0), (0, 0))
    q = jnp.pad(q, pad).astype(f32).transpose(0, 2, 1, 3)
    k = jnp.pad(k, pad).astype(f32).transpose(0, 2, 1, 3)
    v = jnp.pad(v, pad).astype(f32).transpose(0, 2, 1, 3)
    kb = k.reshape(bsz, H, nb, BLK, hd)
    vb = v.reshape(bsz, H, nb, BLK, hd)
    scale = hd ** -0.5
    n_sel = min(MOBA_TOPK, nb - 1)
    nq = Lp // QC
    qc = q.reshape(bsz, H, nq, QC, hd).transpose(2, 0, 1, 3, 4)
    bi = jnp.arange(bsz)[:, None, None, None]
    hi = jnp.arange(H)[None, :, None, None]

    if n_sel > 0:
        k_mean = jnp.mean(kb, axis=3)
        qblk = jnp.arange(Lp) // BLK
        gate = jnp.einsum('bhtd,bhjd->bhtj', q, k_mean)
        past = jnp.arange(nb)[None, :] < qblk[:, None]
        gate = jnp.where(past, gate, NEG_INF)
        _, top_idx = lax.top_k(gate, n_sel)
        top_ok = top_idx < qblk[None, None, :, None]
        sel_c = top_idx.reshape(bsz, H, nq, QC, n_sel).transpose(2, 0, 1, 3, 4)
        ok_c = top_ok.reshape(bsz, H, nq, QC, n_sel).transpose(2, 0, 1, 3, 4)
    else:
        sel_c = jnp.zeros((nq, bsz, H, QC, 1), jnp.int32)
        ok_c = jnp.zeros((nq, bsz, H, QC, 1), bool)

    def attend(args):
        c, q_blk, sel, ok = args
        t = c * QC + jnp.arange(QC)
        own = (c * QC) // BLK
        k_own = lax.dynamic_index_in_dim(kb, own, axis=2, keepdims=False)
        v_own = lax.dynamic_index_in_dim(vb, own, axis=2, keepdims=False)
        kpos = own * BLK + jnp.arange(BLK)
        s_own = jnp.einsum('bhqd,bhkd->bhqk', q_blk, k_own) * scale
        s_own = jnp.where(kpos[None, :] <= t[:, None], s_own, NEG_INF)
        if n_sel > 0:
            k_sel = kb[bi, hi, sel]
            v_sel = vb[bi, hi, sel].reshape(bsz, H, QC, n_sel * BLK, hd)
            s_sel = jnp.einsum('bhqd,bhqnkd->bhqnk', q_blk, k_sel) * scale
            s_sel = jnp.where(ok[..., None], s_sel, NEG_INF).reshape(bsz, H, QC, n_sel * BLK)
            p = jax.nn.softmax(jnp.concatenate([s_sel, s_own], axis=-1), axis=-1)
            return (jnp.einsum('bhqk,bhqkd->bhqd', p[..., :n_sel * BLK], v_sel)
                    + jnp.einsum('bhqk,bhkd->bhqd', p[..., n_sel * BLK:], v_own))
        p = jax.nn.softmax(s_own, axis=-1)
        return jnp.einsum('bhqk,bhkd->bhqd', p, v_own)

    out = lax.map(attend, (jnp.arange(nq), qc, sel_c, ok_c))
    out = out.transpose(1, 0, 3, 2, 4).reshape(bsz, Lp, H * hd)[:, :L]
    return out.astype(out_dtype)


def conv_ffn(x, w_up, conv_w, conv_b, w_down):
    L = x.shape[1]
    h = x @ w_up
    hp = jnp.pad(h, ((0, 0), (CONV_WIDTH - 1, 0), (0, 0)))
    h = conv_b
    for j in range(CONV_WIDTH):
        h = h + hp[:, j:j + L] * conv_w[j]
    g, u = jnp.split(h, 2, axis=-1)
    return (jax.nn.silu(g) * u) @ w_down


def setup_inputs(seed: int = 0) -> dict:
    key = jax.random.key(seed)
    ks = jax.random.split(key, 24)
    f32 = jnp.float32
    Ld = DEPTH
    G, P, Hs = SSM_GROUPS, SSM_STATE, SSM_GROUP

    def nrm(k, shape, scale):
        return jax.random.normal(k, shape, f32) * scale

    return {
        "x": nrm(ks[0], (BATCH, SEQ, D_MODEL), 1.0),
        "norm1_g": 1.0 + nrm(ks[1], (Ld, D_MODEL), 0.02),
        "w_in": nrm(ks[2], (Ld, D_MODEL, N_IN), D_MODEL ** -0.5),
        "ssm_lambda_re": -0.5 + nrm(ks[3], (Ld, G, P), 0.01),
        "ssm_lambda_im": jnp.pi * jnp.arange(P, dtype=f32)[None, None, :] + nrm(ks[4], (Ld, G, P), 0.01),
        "ssm_log_dt": jax.random.uniform(ks[5], (Ld, G), f32, math.log(SSM_DT_MIN), math.log(SSM_DT_MAX)),
        "ssm_b_re": nrm(ks[6], (Ld, G, P, Hs), (2 * Hs) ** -0.5),
        "ssm_b_im": nrm(ks[7], (Ld, G, P, Hs), (2 * Hs) ** -0.5),
        "ssm_c_re": nrm(ks[8], (Ld, G, Hs, P), 0.5),
        "ssm_c_im": nrm(ks[9], (Ld, G, Hs, P), 0.5),
        "ssm_d": nrm(ks[10], (Ld, SSM_WIDTH), 1.0),
        "ssm_w_glu": nrm(ks[11], (Ld, SSM_WIDTH, SSM_WIDTH), SSM_WIDTH ** -0.5),
        "ret_gn_g": 1.0 + nrm(ks[12], (Ld, RET_V_WIDTH), 0.02),
        "attn_qn_g": 1.0 + nrm(ks[13], (Ld, ATT_HEAD_DIM), 0.02),
        "attn_kn_g": 1.0 + nrm(ks[14], (Ld, ATT_HEAD_DIM), 0.02),
        "w_br_ssm": nrm(ks[15], (Ld, SSM_WIDTH, D_MODEL), SSM_WIDTH ** -0.5),
        "w_br_ret": nrm(ks[16], (Ld, RET_V_WIDTH, D_MODEL), RET_V_WIDTH ** -0.5),
        "w_br_att": nrm(ks[17], (Ld, ATT_WIDTH, D_MODEL), ATT_WIDTH ** -0.5),
        "w_o": nrm(ks[18], (Ld, D_MODEL, D_MODEL), D_MODEL ** -0.5),
        "norm2_g": 1.0 + nrm(ks[19], (Ld, D_MODEL), 0.02),
        "ffn_w_up": nrm(ks[20], (Ld, D_MODEL, 2 * FFN_HIDDEN), D_MODEL ** -0.5),
        "ffn_conv_w": nrm(ks[21], (Ld, CONV_WIDTH, 2 * FFN_HIDDEN), CONV_WIDTH ** -0.5),
        "ffn_conv_b": nrm(ks[22], (Ld, 2 * FFN_HIDDEN), 0.01),
        "ffn_w_down": nrm(ks[23], (Ld, FFN_HIDDEN, D_MODEL), FFN_HIDDEN ** -0.5),
    }


def reference(x, norm1_g, w_in, ssm_lambda_re, ssm_lambda_im, ssm_log_dt, ssm_b_re, ssm_b_im,
              ssm_c_re, ssm_c_im, ssm_d, ssm_w_glu, ret_gn_g, attn_qn_g, attn_kn_g,
              w_br_ssm, w_br_ret, w_br_att, w_o, norm2_g, ffn_w_up, ffn_conv_w, ffn_conv_b,
              ffn_w_down):
    bsz, L, _ = x.shape
    offsets = [int(o) for o in np.cumsum(IN_SPLITS)[:-1]]
    for l in range(DEPTH):
        xn = rmsnorm(x, norm1_g[l])
        proj = xn @ w_in[l]
        (u_s, q_r, k_r, v_r, g_r, q_a, k_a, v_a, gates) = jnp.split(proj, offsets, axis=-1)
        y_s = s5_branch(u_s, ssm_lambda_re[l], ssm_lambda_im[l], ssm_log_dt[l], ssm_b_re[l],
                        ssm_b_im[l], ssm_c_re[l], ssm_c_im[l], ssm_d[l], ssm_w_glu[l])
        y_r = retention_branch(q_r.reshape(bsz, L, RET_HEADS, RET_QK_DIM),
                               k_r.reshape(bsz, L, RET_HEADS, RET_QK_DIM),
                               v_r.reshape(bsz, L, RET_HEADS, RET_V_DIM),
                               g_r, ret_gn_g[l])
        y_a = moba_branch(q_a.reshape(bsz, L, ATT_HEADS, ATT_HEAD_DIM),
                          k_a.reshape(bsz, L, ATT_HEADS, ATT_HEAD_DIM),
                          v_a.reshape(bsz, L, ATT_HEADS, ATT_HEAD_DIM),
                          attn_qn_g[l], attn_kn_g[l])
        g_s, g_rt, g_at = jnp.split(jax.nn.sigmoid(gates), N_BRANCH, axis=-1)
        merged = (g_s * (y_s @ w_br_ssm[l]) + g_rt * (y_r @ w_br_ret[l])
                  + g_at * (y_a @ w_br_att[l]))
        x = x + merged @ w_o[l]
        x = x + conv_ffn(rmsnorm(x, norm2_g[l]), ffn_w_up[l], ffn_conv_w[l], ffn_conv_b[l],
                         ffn_w_down[l])
    return x
```

```python
import functools
import math

import numpy as np
import jax
import jax.numpy as jnp
from jax import lax
from jax.experimental import pallas as pl
from jax.experimental.pallas import tpu as pltpu

F32 = jnp.float32
BF16 = jnp.bfloat16

D_MODEL = 1024
SSM_WIDTH = 512
SSM_GROUP = 16
SSM_GROUPS = 32
SSM_STATE = 64
SSM_NSTATE = SSM_GROUPS * SSM_STATE
RET_HEADS = 4
RET_QK_DIM = 64
RET_V_DIM = 128
RET_QK_WIDTH = 256
RET_V_WIDTH = 512
RET_THETA = 10000.0
ATT_HEADS = 8
ATT_HEAD_DIM = 64
ATT_WIDTH = 512
MOBA_BLOCK = 256
MOBA_TOPK = 3
ROPE_THETA = 500000.0
ROPE_HALF = 8
FFN_HIDDEN = 2816
CONV_WIDTH = 3
NORM_EPS = 1e-6
NEG_INF = -1e30

N_MAIN = 2048 + 3 * D_MODEL
N_ATT = 3 * ATT_WIDTH

SUBLANES = 8
LANES = 128
VMEM_LIMIT = 52 * 1024 * 1024

TM_PROJ = 512
TN_PROJ = 1024
LS_S5 = 256
RET_CHUNK = 128
TM_MERGE = 512
TM_FFN = 512
FC_FFN = 1408
HALO = SUBLANES


def _cparams(n_axes):
    return pltpu.CompilerParams(dimension_semantics=("arbitrary",) * n_axes,
                                vmem_limit_bytes=VMEM_LIMIT)


def _sigmoid(x):
    return 1.0 / (1.0 + jnp.exp(-x))


def _rms_rows(x, g):
    ms = jnp.mean(x * x, axis=-1, keepdims=True)
    return x * lax.rsqrt(ms + NORM_EPS) * g


def _s5_prep_kernel(lre_ref, lim_ref, ldt_ref, bre_ref, bim_ref, bmat_ref, coef_ref):
    lre = lre_ref[...]
    lim = lim_ref[...]
    dt = jnp.exp(ldt_ref[...])

    def lam_pow(k):
        mag = jnp.exp(k * lre * dt)
        ang = k * lim * dt
        return mag * jnp.cos(ang), mag * jnp.sin(ang)

    ar, ai = lam_pow(1.0)
    x = ar - 1.0
    den = lre * lre + lim * lim
    f_re = (x * lre + ai * lim) / den
    f_im = (ai * lre - x * lim) / den
    bre = bre_ref[...]
    bim = bim_ref[...]
    bmat_ref[:, :SSM_NSTATE] = (f_re * bre - f_im * bim).astype(BF16)
    bmat_ref[:, SSM_NSTATE:] = (f_re * bim + f_im * bre).astype(BF16)

    row = lax.broadcasted_iota(jnp.int32, (SUBLANES, SSM_NSTATE), 0)
    for idx, k in enumerate((1, 2, 4)):
        pr, pi = lam_pow(float(k))
        keep = row >= k
        coef_ref[idx * 16:idx * 16 + 8, :] = jnp.where(keep, jnp.broadcast_to(pr, keep.shape), 0.0)
        coef_ref[idx * 16 + 8:idx * 16 + 16, :] = jnp.where(keep, jnp.broadcast_to(pi, keep.shape), 0.0)
    pr, pi = lam_pow((row + 1).astype(F32))
    coef_ref[48:56, :] = pr
    coef_ref[56:64, :] = pi


def _s5_prep(lam_re, lam_im, log_dt, b_re, b_im):
    eye = jnp.eye(SSM_GROUPS, dtype=F32)
    bre_d = jnp.einsum('gph,gk->ghkp', b_re, eye).reshape(SSM_WIDTH, SSM_NSTATE)
    bim_d = jnp.einsum('gph,gk->ghkp', b_im, eye).reshape(SSM_WIDTH, SSM_NSTATE)
    ldt = jnp.repeat(log_dt, SSM_STATE).reshape(1, SSM_NSTATE)
    return pl.pallas_call(
        _s5_prep_kernel,
        out_shape=(jax.ShapeDtypeStruct((SSM_WIDTH, 2 * SSM_NSTATE), BF16),
                   jax.ShapeDtypeStruct((64, SSM_NSTATE), F32)),
        compiler_params=pltpu.CompilerParams(vmem_limit_bytes=VMEM_LIMIT),
        name="s5_prep",
    )(lam_re.reshape(1, SSM_NSTATE), lam_im.reshape(1, SSM_NSTATE), ldt, bre_d, bim_d)


def _in_proj_kernel(x_ref, g_ref, wm_ref, wat_ref, proj_ref, att_ref, xn_ref):
    @pl.when(pl.program_id(1) == 0)
    def _():
        xn = _rms_rows(x_ref[...], g_ref[...]).astype(BF16)
        xn_ref[...] = xn
        att_ref[...] = lax.dot_general(wat_ref[...], xn, (((1,), (1,)), ((), ())),
                                       preferred_element_type=F32)

    proj_ref[...] = jnp.dot(xn_ref[...], wm_ref[...], preferred_element_type=F32)


def _in_proj(x2d, g, w_main, w_att_t):
    t = x2d.shape[0]
    return pl.pallas_call(
        _in_proj_kernel,
        grid=(t // TM_PROJ, N_MAIN // TN_PROJ),
        in_specs=[
            pl.BlockSpec((TM_PROJ, D_MODEL), lambda i, j: (i, 0)),
            pl.BlockSpec((1, D_MODEL), lambda i, j: (0, 0)),
            pl.BlockSpec((D_MODEL, TN_PROJ), lambda i, j: (0, j)),
            pl.BlockSpec((N_ATT, D_MODEL), lambda i, j: (0, 0)),
        ],
        out_specs=(
            pl.BlockSpec((TM_PROJ, TN_PROJ), lambda i, j: (i, j)),
            pl.BlockSpec((N_ATT, TM_PROJ), lambda i, j: (0, i)),
        ),
        out_shape=(jax.ShapeDtypeStruct((t, N_MAIN), F32),
                   jax.ShapeDtypeStruct((N_ATT, t), F32)),
        scratch_shapes=[pltpu.VMEM((TM_PROJ, D_MODEL), BF16)],
        compiler_params=_cparams(2),
        name="in_proj",
    )(x2d, g, w_main, w_att_t)


def _s5_kernel(u_ref, bmat_ref, coef_ref, cmat_ref, d_ref, wglu_ref, o_ref, st_ref):
    ls = u_ref.shape[0]

    @pl.when(pl.program_id(1) == 0)
    def _():
        st_ref[0:HALO, :] = jnp.zeros((HALO, 2 * SSM_NSTATE), F32)

    u = u_ref[...]
    st_ref[HALO:, :] = jnp.dot(u.astype(BF16), bmat_ref[...], preferred_element_type=F32)

    def row_group(r, carry):
        row0 = pl.multiple_of(HALO + r * SUBLANES, SUBLANES)
        for cb in range(SSM_NSTATE // LANES):
            cre = slice(cb * LANES, (cb + 1) * LANES)
            cim = slice(SSM_NSTATE + cb * LANES, SSM_NSTATE + (cb + 1) * LANES)
            xr = st_ref[pl.ds(row0, SUBLANES), cre]
            xi = st_ref[pl.ds(row0, SUBLANES), cim]
            for idx, k in enumerate((1, 2, 4)):
                ar = coef_ref[idx * 16:idx * 16 + 8, cre]
                ai = coef_ref[idx * 16 + 8:idx * 16 + 16, cre]
                sr = pltpu.roll(xr, k, 0)
                si = pltpu.roll(xi, k, 0)
                xr, xi = xr + (ar * sr - ai * si), xi + (ar * si + ai * sr)
            prev0 = pl.multiple_of(row0 - SUBLANES, SUBLANES)
            cr = jnp.broadcast_to(st_ref[pl.ds(prev0, SUBLANES), cre][SUBLANES - 1:], (SUBLANES, LANES))
            ci = jnp.broadcast_to(st_ref[pl.ds(prev0, SUBLANES), cim][SUBLANES - 1:], (SUBLANES, LANES))
            pr = coef_ref[48:56, cre]
            pi = coef_ref[56:64, cre]
            st_ref[pl.ds(row0, SUBLANES), cre] = xr + (pr * cr - pi * ci)
            st_ref[pl.ds(row0, SUBLANES), cim] = xi + (pr * ci + pi * cr)
        return carry

    lax.fori_loop(0, ls // SUBLANES, row_group, 0)
    st_ref[HALO - 1:HALO, :] = st_ref[HALO + ls - 1:HALO + ls, :]

    y = jnp.dot(st_ref[HALO:, :].astype(BF16), cmat_ref[...], preferred_element_type=F32)
    y = y + d_ref[...] * u
    y = 0.5 * y * (1.0 + jnp.tanh(0.7978845608028654 * (y + 0.044715 * (y * y * y))))
    glu = jnp.dot(y.astype(BF16), wglu_ref[...], preferred_element_type=F32)
    o_ref[...] = y * _sigmoid(glu)


def _s5(proj, bmat, coef, cmat, d_skip, w_glu, bsz, seq):
    nchunk = seq // LS_S5
    return pl.pallas_call(
        _s5_kernel,
        grid=(bsz, nchunk),
        in_specs=[
            pl.BlockSpec((LS_S5, SSM_WIDTH), lambda b, c: (b * nchunk + c, 0)),
            pl.BlockSpec((SSM_WIDTH, 2 * SSM_NSTATE), lambda b, c: (0, 0)),
            pl.BlockSpec((64, SSM_NSTATE), lambda b, c: (0, 0)),
            pl.BlockSpec((2 * SSM_NSTATE, SSM_WIDTH), lambda b, c: (0, 0)),
            pl.BlockSpec((1, SSM_WIDTH), lambda b, c: (0, 0)),
            pl.BlockSpec((SSM_WIDTH, SSM_WIDTH), lambda b, c: (0, 0)),
        ],
        out_specs=pl.BlockSpec((LS_S5, SSM_WIDTH), lambda b, c: (b * nchunk + c, 0)),
        out_shape=jax.ShapeDtypeStruct((bsz * seq, SSM_WIDTH), F32),
        scratch_shapes=[pltpu.VMEM((HALO + LS_S5, 2 * SSM_NSTATE), F32)],
        compiler_params=_cparams(2),
        name="s5",
    )(proj, bmat, coef, cmat, d_skip, w_glu)


def _ret_tables(seq):
    c = RET_CHUNK
    half = RET_QK_DIM // 2
    inv = 1.0 / (RET_THETA ** np.linspace(0.0, 1.0, half))
    ang = np.arange(seq)[:, None] * inv[None, :]
    cos, sin = np.cos(ang), np.sin(ang)
    cos_full = np.tile(np.concatenate([cos, cos], axis=1), (1, RET_HEADS))
    sin_sgn = np.tile(np.concatenate([-sin, sin], axis=1), (1, RET_HEADS))
    log_gamma = np.log1p(-np.exp2(-5.0 - np.arange(RET_HEADS)))
    pos = np.arange(c)
    rel = pos[:, None] - pos[None, :]
    decay = np.where(rel >= 0, np.exp(log_gamma[:, None, None] * np.maximum(rel, 0)[None]), 0.0)
    zeta = np.exp(log_gamma[None, :] * (c - 1.0 - pos)[:, None])
    xi = np.exp(log_gamma[None, :] * (pos + 1.0)[:, None])
    zeta = np.repeat(zeta, RET_QK_DIM, axis=1)
    xi = np.repeat(xi, RET_QK_DIM, axis=1)
    cdec = np.repeat(np.exp(log_gamma * c), RET_V_DIM)[None, :]
    as32 = lambda a: jnp.asarray(a, dtype=F32)
    return as32(cos_full), as32(sin_sgn), as32(decay), as32(zeta), as32(xi), as32(cdec)


def _ret_kernel(q_ref, k_ref, v_ref, g_ref, cos_ref, sin_ref, decay_ref, zeta_ref, xi_ref,
                cdec_ref, gn_ref, o_ref, state_ref):
    @pl.when(pl.program_id(1) == 0)
    def _():
        state_ref[...] = jnp.zeros(state_ref.shape, F32)

    cos = cos_ref[...]
    sin = sin_ref[...]
    lane = lax.broadcasted_iota(jnp.int32, cos.shape, 1)
    first_half = (lane % RET_QK_DIM) < (RET_QK_DIM // 2)

    def rot(x):
        swapped = jnp.where(first_half, pltpu.roll(x, RET_QK_WIDTH - RET_QK_DIM // 2, 1),
                            pltpu.roll(x, RET_QK_DIM // 2, 1))
        return x * cos + swapped * sin

    q = rot(q_ref[...])
    k = rot(k_ref[...]) * (RET_QK_DIM ** -0.5)
    qx = (q * xi_ref[...]).astype(BF16)
    kz = (k * zeta_ref[...]).astype(BF16)
    qb = q.astype(BF16)
    kb = k.astype(BF16)
    vb = v_ref[...].astype(BF16)
    g = g_ref[...]
    gn = gn_ref[...]
    cdec = cdec_ref[...]
    for h in range(RET_HEADS):
        qs = slice(h * RET_QK_DIM, (h + 1) * RET_QK_DIM)
        vs = slice(h * RET_V_DIM, (h + 1) * RET_V_DIM)
        s = lax.dot_general(qb[:, qs], kb[:, qs], (((1,), (1,)), ((), ())),
                            preferred_element_type=F32) * decay_ref[h]
        state = state_ref[h]
        o = (jnp.dot(s.astype(BF16), vb[:, vs], preferred_element_type=F32)
             + jnp.dot(qx[:, qs], state.astype(BF16), preferred_element_type=F32))
        kv = lax.dot_general(kz[:, qs], vb[:, vs], (((0,), (0,)), ((), ())),
                             preferred_element_type=F32)
        state_ref[h] = cdec[:, vs] * state + kv
        mu = jnp.mean(o, axis=-1, keepdims=True)
        oc = o - mu
        var = jnp.mean(oc * oc, axis=-1, keepdims=True)
        on = oc * lax.rsqrt(var + NORM_EPS) * gn[:, vs]
        gh = g[:, vs]
        o_ref[:, vs] = gh * _sigmoid(gh) * on


def _retention(proj, gn_g, tables, bsz, seq):
    c = RET_CHUNK
    nchunk = seq // c
    cos_full, sin_sgn, decay, zeta, xi, cdec = tables
    tok = lambda b, n: b * nchunk + n
    return pl.pallas_call(
        _ret_kernel,
        grid=(bsz, nchunk),
        in_specs=[
            pl.BlockSpec((c, RET_QK_WIDTH), lambda b, n: (tok(b, n), 2)),
            pl.BlockSpec((c, RET_QK_WIDTH), lambda b, n: (tok(b, n), 3)),
            pl.BlockSpec((c, RET_V_WIDTH), lambda b, n: (tok(b, n), 2)),
            pl.BlockSpec((c, RET_V_WIDTH), lambda b, n: (tok(b, n), 3)),
            pl.BlockSpec((c, RET_QK_WIDTH), lambda b, n: (n, 0)),
            pl.BlockSpec((c, RET_QK_WIDTH), lambda b, n: (n, 0)),
            pl.BlockSpec((RET_HEADS, c, c), lambda b, n: (0, 0, 0)),
            pl.BlockSpec((c, RET_QK_WIDTH), lambda b, n: (0, 0)),
            pl.BlockSpec((c, RET_QK_WIDTH), lambda b, n: (0, 0)),
            pl.BlockSpec((1, RET_V_WIDTH), lambda b, n: (0, 0)),
            pl.BlockSpec((1, RET_V_WIDTH), lambda b, n: (0, 0)),
        ],
        out_specs=pl.BlockSpec((c, RET_V_WIDTH), lambda b, n: (tok(b, n), 0)),
        out_shape=jax.ShapeDtypeStruct((bsz * seq, RET_V_WIDTH), F32),
        scratch_shapes=[pltpu.VMEM((RET_HEADS, RET_QK_DIM, RET_V_DIM), F32)],
        compiler_params=_cparams(2),
        name="retention",
    )(proj, proj, proj, proj, cos_full, sin_sgn, decay, zeta, xi, cdec, gn_g)


def _moba_tables(seq):
    inv = ROPE_THETA ** (-np.arange(ROPE_HALF) / ROPE_HALF)
    ang = inv[:, None] * np.arange(seq)[None, :]
    return jnp.asarray(np.cos(ang), dtype=F32), jnp.asarray(np.sin(ang), dtype=F32)


def _moba_prep_kernel(att_ref, qg_ref, kg_ref, cos_ref, sin_ref,
                      qt_ref, kn_ref, vt_ref, sel_ref, kmean_ref):
    i = pl.program_id(1)
    nb = kmean_ref.shape[1]

    @pl.when(i == 0)
    def _():
        kmean_ref[...] = jnp.zeros(kmean_ref.shape, F32)

    cos = cos_ref[...]
    sin = sin_ref[...]

    def norm_rot(x, g):
        ms = jnp.mean(x * x, axis=0, keepdims=True)
        xn = x * lax.rsqrt(ms + NORM_EPS) * g
        x1 = xn[0:ROPE_HALF]
        x2 = xn[ROPE_HALF:2 * ROPE_HALF]
        return jnp.concatenate([x1 * cos - x2 * sin, x1 * sin + x2 * cos, xn[2 * ROPE_HALF:]], axis=0)

    row = lax.broadcasted_iota(jnp.int32, (nb, MOBA_BLOCK), 0)
    past = row < i
    for h in range(ATT_HEADS):
        hs = slice(h * ATT_HEAD_DIM, (h + 1) * ATT_HEAD_DIM)
        q = norm_rot(att_ref[hs, :], qg_ref[...])
        k = norm_rot(att_ref[ATT_WIDTH + h * ATT_HEAD_DIM:ATT_WIDTH + (h + 1) * ATT_HEAD_DIM, :], kg_ref[...])
        kn = k.T
        kn_ref[0, h] = kn.astype(BF16)
        qt_ref[0, hs, :] = (q * (ATT_HEAD_DIM ** -0.5)).astype(BF16)
        vt_ref[0, hs, :] = att_ref[2 * ATT_WIDTH + h * ATT_HEAD_DIM:2 * ATT_WIDTH + (h + 1) * ATT_HEAD_DIM, :].astype(BF16)
        kmean_ref[h, pl.ds(i, 1), :] = jnp.mean(kn, axis=0, keepdims=True)
        gate = jnp.dot(kmean_ref[h], q, preferred_element_type=F32, precision=lax.Precision.HIGHEST)
        gate = jnp.where(past, gate, NEG_INF)
        beaten = jnp.zeros(gate.shape, F32)
        for j in range(nb):
            gj = jnp.broadcast_to(gate[j:j + 1, :], gate.shape)
            ahead = jnp.where(gj > gate, 1.0, jnp.where(gj == gate, jnp.where(row > j, 1.0, 0.0), 0.0))
            beaten = beaten + ahead
        sel_ref[0, h] = jnp.where(past, jnp.where(beaten < MOBA_TOPK, 1.0, 0.0), 0.0)


def _moba_prep(att_t, qn_g, kn_g, tables, bsz, seq):
    nb = seq // MOBA_BLOCK
    cos_t, sin_t = tables
    return pl.pallas_call(
        _moba_prep_kernel,
        grid=(bsz, nb),
        in_specs=[
            pl.BlockSpec((N_ATT, MOBA_BLOCK), lambda b, i: (0, b * nb + i)),
            pl.BlockSpec((ATT_HEAD_DIM, 1), lambda b, i: (0, 0)),
            pl.BlockSpec((ATT_HEAD_DIM, 1), lambda b, i: (0, 0)),
            pl.BlockSpec((ROPE_HALF, MOBA_BLOCK), lambda b, i: (0, i)),
            pl.BlockSpec((ROPE_HALF, MOBA_BLOCK), lambda b, i: (0, i)),
        ],
        out_specs=(
            pl.BlockSpec((1, ATT_WIDTH, MOBA_BLOCK), lambda b, i: (b, 0, i)),
            pl.BlockSpec((1, ATT_HEADS, MOBA_BLOCK, ATT_HEAD_DIM), lambda b, i: (b, 0, i, 0)),
            pl.BlockSpec((1, ATT_WIDTH, MOBA_BLOCK), lambda b, i: (b, 0, i)),
            pl.BlockSpec((1, ATT_HEADS, nb, MOBA_BLOCK), lambda b, i: (b, 0, 0, i)),
        ),
        out_shape=(
            jax.ShapeDtypeStruct((bsz, ATT_WIDTH, seq), BF16),
            jax.ShapeDtypeStruct((bsz, ATT_HEADS, seq, ATT_HEAD_DIM), BF16),
            jax.ShapeDtypeStruct((bsz, ATT_WIDTH, seq), BF16),
            jax.ShapeDtypeStruct((bsz, ATT_HEADS, nb, seq), F32),
        ),
        scratch_shapes=[pltpu.VMEM((ATT_HEADS, nb, ATT_HEAD_DIM), F32)],
        compiler_params=_cparams(2),
        name="moba_prep",
    )(att_t, qn_g.reshape(ATT_HEAD_DIM, 1), kn_g.reshape(ATT_HEAD_DIM, 1), cos_t, sin_t)


def _moba_attn_kernel(qt_ref, k_ref, vt_ref, sel_ref, o_ref):
    blk = MOBA_BLOCK
    nb = sel_ref.shape[2]
    kpos = lax.broadcasted_iota(jnp.int32, (blk, blk), 0)
    qpos = lax.broadcasted_iota(jnp.int32, (blk, blk), 1)
    causal = kpos <= qpos

    def q_block(i, carry):
        q0 = pl.multiple_of(i * blk, blk)
        qt = qt_ref[0, :, pl.ds(q0, blk)]

        def scores(j0):
            return jnp.dot(k_ref[0, 0, pl.ds(j0, blk), :], qt, preferred_element_type=F32)

        s = jnp.where(causal, scores(q0), NEG_INF)
        m = jnp.max(s, axis=0, keepdims=True)
        p = jnp.exp(s - m)
        l = jnp.sum(p, axis=0, keepdims=True)
        acc = jnp.dot(vt_ref[0, :, pl.ds(q0, blk)], p.astype(BF16), preferred_element_type=F32)

        def past_block(j, st):
            m, l, acc = st
            j0 = pl.multiple_of(j * blk, blk)
            picked = sel_ref[0, 0, pl.ds(j, 1), pl.ds(q0, blk)] > 0.5
            s = jnp.where(picked, scores(j0), NEG_INF)
            m_new = jnp.maximum(m, jnp.max(s, axis=0, keepdims=True))
            alpha = jnp.exp(m - m_new)
            p = jnp.exp(s - m_new)
            l = alpha * l + jnp.sum(p, axis=0, keepdims=True)
            acc = alpha * acc + jnp.dot(vt_ref[0, :, pl.ds(j0, blk)], p.astype(BF16),
                                        preferred_element_type=F32)
            return m_new, l, acc

        m, l, acc = lax.fori_loop(0, i, past_block, (m, l, acc))
        o_ref[0, :, pl.ds(q0, blk)] = acc / l
        return carry

    lax.fori_loop(0, nb, q_block, 0)


def _moba_attn(qt, kn, vt, sel):
    bsz, _, seq = qt.shape
    nb = seq // MOBA_BLOCK
    return pl.pallas_call(
        _moba_attn_kernel,
        grid=(bsz, ATT_HEADS),
        in_specs=[
            pl.BlockSpec((1, ATT_HEAD_DIM, seq), lambda b, h: (b, h, 0)),
            pl.BlockSpec((1, 1, seq, ATT_HEAD_DIM), lambda b, h: (b, h, 0, 0)),
            pl.BlockSpec((1, ATT_HEAD_DIM, seq), lambda b, h: (b, h, 0)),
            pl.BlockSpec((1, 1, nb, seq), lambda b, h: (b, h, 0, 0)),
        ],
        out_specs=pl.BlockSpec((1, ATT_HEAD_DIM, seq), lambda b, h: (b, h, 0)),
        out_shape=jax.ShapeDtypeStruct((bsz, ATT_WIDTH, seq), F32),
        compiler_params=_cparams(2),
        name="moba_attn",
    )(qt, kn, vt, sel)


def _merge_kernel(x_ref, ys_ref, yr_ref, at_ref, gs_ref, gr_ref, ga_ref,
                  ws_ref, wr_ref, wa_ref, wo_ref, o_ref):
    ps = jnp.dot(ys_ref[...].astype(BF16), ws_ref[...], preferred_element_type=F32)
    pr = jnp.dot(yr_ref[...].astype(BF16), wr_ref[...], preferred_element_type=F32)
    pa = lax.dot_general(at_ref[0].astype(BF16), wa_ref[...], (((0,), (0,)), ((), ())),
                         preferred_element_type=F32)
    merged = _sigmoid(gs_ref[...]) * ps + _sigmoid(gr_ref[...]) * pr + _sigmoid(ga_ref[...]) * pa
    o_ref[...] = x_ref[...] + jnp.dot(merged.astype(BF16), wo_ref[...], preferred_element_type=F32)


def _merge(x2d, y_s, y_r, o_t, proj, w_s, w_r, w_a, w_o, bsz, seq):
    tm = TM_MERGE
    per_seq = seq // tm
    row = lambda i: (i, 0)
    const = lambda i: (0, 0)
    return pl.pallas_call(
        _merge_kernel,
        grid=(bsz * per_seq,),
        in_specs=[
            pl.BlockSpec((tm, D_MODEL), row),
            pl.BlockSpec((tm, SSM_WIDTH), row),
            pl.BlockSpec((tm, RET_V_WIDTH), row),
            pl.BlockSpec((1, ATT_WIDTH, tm), lambda i: (i // per_seq, 0, i % per_seq)),
            pl.BlockSpec((tm, D_MODEL), lambda i: (i, 2)),
            pl.BlockSpec((tm, D_MODEL), lambda i: (i, 3)),
            pl.BlockSpec((tm, D_MODEL), lambda i: (i, 4)),
            pl.BlockSpec((SSM_WIDTH, D_MODEL), const),
            pl.BlockSpec((RET_V_WIDTH, D_MODEL), const),
            pl.BlockSpec((ATT_WIDTH, D_MODEL), const),
            pl.BlockSpec((D_MODEL, D_MODEL), const),
        ],
        out_specs=pl.BlockSpec((tm, D_MODEL), row),
        out_shape=jax.ShapeDtypeStruct(x2d.shape, F32),
        compiler_params=_cparams(1),
        name="merge",
    )(x2d, y_s, y_r, o_t, proj, proj, proj, w_s, w_r, w_a, w_o)


def _ffn_kernel(x_ref, halo_ref, g_ref, wg_ref, wu_ref, cwg_ref, cwu_ref, cbg_ref, cbu_ref,
                wd_ref, o_ref, xn_ref, acc_ref, *, tiles_per_seq):
    i = pl.program_id(0)
    c = pl.program_id(1)
    tm = x_ref.shape[0]

    @pl.when(c == 0)
    def _():
        g = g_ref[...]
        xn_ref[0:HALO, :] = _rms_rows(halo_ref[...], g).astype(BF16)
        xn_ref[HALO:, :] = _rms_rows(x_ref[...], g).astype(BF16)
        acc_ref[...] = jnp.zeros(acc_ref.shape, F32)

    rowi = lax.broadcasted_iota(jnp.int32, (tm + HALO, 1), 0)
    live = jnp.logical_or(rowi >= HALO, i % tiles_per_seq != 0)

    def conv(w_ref, cw_ref, cb_ref):
        h = jnp.dot(xn_ref[...], w_ref[...], preferred_element_type=F32)
        h = jnp.where(live, h, 0.0)
        cw = cw_ref[...]
        return (cb_ref[...] + h[HALO - 2:tm + HALO - 2] * cw[0:1] + h[HALO - 1:tm + HALO - 1] * cw[1:2]
                + h[HALO:] * cw[2:3])

    hg = conv(wg_ref, cwg_ref, cbg_ref)
    hu = conv(wu_ref, cwu_ref, cbu_ref)
    act = (hg * _sigmoid(hg) * hu).astype(BF16)
    acc_ref[...] += jnp.dot(act, wd_ref[...], preferred_element_type=F32)

    @pl.when(c == pl.num_programs(1) - 1)
    def _():
        o_ref[...] = x_ref[...] + acc_ref[...]


def _ffn(x2d, g, w_up, conv_w, conv_b, w_down, seq):
    t = x2d.shape[0]
    tm, fc = TM_FFN, FC_FFN
    nfc = FFN_HIDDEN // fc
    halo_blocks = tm // HALO
    return pl.pallas_call(
        functools.partial(_ffn_kernel, tiles_per_seq=seq // tm),
        grid=(t // tm, nfc),
        in_specs=[
            pl.BlockSpec((tm, D_MODEL), lambda i, c: (i, 0)),
            pl.BlockSpec((HALO, D_MODEL), lambda i, c: (jnp.maximum(i * halo_blocks - 1, 0), 0)),
            pl.BlockSpec((1, D_MODEL), lambda i, c: (0, 0)),
            pl.BlockSpec((D_MODEL, fc), lambda i, c: (0, c)),
            pl.BlockSpec((D_MODEL, fc), lambda i, c: (0, c + nfc)),
            pl.BlockSpec((CONV_WIDTH, fc), lambda i, c: (0, c)),
            pl.BlockSpec((CONV_WIDTH, fc), lambda i, c: (0, c + nfc)),
            pl.BlockSpec((1, fc), lambda i, c: (0, c)),
            pl.BlockSpec((1, fc), lambda i, c: (0, c + nfc)),
            pl.BlockSpec((fc, D_MODEL), lambda i, c: (c, 0)),
        ],
        out_specs=pl.BlockSpec((tm, D_MODEL), lambda i, c: (i, 0)),
        out_shape=jax.ShapeDtypeStruct(x2d.shape, F32),
        scratch_shapes=[pltpu.VMEM((tm + HALO, D_MODEL), BF16), pltpu.VMEM((tm, D_MODEL), F32)],
        compiler_params=_cparams(2),
        name="ffn",
    )(x2d, x2d, g, w_up, w_up, conv_w, conv_w, conv_b, conv_b, w_down)


def kernel(x, norm1_g, w_in, ssm_lambda_re, ssm_lambda_im, ssm_log_dt, ssm_b_re, ssm_b_im, ssm_c_re, ssm_c_im, ssm_d, ssm_w_glu, ret_gn_g, attn_qn_g, attn_kn_g, w_br_ssm, w_br_ret, w_br_att, w_o, norm2_g, ffn_w_up, ffn_conv_w, ffn_conv_b, ffn_w_down):
    bsz, seq, _ = x.shape
    depth = w_in.shape[0]
    assert seq % MOBA_BLOCK == 0 and seq % TM_PROJ == 0 and seq % TM_FFN == 0
    ret_tables = _ret_tables(seq)
    moba_tables = _moba_tables(seq)
    eye = jnp.eye(SSM_GROUPS, dtype=F32)
    att0 = 2048
    x2d = x.reshape(bsz * seq, D_MODEL)
    for l in range(depth):
        w = w_in[l]
        w_main = jnp.concatenate([w[:, :att0], w[:, att0 + N_ATT:]], axis=1).astype(BF16)
        w_att_t = w[:, att0:att0 + N_ATT].T.astype(BF16)
        cre_d = jnp.einsum('ghp,gk->gpkh', ssm_c_re[l], eye).reshape(SSM_NSTATE, SSM_WIDTH)
        cim_d = jnp.einsum('ghp,gk->gpkh', ssm_c_im[l], eye).reshape(SSM_NSTATE, SSM_WIDTH)
        cmat = jnp.concatenate([cre_d, -cim_d], axis=0).astype(BF16)

        bmat, coef = _s5_prep(ssm_lambda_re[l], ssm_lambda_im[l], ssm_log_dt[l], ssm_b_re[l], ssm_b_im[l])
        proj, att_t = _in_proj(x2d, norm1_g[l].reshape(1, D_MODEL), w_main, w_att_t)
        y_s = _s5(proj, bmat, coef, cmat, ssm_d[l].reshape(1, SSM_WIDTH), ssm_w_glu[l].astype(BF16), bsz, seq)
        y_r = _retention(proj, ret_gn_g[l].reshape(1, RET_V_WIDTH), ret_tables, bsz, seq)
        qt, kn, vt, sel = _moba_prep(att_t, attn_qn_g[l], attn_kn_g[l], moba_tables, bsz, seq)
        o_t = _moba_attn(qt, kn, vt, sel)
        x2d = _merge(x2d, y_s, y_r, o_t, proj, w_br_ssm[l].astype(BF16), w_br_ret[l].astype(BF16),
                     w_br_att[l].astype(BF16), w_o[l].astype(BF16), bsz, seq)
        x2d = _ffn(x2d, norm2_g[l].reshape(1, D_MODEL), ffn_w_up[l].astype(BF16), ffn_conv_w[l],
                   ffn_conv_b[l].reshape(1, 2 * FFN_HIDDEN), ffn_w_down[l].astype(BF16), seq)
    return x2d.reshape(bsz, seq, D_MODEL)
```

```python
import functools
import math

import numpy as np
import jax
import jax.numpy as jnp
from jax import lax
from jax.experimental import pallas as pl
from jax.experimental.pallas import tpu as pltpu

F32 = jnp.float32
BF16 = jnp.bfloat16

D_MODEL = 1024
SSM_WIDTH = 512
SSM_GROUP = 16
SSM_GROUPS = 32
SSM_STATE = 64
SSM_NSTATE = SSM_GROUPS * SSM_STATE
RET_HEADS = 4
RET_QK_DIM = 64
RET_V_DIM = 128
RET_QK_WIDTH = 256
RET_V_WIDTH = 512
RET_THETA = 10000.0
ATT_HEADS = 8
ATT_HEAD_DIM = 64
ATT_WIDTH = 512
MOBA_BLOCK = 256
MOBA_TOPK = 3
ROPE_THETA = 500000.0
ROPE_HALF = 8
FFN_HIDDEN = 2816
CONV_WIDTH = 3
NORM_EPS = 1e-6
NEG_INF = -1e30

N_MAIN = 2048 + 3 * D_MODEL
N_ATT = 3 * ATT_WIDTH

SUBLANES = 8
LANES = 128
VMEM_LIMIT = 52 * 1024 * 1024

TM_PROJ = 512
TN_PROJ = 1024
LS_S5 = 256
RET_CHUNK = 128
TM_MERGE = 512
TM_FFN = 512
FC_FFN = 1408
HALO = SUBLANES
ATT_HEADS_PER_STEP = 4
ATT_Q_SCALE = math.log2(math.e) * ATT_HEAD_DIM ** -0.5


def _cparams(n_axes):
    return pltpu.CompilerParams(dimension_semantics=("arbitrary",) * n_axes,
                                vmem_limit_bytes=VMEM_LIMIT)


def _sigmoid(x):
    return 1.0 / (1.0 + jnp.exp(-x))


def _rms_rows(x, g):
    ms = jnp.mean(x * x, axis=-1, keepdims=True)
    return x * lax.rsqrt(ms + NORM_EPS) * g


def _s5_prep_kernel(lre_ref, lim_ref, ldt_ref, bre_ref, bim_ref, bmat_ref, coef_ref):
    lre = lre_ref[...]
    lim = lim_ref[...]
    dt = jnp.exp(ldt_ref[...])

    def lam_pow(k):
        mag = jnp.exp(k * lre * dt)
        ang = k * lim * dt
        return mag * jnp.cos(ang), mag * jnp.sin(ang)

    ar, ai = lam_pow(1.0)
    x = ar - 1.0
    den = lre * lre + lim * lim
    f_re = (x * lre + ai * lim) / den
    f_im = (ai * lre - x * lim) / den
    bre = bre_ref[...]
    bim = bim_ref[...]
    bmat_ref[:, :SSM_NSTATE] = (f_re * bre - f_im * bim).astype(BF16)
    bmat_ref[:, SSM_NSTATE:] = (f_re * bim + f_im * bre).astype(BF16)

    row = lax.broadcasted_iota(jnp.int32, (SUBLANES, SSM_NSTATE), 0)
    for idx, k in enumerate((1, 2, 4)):
        pr, pi = lam_pow(float(k))
        keep = row >= k
        coef_ref[idx * 16:idx * 16 + 8, :] = jnp.where(keep, jnp.broadcast_to(pr, keep.shape), 0.0)
        coef_ref[idx * 16 + 8:idx * 16 + 16, :] = jnp.where(keep, jnp.broadcast_to(pi, keep.shape), 0.0)
    pr, pi = lam_pow((row + 1).astype(F32))
    coef_ref[48:56, :] = pr
    coef_ref[56:64, :] = pi


def _s5_prep(lam_re, lam_im, log_dt, b_re, b_im):
    eye = jnp.eye(SSM_GROUPS, dtype=F32)
    bre_d = jnp.einsum('gph,gk->ghkp', b_re, eye).reshape(SSM_WIDTH, SSM_NSTATE)
    bim_d = jnp.einsum('gph,gk->ghkp', b_im, eye).reshape(SSM_WIDTH, SSM_NSTATE)
    ldt = jnp.repeat(log_dt, SSM_STATE).reshape(1, SSM_NSTATE)
    return pl.pallas_call(
        _s5_prep_kernel,
        out_shape=(jax.ShapeDtypeStruct((SSM_WIDTH, 2 * SSM_NSTATE), BF16),
                   jax.ShapeDtypeStruct((64, SSM_NSTATE), F32)),
        compiler_params=pltpu.CompilerParams(vmem_limit_bytes=VMEM_LIMIT),
        name="s5_prep",
    )(lam_re.reshape(1, SSM_NSTATE), lam_im.reshape(1, SSM_NSTATE), ldt, bre_d, bim_d)


def _in_proj_kernel(x_ref, g_ref, wm_ref, wat_ref, proj_ref, att_ref, xn_ref):
    @pl.when(pl.program_id(1) == 0)
    def _():
        xn = _rms_rows(x_ref[...], g_ref[...]).astype(BF16)
        xn_ref[...] = xn
        att_ref[...] = lax.dot_general(wat_ref[...], xn, (((1,), (1,)), ((), ())),
                                       preferred_element_type=F32)

    proj_ref[...] = jnp.dot(xn_ref[...], wm_ref[...], preferred_element_type=F32)


def _in_proj(x2d, g, w_main, w_att_t):
    t = x2d.shape[0]
    return pl.pallas_call(
        _in_proj_kernel,
        grid=(t // TM_PROJ, N_MAIN // TN_PROJ),
        in_specs=[
            pl.BlockSpec((TM_PROJ, D_MODEL), lambda i, j: (i, 0)),
            pl.BlockSpec((1, D_MODEL), lambda i, j: (0, 0)),
            pl.BlockSpec((D_MODEL, TN_PROJ), lambda i, j: (0, j)),
            pl.BlockSpec((N_ATT, D_MODEL), lambda i, j: (0, 0)),
        ],
        out_specs=(
            pl.BlockSpec((TM_PROJ, TN_PROJ), lambda i, j: (i, j)),
            pl.BlockSpec((N_ATT, TM_PROJ), lambda i, j: (0, i)),
        ),
        out_shape=(jax.ShapeDtypeStruct((t, N_MAIN), F32),
                   jax.ShapeDtypeStruct((N_ATT, t), F32)),
        scratch_shapes=[pltpu.VMEM((TM_PROJ, D_MODEL), BF16)],
        compiler_params=_cparams(2),
        name="in_proj",
    )(x2d, g, w_main, w_att_t)


def _s5_kernel(u_ref, bmat_ref, coef_ref, cmat_ref, d_ref, wglu_ref, o_ref, st_ref):
    ls = u_ref.shape[0]

    @pl.when(pl.program_id(1) == 0)
    def _():
        st_ref[0:HALO, :] = jnp.zeros((HALO, 2 * SSM_NSTATE), F32)

    u = u_ref[...]
    st_ref[HALO:, :] = jnp.dot(u.astype(BF16), bmat_ref[...], preferred_element_type=F32)

    def row_group(r, carry):
        row0 = pl.multiple_of(HALO + r * SUBLANES, SUBLANES)
        for cb in range(SSM_NSTATE // LANES):
            cre = slice(cb * LANES, (cb + 1) * LANES)
            cim = slice(SSM_NSTATE + cb * LANES, SSM_NSTATE + (cb + 1) * LANES)
            xr = st_ref[pl.ds(row0, SUBLANES), cre]
            xi = st_ref[pl.ds(row0, SUBLANES), cim]
            for idx, k in enumerate((1, 2, 4)):
                ar = coef_ref[idx * 16:idx * 16 + 8, cre]
                ai = coef_ref[idx * 16 + 8:idx * 16 + 16, cre]
                sr = pltpu.roll(xr, k, 0)
                si = pltpu.roll(xi, k, 0)
                xr, xi = xr + (ar * sr - ai * si), xi + (ar * si + ai * sr)
            prev0 = pl.multiple_of(row0 - SUBLANES, SUBLANES)
            cr = jnp.broadcast_to(st_ref[pl.ds(prev0, SUBLANES), cre][SUBLANES - 1:], (SUBLANES, LANES))
            ci = jnp.broadcast_to(st_ref[pl.ds(prev0, SUBLANES), cim][SUBLANES - 1:], (SUBLANES, LANES))
            pr = coef_ref[48:56, cre]
            pi = coef_ref[56:64, cre]
            st_ref[pl.ds(row0, SUBLANES), cre] = xr + (pr * cr - pi * ci)
            st_ref[pl.ds(row0, SUBLANES), cim] = xi + (pr * ci + pi * cr)
        return carry

    lax.fori_loop(0, ls // SUBLANES, row_group, 0)
    st_ref[HALO - 1:HALO, :] = st_ref[HALO + ls - 1:HALO + ls, :]

    y = jnp.dot(st_ref[HALO:, :].astype(BF16), cmat_ref[...], preferred_element_type=F32)
    y = y + d_ref[...] * u
    y = 0.5 * y * (1.0 + jnp.tanh(0.7978845608028654 * (y + 0.044715 * (y * y * y))))
    glu = jnp.dot(y.astype(BF16), wglu_ref[...], preferred_element_type=F32)
    o_ref[...] = y * _sigmoid(glu)


def _s5(proj, bmat, coef, cmat, d_skip, w_glu, bsz, seq):
    nchunk = seq // LS_S5
    return pl.pallas_call(
        _s5_kernel,
        grid=(bsz, nchunk),
        in_specs=[
            pl.BlockSpec((LS_S5, SSM_WIDTH), lambda b, c: (b * nchunk + c, 0)),
            pl.BlockSpec((SSM_WIDTH, 2 * SSM_NSTATE), lambda b, c: (0, 0)),
            pl.BlockSpec((64, SSM_NSTATE), lambda b, c: (0, 0)),
            pl.BlockSpec((2 * SSM_NSTATE, SSM_WIDTH), lambda b, c: (0, 0)),
            pl.BlockSpec((1, SSM_WIDTH), lambda b, c: (0, 0)),
            pl.BlockSpec((SSM_WIDTH, SSM_WIDTH), lambda b, c: (0, 0)),
        ],
        out_specs=pl.BlockSpec((LS_S5, SSM_WIDTH), lambda b, c: (b * nchunk + c, 0)),
        out_shape=jax.ShapeDtypeStruct((bsz * seq, SSM_WIDTH), F32),
        scratch_shapes=[pltpu.VMEM((HALO + LS_S5, 2 * SSM_NSTATE), F32)],
        compiler_params=_cparams(2),
        name="s5",
    )(proj, bmat, coef, cmat, d_skip, w_glu)


def _ret_tables(seq):
    c = RET_CHUNK
    half = RET_QK_DIM // 2
    inv = 1.0 / (RET_THETA ** np.linspace(0.0, 1.0, half))
    ang = np.arange(seq)[:, None] * inv[None, :]
    cos, sin = np.cos(ang), np.sin(ang)
    cos_full = np.tile(np.concatenate([cos, cos], axis=1), (1, RET_HEADS))
    sin_sgn = np.tile(np.concatenate([-sin, sin], axis=1), (1, RET_HEADS))
    log_gamma = np.log1p(-np.exp2(-5.0 - np.arange(RET_HEADS)))
    pos = np.arange(c)
    rel = pos[:, None] - pos[None, :]
    decay = np.where(rel >= 0, np.exp(log_gamma[:, None, None] * np.maximum(rel, 0)[None]), 0.0)
    zeta = np.exp(log_gamma[None, :] * (c - 1.0 - pos)[:, None])
    xi = np.exp(log_gamma[None, :] * (pos + 1.0)[:, None])
    zeta = np.repeat(zeta, RET_QK_DIM, axis=1)
    xi = np.repeat(xi, RET_QK_DIM, axis=1)
    cdec = np.repeat(np.exp(log_gamma * c), RET_V_DIM)[None, :]
    as32 = lambda a: jnp.asarray(a, dtype=F32)
    return as32(cos_full), as32(sin_sgn), as32(decay), as32(zeta), as32(xi), as32(cdec)


def _ret_kernel(q_ref, k_ref, v_ref, g_ref, cos_ref, sin_ref, decay_ref, zeta_ref, xi_ref,
                cdec_ref, gn_ref, o_ref, state_ref):
    @pl.when(pl.program_id(1) == 0)
    def _():
        state_ref[...] = jnp.zeros(state_ref.shape, F32)

    cos = cos_ref[...]
    sin = sin_ref[...]
    lane = lax.broadcasted_iota(jnp.int32, cos.shape, 1)
    first_half = (lane % RET_QK_DIM) < (RET_QK_DIM // 2)

    def rot(x):
        swapped = jnp.where(first_half, pltpu.roll(x, RET_QK_WIDTH - RET_QK_DIM // 2, 1),
                            pltpu.roll(x, RET_QK_DIM // 2, 1))
        return x * cos + swapped * sin

    q = rot(q_ref[...])
    k = rot(k_ref[...]) * (RET_QK_DIM ** -0.5)
    qx = (q * xi_ref[...]).astype(BF16)
    kz = (k * zeta_ref[...]).astype(BF16)
    qb = q.astype(BF16)
    kb = k.astype(BF16)
    vb = v_ref[...].astype(BF16)
    g = g_ref[...]
    gn = gn_ref[...]
    cdec = cdec_ref[...]
    for h in range(RET_HEADS):
        qs = slice(h * RET_QK_DIM, (h + 1) * RET_QK_DIM)
        vs = slice(h * RET_V_DIM, (h + 1) * RET_V_DIM)
        s = lax.dot_general(qb[:, qs], kb[:, qs], (((1,), (1,)), ((), ())),
                            preferred_element_type=F32) * decay_ref[h]
        state = state_ref[h]
        o = (jnp.dot(s.astype(BF16), vb[:, vs], preferred_element_type=F32)
             + jnp.dot(qx[:, qs], state.astype(BF16), preferred_element_type=F32))
        kv = lax.dot_general(kz[:, qs], vb[:, vs], (((0,), (0,)), ((), ())),
                             preferred_element_type=F32)
        state_ref[h] = cdec[:, vs] * state + kv
        mu = jnp.mean(o, axis=-1, keepdims=True)
        oc = o - mu
        var = jnp.mean(oc * oc, axis=-1, keepdims=True)
        on = oc * lax.rsqrt(var + NORM_EPS) * gn[:, vs]
        gh = g[:, vs]
        o_ref[:, vs] = gh * _sigmoid(gh) * on


def _retention(proj, gn_g, tables, bsz, seq):
    c = RET_CHUNK
    nchunk = seq // c
    cos_full, sin_sgn, decay, zeta, xi, cdec = tables
    tok = lambda b, n: b * nchunk + n
    return pl.pallas_call(
        _ret_kernel,
        grid=(bsz, nchunk),
        in_specs=[
            pl.BlockSpec((c, RET_QK_WIDTH), lambda b, n: (tok(b, n), 2)),
            pl.BlockSpec((c, RET_QK_WIDTH), lambda b, n: (tok(b, n), 3)),
            pl.BlockSpec((c, RET_V_WIDTH), lambda b, n: (tok(b, n), 2)),
            pl.BlockSpec((c, RET_V_WIDTH), lambda b, n: (tok(b, n), 3)),
            pl.BlockSpec((c, RET_QK_WIDTH), lambda b, n: (n, 0)),
            pl.BlockSpec((c, RET_QK_WIDTH), lambda b, n: (n, 0)),
            pl.BlockSpec((RET_HEADS, c, c), lambda b, n: (0, 0, 0)),
            pl.BlockSpec((c, RET_QK_WIDTH), lambda b, n: (0, 0)),
            pl.BlockSpec((c, RET_QK_WIDTH), lambda b, n: (0, 0)),
            pl.BlockSpec((1, RET_V_WIDTH), lambda b, n: (0, 0)),
            pl.BlockSpec((1, RET_V_WIDTH), lambda b, n: (0, 0)),
        ],
        out_specs=pl.BlockSpec((c, RET_V_WIDTH), lambda b, n: (tok(b, n), 0)),
        out_shape=jax.ShapeDtypeStruct((bsz * seq, RET_V_WIDTH), F32),
        scratch_shapes=[pltpu.VMEM((RET_HEADS, RET_QK_DIM, RET_V_DIM), F32)],
        compiler_params=_cparams(2),
        name="retention",
    )(proj, proj, proj, proj, cos_full, sin_sgn, decay, zeta, xi, cdec, gn_g)


def _moba_tables(seq):
    inv = ROPE_THETA ** (-np.arange(ROPE_HALF) / ROPE_HALF)
    ang = inv[:, None] * np.arange(seq)[None, :]
    return jnp.asarray(np.cos(ang), dtype=F32), jnp.asarray(np.sin(ang), dtype=F32)


def _moba_prep_kernel(att_ref, qg_ref, kg_ref, cos_ref, sin_ref,
                      qt_ref, kn_ref, vt_ref, sel_ref, kmean_ref):
    i = pl.program_id(1)
    nb = kmean_ref.shape[1]

    @pl.when(i == 0)
    def _():
        kmean_ref[...] = jnp.zeros(kmean_ref.shape, F32)

    cos = cos_ref[...]
    sin = sin_ref[...]

    def norm_rot(x, g):
        ms = jnp.mean(x * x, axis=0, keepdims=True)
        xn = x * lax.rsqrt(ms + NORM_EPS) * g
        x1 = xn[0:ROPE_HALF]
        x2 = xn[ROPE_HALF:2 * ROPE_HALF]
        return jnp.concatenate([x1 * cos - x2 * sin, x1 * sin + x2 * cos, xn[2 * ROPE_HALF:]], axis=0)

    row = lax.broadcasted_iota(jnp.int32, (nb, MOBA_BLOCK), 0)
    past = row < i
    for h in range(ATT_HEADS):
        hs = slice(h * ATT_HEAD_DIM, (h + 1) * ATT_HEAD_DIM)
        q = norm_rot(att_ref[hs, :], qg_ref[...])
        k = norm_rot(att_ref[ATT_WIDTH + h * ATT_HEAD_DIM:ATT_WIDTH + (h + 1) * ATT_HEAD_DIM, :], kg_ref[...])
        kn = k.T
        kn_ref[0, h] = kn.astype(BF16)
        qt_ref[0, hs, :] = (q * ATT_Q_SCALE).astype(BF16)
        vt_ref[0, hs, :] = att_ref[2 * ATT_WIDTH + h * ATT_HEAD_DIM:2 * ATT_WIDTH + (h + 1) * ATT_HEAD_DIM, :].astype(BF16)
        kmean_ref[h, pl.ds(i, 1), :] = jnp.mean(kn, axis=0, keepdims=True)
        gate = jnp.dot(kmean_ref[h], q, preferred_element_type=F32, precision=lax.Precision.HIGHEST)
        gate = jnp.where(past, gate, NEG_INF)
        beaten = jnp.zeros(gate.shape, F32)
        for j in range(nb):
            gj = jnp.broadcast_to(gate[j:j + 1, :], gate.shape)
            ahead = jnp.where(gj > gate, 1.0, jnp.where(gj == gate, jnp.where(row > j, 1.0, 0.0), 0.0))
            beaten = beaten + ahead
        sel_ref[0, h] = jnp.where(past, jnp.where(beaten < MOBA_TOPK, 1.0, 0.0), 0.0)


def _moba_prep(att_t, qn_g, kn_g, tables, bsz, seq):
    nb = seq // MOBA_BLOCK
    cos_t, sin_t = tables
    return pl.pallas_call(
        _moba_prep_kernel,
        grid=(bsz, nb),
        in_specs=[
            pl.BlockSpec((N_ATT, MOBA_BLOCK), lambda b, i: (0, b * nb + i)),
            pl.BlockSpec((ATT_HEAD_DIM, 1), lambda b, i: (0, 0)),
            pl.BlockSpec((ATT_HEAD_DIM, 1), lambda b, i: (0, 0)),
            pl.BlockSpec((ROPE_HALF, MOBA_BLOCK), lambda b, i: (0, i)),
            pl.BlockSpec((ROPE_HALF, MOBA_BLOCK), lambda b, i: (0, i)),
        ],
        out_specs=(
            pl.BlockSpec((1, ATT_WIDTH, MOBA_BLOCK), lambda b, i: (b, 0, i)),
            pl.BlockSpec((1, ATT_HEADS, MOBA_BLOCK, ATT_HEAD_DIM), lambda b, i: (b, 0, i, 0)),
            pl.BlockSpec((1, ATT_WIDTH, MOBA_BLOCK), lambda b, i: (b, 0, i)),
            pl.BlockSpec((1, ATT_HEADS, nb, MOBA_BLOCK), lambda b, i: (b, 0, 0, i)),
        ),
        out_shape=(
            jax.ShapeDtypeStruct((bsz, ATT_WIDTH, seq), BF16),
            jax.ShapeDtypeStruct((bsz, ATT_HEADS, seq, ATT_HEAD_DIM), BF16),
            jax.ShapeDtypeStruct((bsz, ATT_WIDTH, seq), BF16),
            jax.ShapeDtypeStruct((bsz, ATT_HEADS, nb, seq), F32),
        ),
        scratch_shapes=[pltpu.VMEM((ATT_HEADS, nb, ATT_HEAD_DIM), F32)],
        compiler_params=_cparams(2),
        name="moba_prep",
    )(att_t, qn_g.reshape(ATT_HEAD_DIM, 1), kn_g.reshape(ATT_HEAD_DIM, 1), cos_t, sin_t)


def _moba_attn_kernel(qt_ref, k_ref, vt_ref, sel_ref, o_ref, m_ref, l_ref, acc_ref):
    blk = MOBA_BLOCK
    hd = ATT_HEAD_DIM
    nb = sel_ref.shape[2]
    heads = k_ref.shape[1]
    kpos = lax.broadcasted_iota(jnp.int32, (blk, blk), 0)
    qpos = lax.broadcasted_iota(jnp.int32, (blk, blk), 1)
    causal = kpos <= qpos

    def q_block(i, carry):
        q0 = pl.multiple_of(i * blk, blk)

        def scores(h, j0):
            return jnp.dot(k_ref[0, h, pl.ds(j0, blk), :], qt_ref[0, h * hd:(h + 1) * hd, pl.ds(q0, blk)],
                           preferred_element_type=F32)

        def values(h, j0):
            return vt_ref[0, h * hd:(h + 1) * hd, pl.ds(j0, blk)]

        ss = [scores(h, q0) for h in range(heads)]
        ps = []
        for h in range(heads):
            s = jnp.where(causal, ss[h], NEG_INF)
            m = jnp.max(s, axis=0, keepdims=True)
            p = jnp.exp2(s - m)
            m_ref[h] = m
            l_ref[h] = jnp.sum(p, axis=0, keepdims=True)
            ps.append(p.astype(BF16))
        for h in range(heads):
            acc_ref[h] = jnp.dot(values(h, q0), ps[h], preferred_element_type=F32)

        def past_block(j, carry):
            j0 = pl.multiple_of(j * blk, blk)
            ss = [scores(h, j0) for h in range(heads)]
            ps, ms, ls = [], [], []
            for h in range(heads):
                picked = sel_ref[0, h, pl.ds(j, 1), pl.ds(q0, blk)] > 0.5
                m_blk = jnp.max(ss[h], axis=0, keepdims=True)
                p = jnp.exp2(ss[h] - jnp.where(picked, m_blk, -NEG_INF))
                ps.append(p.astype(BF16))
                ms.append(jnp.where(picked, m_blk, NEG_INF))
                ls.append(jnp.sum(p, axis=0, keepdims=True))
            pvs = [jnp.dot(values(h, j0), ps[h], preferred_element_type=F32) for h in range(heads)]
            for h in range(heads):
                m_old = m_ref[h]
                m_new = jnp.maximum(m_old, ms[h])
                a_old = jnp.exp2(m_old - m_new)
                a_blk = jnp.exp2(ms[h] - m_new)
                m_ref[h] = m_new
                l_ref[h] = a_old * l_ref[h] + a_blk * ls[h]
                acc_ref[h] = a_old * acc_ref[h] + a_blk * pvs[h]
            return carry

        lax.fori_loop(0, i, past_block, 0)
        for h in range(heads):
            o_ref[0, h * hd:(h + 1) * hd, pl.ds(q0, blk)] = acc_ref[h] / l_ref[h]
        return carry

    lax.fori_loop(0, nb, q_block, 0)


def _moba_attn(qt, kn, vt, sel):
    bsz, _, seq = qt.shape
    nb = seq // MOBA_BLOCK
    hg = ATT_HEADS_PER_STEP
    return pl.pallas_call(
        _moba_attn_kernel,
        grid=(bsz, ATT_HEADS // hg),
        in_specs=[
            pl.BlockSpec((1, hg * ATT_HEAD_DIM, seq), lambda b, h: (b, h, 0)),
            pl.BlockSpec((1, hg, seq, ATT_HEAD_DIM), lambda b, h: (b, h, 0, 0)),
            pl.BlockSpec((1, hg * ATT_HEAD_DIM, seq), lambda b, h: (b, h, 0)),
            pl.BlockSpec((1, hg, nb, seq), lambda b, h: (b, h, 0, 0)),
        ],
        out_specs=pl.BlockSpec((1, hg * ATT_HEAD_DIM, seq), lambda b, h: (b, h, 0)),
        out_shape=jax.ShapeDtypeStruct((bsz, ATT_WIDTH, seq), F32),
        scratch_shapes=[pltpu.VMEM((hg, 1, MOBA_BLOCK), F32), pltpu.VMEM((hg, 1, MOBA_BLOCK), F32),
                        pltpu.VMEM((hg, ATT_HEAD_DIM, MOBA_BLOCK), F32)],
        compiler_params=_cparams(2),
        name="moba_attn",
    )(qt, kn, vt, sel)


def _merge_kernel(x_ref, ys_ref, yr_ref, at_ref, gs_ref, gr_ref, ga_ref,
                  ws_ref, wr_ref, wa_ref, wo_ref, o_ref):
    ps = jnp.dot(ys_ref[...].astype(BF16), ws_ref[...], preferred_element_type=F32)
    pr = jnp.dot(yr_ref[...].astype(BF16), wr_ref[...], preferred_element_type=F32)
    pa = lax.dot_general(at_ref[0].astype(BF16), wa_ref[...], (((0,), (0,)), ((), ())),
                         preferred_element_type=F32)
    merged = _sigmoid(gs_ref[...]) * ps + _sigmoid(gr_ref[...]) * pr + _sigmoid(ga_ref[...]) * pa
    o_ref[...] = x_ref[...] + jnp.dot(merged.astype(BF16), wo_ref[...], preferred_element_type=F32)


def _merge(x2d, y_s, y_r, o_t, proj, w_s, w_r, w_a, w_o, bsz, seq):
    tm = TM_MERGE
    per_seq = seq // tm
    row = lambda i: (i, 0)
    const = lambda i: (0, 0)
    return pl.pallas_call(
        _merge_kernel,
        grid=(bsz * per_seq,),
        in_specs=[
            pl.BlockSpec((tm, D_MODEL), row),
            pl.BlockSpec((tm, SSM_WIDTH), row),
            pl.BlockSpec((tm, RET_V_WIDTH), row),
            pl.BlockSpec((1, ATT_WIDTH, tm), lambda i: (i // per_seq, 0, i % per_seq)),
            pl.BlockSpec((tm, D_MODEL), lambda i: (i, 2)),
            pl.BlockSpec((tm, D_MODEL), lambda i: (i, 3)),
            pl.BlockSpec((tm, D_MODEL), lambda i: (i, 4)),
            pl.BlockSpec((SSM_WIDTH, D_MODEL), const),
            pl.BlockSpec((RET_V_WIDTH, D_MODEL), const),
            pl.BlockSpec((ATT_WIDTH, D_MODEL), const),
            pl.BlockSpec((D_MODEL, D_MODEL), const),
        ],
        out_specs=pl.BlockSpec((tm, D_MODEL), row),
        out_shape=jax.ShapeDtypeStruct(x2d.shape, F32),
        compiler_params=_cparams(1),
        name="merge",
    )(x2d, y_s, y_r, o_t, proj, proj, proj, w_s, w_r, w_a, w_o)


def _ffn_kernel(x_ref, halo_ref, g_ref, wg_ref, wu_ref, cwg_ref, cwu_ref, cbg_ref, cbu_ref,
                wd_ref, o_ref, xn_ref, acc_ref, *, tiles_per_seq):
    i = pl.program_id(0)
    c = pl.program_id(1)
    tm = x_ref.shape[0]

    @pl.when(c == 0)
    def _():
        g = g_ref[...]
        xn_ref[0:HALO, :] = _rms_rows(halo_ref[...], g).astype(BF16)
        xn_ref[HALO:, :] = _rms_rows(x_ref[...], g).astype(BF16)
        acc_ref[...] = jnp.zeros(acc_ref.shape, F32)

    rowi = lax.broadcasted_iota(jnp.int32, (tm + HALO, 1), 0)
    live = jnp.logical_or(rowi >= HALO, i % tiles_per_seq != 0)

    def conv(w_ref, cw_ref, cb_ref):
        h = jnp.dot(xn_ref[...], w_ref[...], preferred_element_type=F32)
        h = jnp.where(live, h, 0.0)
        cw = cw_ref[...]
        return (cb_ref[...] + h[HALO - 2:tm + HALO - 2] * cw[0:1] + h[HALO - 1:tm + HALO - 1] * cw[1:2]
                + h[HALO:] * cw[2:3])

    hg = conv(wg_ref, cwg_ref, cbg_ref)
    hu = conv(wu_ref, cwu_ref, cbu_ref)
    act = (hg * _sigmoid(hg) * hu).astype(BF16)
    acc_ref[...] += jnp.dot(act, wd_ref[...], preferred_element_type=F32)

    @pl.when(c == pl.num_programs(1) - 1)
    def _():
        o_ref[...] = x_ref[...] + acc_ref[...]


def _ffn(x2d, g, w_up, conv_w, conv_b, w_down, seq):
    t = x2d.shape[0]
    tm, fc = TM_FFN, FC_FFN
    nfc = FFN_HIDDEN // fc
    halo_blocks = tm // HALO
    return pl.pallas_call(
        functools.partial(_ffn_kernel, tiles_per_seq=seq // tm),
        grid=(t // tm, nfc),
        in_specs=[
            pl.BlockSpec((tm, D_MODEL), lambda i, c: (i, 0)),
            pl.BlockSpec((HALO, D_MODEL), lambda i, c: (jnp.maximum(i * halo_blocks - 1, 0), 0)),
            pl.BlockSpec((1, D_MODEL), lambda i, c: (0, 0)),
            pl.BlockSpec((D_MODEL, fc), lambda i, c: (0, c)),
            pl.BlockSpec((D_MODEL, fc), lambda i, c: (0, c + nfc)),
            pl.BlockSpec((CONV_WIDTH, fc), lambda i, c: (0, c)),
            pl.BlockSpec((CONV_WIDTH, fc), lambda i, c: (0, c + nfc)),
            pl.BlockSpec((1, fc), lambda i, c: (0, c)),
            pl.BlockSpec((1, fc), lambda i, c: (0, c + nfc)),
            pl.BlockSpec((fc, D_MODEL), lambda i, c: (c, 0)),
        ],
        out_specs=pl.BlockSpec((tm, D_MODEL), lambda i, c: (i, 0)),
        out_shape=jax.ShapeDtypeStruct(x2d.shape, F32),
        scratch_shapes=[pltpu.VMEM((tm + HALO, D_MODEL), BF16), pltpu.VMEM((tm, D_MODEL), F32)],
        compiler_params=_cparams(2),
        name="ffn",
    )(x2d, x2d, g, w_up, w_up, conv_w, conv_w, conv_b, conv_b, w_down)


def kernel(x, norm1_g, w_in, ssm_lambda_re, ssm_lambda_im, ssm_log_dt, ssm_b_re, ssm_b_im, ssm_c_re, ssm_c_im, ssm_d, ssm_w_glu, ret_gn_g, attn_qn_g, attn_kn_g, w_br_ssm, w_br_ret, w_br_att, w_o, norm2_g, ffn_w_up, ffn_conv_w, ffn_conv_b, ffn_w_down):
    bsz, seq, _ = x.shape
    depth = w_in.shape[0]
    assert seq % MOBA_BLOCK == 0 and seq % TM_PROJ == 0 and seq % TM_FFN == 0
    ret_tables = _ret_tables(seq)
    moba_tables = _moba_tables(seq)
    eye = jnp.eye(SSM_GROUPS, dtype=F32)
    att0 = 2048
    x2d = x.reshape(bsz * seq, D_MODEL)
    for l in range(depth):
        w = w_in[l]
        w_main = jnp.concatenate([w[:, :att0], w[:, att0 + N_ATT:]], axis=1).astype(BF16)
        w_att_t = w[:, att0:att0 + N_ATT].T.astype(BF16)
        cre_d = jnp.einsum('ghp,gk->gpkh', ssm_c_re[l], eye).reshape(SSM_NSTATE, SSM_WIDTH)
        cim_d = jnp.einsum('ghp,gk->gpkh', ssm_c_im[l], eye).reshape(SSM_NSTATE, SSM_WIDTH)
        cmat = jnp.concatenate([cre_d, -cim_d], axis=0).astype(BF16)

        bmat, coef = _s5_prep(ssm_lambda_re[l], ssm_lambda_im[l], ssm_log_dt[l], ssm_b_re[l], ssm_b_im[l])
        proj, att_t = _in_proj(x2d, norm1_g[l].reshape(1, D_MODEL), w_main, w_att_t)
        y_s = _s5(proj, bmat, coef, cmat, ssm_d[l].reshape(1, SSM_WIDTH), ssm_w_glu[l].astype(BF16), bsz, seq)
        y_r = _retention(proj, ret_gn_g[l].reshape(1, RET_V_WIDTH), ret_tables, bsz, seq)
        qt, kn, vt, sel = _moba_prep(att_t, attn_qn_g[l], attn_kn_g[l], moba_tables, bsz, seq)
        o_t = _moba_attn(qt, kn, vt, sel)
        x2d = _merge(x2d, y_s, y_r, o_t, proj, w_br_ssm[l].astype(BF16), w_br_ret[l].astype(BF16),
                     w_br_att[l].astype(BF16), w_o[l].astype(BF16), bsz, seq)
        x2d = _ffn(x2d, norm2_g[l].reshape(1, D_MODEL), ffn_w_up[l].astype(BF16), ffn_conv_w[l],
                   ffn_conv_b[l].reshape(1, 2 * FFN_HIDDEN), ffn_w_down[l].astype(BF16), seq)
    return x2d.reshape(bsz, seq, D_MODEL)
```

```python
import functools
import math

import numpy as np
import jax
import jax.numpy as jnp
from jax import lax
from jax.experimental import pallas as pl
from jax.experimental.pallas import tpu as pltpu

F32 = jnp.float32
BF16 = jnp.bfloat16

D_MODEL = 1024
SSM_WIDTH = 512
SSM_GROUP = 16
SSM_GROUPS = 32
SSM_STATE = 64
SSM_NSTATE = SSM_GROUPS * SSM_STATE
RET_HEADS = 4
RET_QK_DIM = 64
RET_V_DIM = 128
RET_QK_WIDTH = 256
RET_V_WIDTH = 512
RET_THETA = 10000.0
ATT_HEADS = 8
ATT_HEAD_DIM = 64
ATT_WIDTH = 512
MOBA_BLOCK = 256
MOBA_TOPK = 3
ROPE_THETA = 500000.0
ROPE_HALF = 8
FFN_HIDDEN = 2816
CONV_WIDTH = 3
NORM_EPS = 1e-6
NEG_INF = -1e30

N_MAIN = SSM_WIDTH + 2 * RET_QK_WIDTH + 2 * RET_V_WIDTH
N_ATT = 3 * ATT_WIDTH
N_GATES = 3 * D_MODEL

SUBLANES = 8
LANES = 128
VMEM_LIMIT = 52 * 1024 * 1024

TM_PROJ = 512
LS_S5 = 256
RET_CHUNK = 256
TM_MERGE = 512
TM_FFN = 512
FC_FFN = 1408
HALO = SUBLANES
ATT_HEADS_PER_STEP = 4
ATT_V_ROWS = ATT_HEAD_DIM + 16
ATT_Q_SCALE = math.log2(math.e) * ATT_HEAD_DIM ** -0.5


def _cparams(n_axes):
    return pltpu.CompilerParams(dimension_semantics=("arbitrary",) * n_axes,
                                vmem_limit_bytes=VMEM_LIMIT)


def _sigmoid(x):
    return 1.0 / (1.0 + jnp.exp(-x))


def _rms_rows(x, g):
    ms = jnp.mean(x * x, axis=-1, keepdims=True)
    return x * lax.rsqrt(ms + NORM_EPS) * g


def _s5_prep_kernel(lre_ref, lim_ref, ldt_ref, bre_ref, bim_ref, bmat_ref, coef_ref):
    lre = lre_ref[...]
    lim = lim_ref[...]
    dt = jnp.exp(ldt_ref[...])

    def lam_pow(k):
        mag = jnp.exp(k * lre * dt)
        ang = k * lim * dt
        return mag * jnp.cos(ang), mag * jnp.sin(ang)

    ar, ai = lam_pow(1.0)
    x = ar - 1.0
    den = lre * lre + lim * lim
    f_re = (x * lre + ai * lim) / den
    f_im = (ai * lre - x * lim) / den
    bre = bre_ref[...]
    bim = bim_ref[...]
    bmat_ref[:, :SSM_NSTATE] = (f_re * bre - f_im * bim).astype(BF16)
    bmat_ref[:, SSM_NSTATE:] = (f_re * bim + f_im * bre).astype(BF16)

    row = lax.broadcasted_iota(jnp.int32, (SUBLANES, SSM_NSTATE), 0)
    for idx, k in enumerate((1, 2, 4)):
        pr, pi = lam_pow(float(k))
        keep = row >= k
        coef_ref[idx * 16:idx * 16 + 8, :] = jnp.where(keep, jnp.broadcast_to(pr, keep.shape), 0.0)
        coef_ref[idx * 16 + 8:idx * 16 + 16, :] = jnp.where(keep, jnp.broadcast_to(pi, keep.shape), 0.0)
    pr, pi = lam_pow((row + 1).astype(F32))
    coef_ref[48:56, :] = pr
    coef_ref[56:64, :] = pi


def _s5_prep(lam_re, lam_im, log_dt, b_re, b_im):
    eye = jnp.eye(SSM_GROUPS, dtype=F32)
    bre_d = jnp.einsum('gph,gk->ghkp', b_re, eye).reshape(SSM_WIDTH, SSM_NSTATE)
    bim_d = jnp.einsum('gph,gk->ghkp', b_im, eye).reshape(SSM_WIDTH, SSM_NSTATE)
    ldt = jnp.repeat(log_dt, SSM_STATE).reshape(1, SSM_NSTATE)
    return pl.pallas_call(
        _s5_prep_kernel,
        out_shape=(jax.ShapeDtypeStruct((SSM_WIDTH, 2 * SSM_NSTATE), BF16),
                   jax.ShapeDtypeStruct((64, SSM_NSTATE), F32)),
        compiler_params=pltpu.CompilerParams(vmem_limit_bytes=VMEM_LIMIT),
        name="s5_prep",
    )(lam_re.reshape(1, SSM_NSTATE), lam_im.reshape(1, SSM_NSTATE), ldt, bre_d, bim_d)


def _in_proj_kernel(x_ref, g_ref, wm_ref, wat_ref, proj_ref, att_ref):
    xn = _rms_rows(x_ref[...], g_ref[...]).astype(BF16)
    att_ref[...] = lax.dot_general(wat_ref[...], xn, (((1,), (1,)), ((), ())),
                                   preferred_element_type=F32)
    proj_ref[...] = jnp.dot(xn, wm_ref[...], preferred_element_type=F32)


def _in_proj(x2d, g, w_main, w_att_t):
    t = x2d.shape[0]
    return pl.pallas_call(
        _in_proj_kernel,
        grid=(t // TM_PROJ,),
        in_specs=[
            pl.BlockSpec((TM_PROJ, D_MODEL), lambda i: (i, 0)),
            pl.BlockSpec((1, D_MODEL), lambda i: (0, 0)),
            pl.BlockSpec((D_MODEL, N_MAIN), lambda i: (0, 0)),
            pl.BlockSpec((N_ATT, D_MODEL), lambda i: (0, 0)),
        ],
        out_specs=(
            pl.BlockSpec((TM_PROJ, N_MAIN), lambda i: (i, 0)),
            pl.BlockSpec((N_ATT, TM_PROJ), lambda i: (0, i)),
        ),
        out_shape=(jax.ShapeDtypeStruct((t, N_MAIN), F32),
                   jax.ShapeDtypeStruct((N_ATT, t), F32)),
        compiler_params=_cparams(1),
        name="in_proj",
    )(x2d, g, w_main, w_att_t)


def _s5_kernel(u_ref, bmat_ref, coef_ref, cmat_ref, d_ref, wglu_ref, o_ref, st_ref):
    ls = u_ref.shape[0]

    @pl.when(pl.program_id(1) == 0)
    def _():
        st_ref[0:HALO, :] = jnp.zeros((HALO, 2 * SSM_NSTATE), F32)

    u = u_ref[...]
    st_ref[HALO:, :] = jnp.dot(u.astype(BF16), bmat_ref[...], preferred_element_type=F32)

    def row_group(r, carry):
        row0 = pl.multiple_of(HALO + r * SUBLANES, SUBLANES)
        for cb in range(SSM_NSTATE // LANES):
            cre = slice(cb * LANES, (cb + 1) * LANES)
            cim = slice(SSM_NSTATE + cb * LANES, SSM_NSTATE + (cb + 1) * LANES)
            xr = st_ref[pl.ds(row0, SUBLANES), cre]
            xi = st_ref[pl.ds(row0, SUBLANES), cim]
            for idx, k in enumerate((1, 2, 4)):
                ar = coef_ref[idx * 16:idx * 16 + 8, cre]
                ai = coef_ref[idx * 16 + 8:idx * 16 + 16, cre]
                sr = pltpu.roll(xr, k, 0)
                si = pltpu.roll(xi, k, 0)
                xr, xi = xr + (ar * sr - ai * si), xi + (ar * si + ai * sr)
            prev0 = pl.multiple_of(row0 - SUBLANES, SUBLANES)
            cr = jnp.broadcast_to(st_ref[pl.ds(prev0, SUBLANES), cre][SUBLANES - 1:], (SUBLANES, LANES))
            ci = jnp.broadcast_to(st_ref[pl.ds(prev0, SUBLANES), cim][SUBLANES - 1:], (SUBLANES, LANES))
            pr = coef_ref[48:56, cre]
            pi = coef_ref[56:64, cre]
            st_ref[pl.ds(row0, SUBLANES), cre] = xr + (pr * cr - pi * ci)
            st_ref[pl.ds(row0, SUBLANES), cim] = xi + (pr * ci + pi * cr)
        return carry

    lax.fori_loop(0, ls // SUBLANES, row_group, 0)
    st_ref[HALO - 1:HALO, :] = st_ref[HALO + ls - 1:HALO + ls, :]

    y = jnp.dot(st_ref[HALO:, :].astype(BF16), cmat_ref[...], preferred_element_type=F32)
    y = y + d_ref[...] * u
    y = 0.5 * y * (1.0 + jnp.tanh(0.7978845608028654 * (y + 0.044715 * (y * y * y))))
    glu = jnp.dot(y.astype(BF16), wglu_ref[...], preferred_element_type=F32)
    o_ref[...] = y * _sigmoid(glu)


def _s5(proj, bmat, coef, cmat, d_skip, w_glu, bsz, seq):
    nchunk = seq // LS_S5
    return pl.pallas_call(
        _s5_kernel,
        grid=(bsz, nchunk),
        in_specs=[
            pl.BlockSpec((LS_S5, SSM_WIDTH), lambda b, c: (b * nchunk + c, 0)),
            pl.BlockSpec((SSM_WIDTH, 2 * SSM_NSTATE), lambda b, c: (0, 0)),
            pl.BlockSpec((64, SSM_NSTATE), lambda b, c: (0, 0)),
            pl.BlockSpec((2 * SSM_NSTATE, SSM_WIDTH), lambda b, c: (0, 0)),
            pl.BlockSpec((1, SSM_WIDTH), lambda b, c: (0, 0)),
            pl.BlockSpec((SSM_WIDTH, SSM_WIDTH), lambda b, c: (0, 0)),
        ],
        out_specs=pl.BlockSpec((LS_S5, SSM_WIDTH), lambda b, c: (b * nchunk + c, 0)),
        out_shape=jax.ShapeDtypeStruct((bsz * seq, SSM_WIDTH), F32),
        scratch_shapes=[pltpu.VMEM((HALO + LS_S5, 2 * SSM_NSTATE), F32)],
        compiler_params=_cparams(2),
        name="s5",
    )(proj, bmat, coef, cmat, d_skip, w_glu)


def _ret_tables(seq):
    c = RET_CHUNK
    half = RET_QK_DIM // 2
    inv = 1.0 / (RET_THETA ** np.linspace(0.0, 1.0, half))
    ang = np.arange(seq)[:, None] * inv[None, :]
    cos, sin = np.cos(ang), np.sin(ang)
    cos_full = np.tile(np.concatenate([cos, cos], axis=1), (1, RET_HEADS))
    sin_sgn = np.tile(np.concatenate([-sin, sin], axis=1), (1, RET_HEADS))
    log_gamma = np.log1p(-np.exp2(-5.0 - np.arange(RET_HEADS)))
    pos = np.arange(c)
    rel = pos[:, None] - pos[None, :]
    decay = np.where(rel >= 0, np.exp(log_gamma[:, None, None] * np.maximum(rel, 0)[None]), 0.0)
    zeta = np.exp(log_gamma[None, :] * (c - 1.0 - pos)[:, None])
    xi = np.exp(log_gamma[None, :] * (pos + 1.0)[:, None])
    zeta = np.repeat(zeta, RET_QK_DIM, axis=1)
    xi = np.repeat(xi, RET_QK_DIM, axis=1)
    cdec = np.repeat(np.exp(log_gamma * c), RET_V_DIM)[None, :]
    as32 = lambda a: jnp.asarray(a, dtype=F32)
    return as32(cos_full), as32(sin_sgn), as32(decay), as32(zeta), as32(xi), as32(cdec)


def _ret_kernel(q_ref, k_ref, v_ref, g_ref, cos_ref, sin_ref, decay_ref, zeta_ref, xi_ref,
                cdec_ref, gn_ref, o_ref, state_ref):
    @pl.when(pl.program_id(1) == 0)
    def _():
        state_ref[...] = jnp.zeros(state_ref.shape, F32)

    cos = cos_ref[...]
    sin = sin_ref[...]
    lane = lax.broadcasted_iota(jnp.int32, cos.shape, 1)
    first_half = (lane % RET_QK_DIM) < (RET_QK_DIM // 2)

    def rot(x):
        swapped = jnp.where(first_half, pltpu.roll(x, RET_QK_WIDTH - RET_QK_DIM // 2, 1),
                            pltpu.roll(x, RET_QK_DIM // 2, 1))
        return x * cos + swapped * sin

    q = rot(q_ref[...])
    k = rot(k_ref[...]) * (RET_QK_DIM ** -0.5)
    qx = (q * xi_ref[...]).astype(BF16)
    kz = (k * zeta_ref[...]).astype(BF16)
    qb = q.astype(BF16)
    kb = k.astype(BF16)
    vb = v_ref[...].astype(BF16)
    g = g_ref[...]
    gn = gn_ref[...]
    cdec = cdec_ref[...]
    for h in range(RET_HEADS):
        qs = slice(h * RET_QK_DIM, (h + 1) * RET_QK_DIM)
        vs = slice(h * RET_V_DIM, (h + 1) * RET_V_DIM)
        s = lax.dot_general(qb[:, qs], kb[:, qs], (((1,), (1,)), ((), ())),
                            preferred_element_type=F32) * decay_ref[h]
        state = state_ref[h]
        o = (jnp.dot(s.astype(BF16), vb[:, vs], preferred_element_type=F32)
             + jnp.dot(qx[:, qs], state.astype(BF16), preferred_element_type=F32))
        kv = lax.dot_general(kz[:, qs], vb[:, vs], (((0,), (0,)), ((), ())),
                             preferred_element_type=F32)
        state_ref[h] = cdec[:, vs] * state + kv
        mu = jnp.mean(o, axis=-1, keepdims=True)
        oc = o - mu
        var = jnp.mean(oc * oc, axis=-1, keepdims=True)
        on = oc * lax.rsqrt(var + NORM_EPS) * gn[:, vs]
        gh = g[:, vs]
        o_ref[:, vs] = gh * _sigmoid(gh) * on


def _retention(proj, gn_g, tables, bsz, seq):
    c = RET_CHUNK
    nchunk = seq // c
    cos_full, sin_sgn, decay, zeta, xi, cdec = tables
    tok = lambda b, n: b * nchunk + n
    return pl.pallas_call(
        _ret_kernel,
        grid=(bsz, nchunk),
        in_specs=[
            pl.BlockSpec((c, RET_QK_WIDTH), lambda b, n: (tok(b, n), 2)),
            pl.BlockSpec((c, RET_QK_WIDTH), lambda b, n: (tok(b, n), 3)),
            pl.BlockSpec((c, RET_V_WIDTH), lambda b, n: (tok(b, n), 2)),
            pl.BlockSpec((c, RET_V_WIDTH), lambda b, n: (tok(b, n), 3)),
            pl.BlockSpec((c, RET_QK_WIDTH), lambda b, n: (n, 0)),
            pl.BlockSpec((c, RET_QK_WIDTH), lambda b, n: (n, 0)),
            pl.BlockSpec((RET_HEADS, c, c), lambda b, n: (0, 0, 0)),
            pl.BlockSpec((c, RET_QK_WIDTH), lambda b, n: (0, 0)),
            pl.BlockSpec((c, RET_QK_WIDTH), lambda b, n: (0, 0)),
            pl.BlockSpec((1, RET_V_WIDTH), lambda b, n: (0, 0)),
            pl.BlockSpec((1, RET_V_WIDTH), lambda b, n: (0, 0)),
        ],
        out_specs=pl.BlockSpec((c, RET_V_WIDTH), lambda b, n: (tok(b, n), 0)),
        out_shape=jax.ShapeDtypeStruct((bsz * seq, RET_V_WIDTH), F32),
        scratch_shapes=[pltpu.VMEM((RET_HEADS, RET_QK_DIM, RET_V_DIM), F32)],
        compiler_params=_cparams(2),
        name="retention",
    )(proj, proj, proj, proj, cos_full, sin_sgn, decay, zeta, xi, cdec, gn_g)


def _moba_tables(seq):
    inv = ROPE_THETA ** (-np.arange(ROPE_HALF) / ROPE_HALF)
    ang = inv[:, None] * np.arange(seq)[None, :]
    return jnp.asarray(np.cos(ang), dtype=F32), jnp.asarray(np.sin(ang), dtype=F32)


def _moba_prep_kernel(att_ref, qg_ref, kg_ref, cos_ref, sin_ref,
                      qt_ref, kn_ref, vt_ref, sel_ref, kmean_ref):
    i = pl.program_id(1)
    nb = kmean_ref.shape[1]

    @pl.when(i == 0)
    def _():
        kmean_ref[...] = jnp.zeros(kmean_ref.shape, F32)

    cos = cos_ref[...]
    sin = sin_ref[...]

    def norm_rot(x, g):
        ms = jnp.mean(x * x, axis=0, keepdims=True)
        xn = x * lax.rsqrt(ms + NORM_EPS) * g
        x1 = xn[0:ROPE_HALF]
        x2 = xn[ROPE_HALF:2 * ROPE_HALF]
        return jnp.concatenate([x1 * cos - x2 * sin, x1 * sin + x2 * cos, xn[2 * ROPE_HALF:]], axis=0)

    row = lax.broadcasted_iota(jnp.int32, (nb, MOBA_BLOCK), 0)
    past = row < i
    for h in range(ATT_HEADS):
        hs = slice(h * ATT_HEAD_DIM, (h + 1) * ATT_HEAD_DIM)
        q = norm_rot(att_ref[hs, :], qg_ref[...])
        k = norm_rot(att_ref[ATT_WIDTH + h * ATT_HEAD_DIM:ATT_WIDTH + (h + 1) * ATT_HEAD_DIM, :], kg_ref[...])
        kn = k.T
        kn_ref[0, h] = kn.astype(BF16)
        qt_ref[0, hs, :] = (q * ATT_Q_SCALE).astype(BF16)
        v0 = 2 * ATT_WIDTH + h * ATT_HEAD_DIM
        vt_ref[0, h * ATT_V_ROWS:h * ATT_V_ROWS + ATT_HEAD_DIM, :] = att_ref[v0:v0 + ATT_HEAD_DIM, :].astype(BF16)
        vt_ref[0, h * ATT_V_ROWS + ATT_HEAD_DIM:(h + 1) * ATT_V_ROWS, :] = jnp.ones(
            (ATT_V_ROWS - ATT_HEAD_DIM, MOBA_BLOCK), BF16)
        kmean_ref[h, pl.ds(i, 1), :] = jnp.mean(kn, axis=0, keepdims=True)
        gate = jnp.dot(kmean_ref[h], q, preferred_element_type=F32, precision=lax.Precision.HIGHEST)
        gate = jnp.where(past, gate, NEG_INF)
        beaten = jnp.zeros(gate.shape, F32)
        for j in range(nb):
            gj = jnp.broadcast_to(gate[j:j + 1, :], gate.shape)
            ahead = jnp.where(gj > gate, 1.0, jnp.where(gj == gate, jnp.where(row > j, 1.0, 0.0), 0.0))
            beaten = beaten + ahead
        sel_ref[0, h] = jnp.where(past, jnp.where(beaten < MOBA_TOPK, 1.0, 0.0), 0.0)


def _moba_prep(att_t, qn_g, kn_g, tables, bsz, seq):
    nb = seq // MOBA_BLOCK
    cos_t, sin_t = tables
    return pl.pallas_call(
        _moba_prep_kernel,
        grid=(bsz, nb),
        in_specs=[
            pl.BlockSpec((N_ATT, MOBA_BLOCK), lambda b, i: (0, b * nb + i)),
            pl.BlockSpec((ATT_HEAD_DIM, 1), lambda b, i: (0, 0)),
            pl.BlockSpec((ATT_HEAD_DIM, 1), lambda b, i: (0, 0)),
            pl.BlockSpec((ROPE_HALF, MOBA_BLOCK), lambda b, i: (0, i)),
            pl.BlockSpec((ROPE_HALF, MOBA_BLOCK), lambda b, i: (0, i)),
        ],
        out_specs=(
            pl.BlockSpec((1, ATT_WIDTH, MOBA_BLOCK), lambda b, i: (b, 0, i)),
            pl.BlockSpec((1, ATT_HEADS, MOBA_BLOCK, ATT_HEAD_DIM), lambda b, i: (b, 0, i, 0)),
            pl.BlockSpec((1, ATT_HEADS * ATT_V_ROWS, MOBA_BLOCK), lambda b, i: (b, 0, i)),
            pl.BlockSpec((1, ATT_HEADS, nb, MOBA_BLOCK), lambda b, i: (b, 0, 0, i)),
        ),
        out_shape=(
            jax.ShapeDtypeStruct((bsz, ATT_WIDTH, seq), BF16),
            jax.ShapeDtypeStruct((bsz, ATT_HEADS, seq, ATT_HEAD_DIM), BF16),
            jax.ShapeDtypeStruct((bsz, ATT_HEADS * ATT_V_ROWS, seq), BF16),
            jax.ShapeDtypeStruct((bsz, ATT_HEADS, nb, seq), F32),
        ),
        scratch_shapes=[pltpu.VMEM((ATT_HEADS, nb, ATT_HEAD_DIM), F32)],
        compiler_params=_cparams(2),
        name="moba_prep",
    )(att_t, qn_g.reshape(ATT_HEAD_DIM, 1), kn_g.reshape(ATT_HEAD_DIM, 1), cos_t, sin_t)


def _moba_attn_kernel(qt_ref, k_ref, vt_ref, sel_ref, o_ref, m_ref, acc_ref, s_ref, pv_ref, mb_ref):
    blk = MOBA_BLOCK
    hd = ATT_HEAD_DIM
    nb = sel_ref.shape[2]
    heads = k_ref.shape[1]
    vrows = ATT_V_ROWS
    kpos = lax.broadcasted_iota(jnp.int32, (blk, blk), 0)
    qpos = lax.broadcasted_iota(jnp.int32, (blk, blk), 1)
    causal = kpos <= qpos

    def q_block(i, carry):
        q0 = pl.multiple_of(i * blk, blk)

        def scores(h, j0):
            return jnp.dot(k_ref[0, h, pl.ds(j0, blk), :], qt_ref[0, h * hd:(h + 1) * hd, pl.ds(q0, blk)],
                           preferred_element_type=F32)

        def values(h, j0):
            return vt_ref[0, h * vrows:(h + 1) * vrows, pl.ds(j0, blk)]

        def local_softmax(s, m_blk, picked):
            shift = m_blk if picked is None else jnp.where(picked, m_blk, -NEG_INF)
            p = jnp.exp2((s - shift).astype(BF16))
            return p, (m_blk if picked is None else jnp.where(picked, m_blk, NEG_INF))

        own = [scores(h, q0) for h in range(heads)]
        for h in range(heads):
            s_ref[0, h] = scores(h, 0)
            pv_ref[1, h] = jnp.zeros((vrows, blk), F32)
            mb_ref[1, h] = jnp.full((1, blk), NEG_INF, F32)
        for h in range(heads):
            s = jnp.where(causal, own[h], NEG_INF)
            p, m = local_softmax(s, jnp.max(s, axis=0, keepdims=True), None)
            m_ref[h] = m
            acc_ref[h] = jnp.dot(values(h, q0), p, preferred_element_type=F32)

        def fold(slot):
            for h in range(heads):
                m_blk = mb_ref[slot, h]
                m_old = m_ref[h]
                m_new = jnp.maximum(m_old, m_blk)
                m_ref[h] = m_new
                acc_ref[h] = jnp.exp2(m_old - m_new) * acc_ref[h] + jnp.exp2(m_blk - m_new) * pv_ref[slot, h]

        def step(j, cur, nxt, live):
            jn0 = pl.multiple_of(jnp.minimum(j + 1, nb - 1) * blk, blk)
            for h in range(heads):
                s_ref[nxt, h] = scores(h, jn0)
            jc = jnp.minimum(j, nb - 1)
            jc0 = pl.multiple_of(jc * blk, blk)
            for h in range(heads):
                picked = jnp.logical_and(sel_ref[0, h, pl.ds(jc, 1), pl.ds(q0, blk)] > 0.5, live)
                s = s_ref[cur, h]
                p, m = local_softmax(s, jnp.max(s, axis=0, keepdims=True), picked)
                pv_ref[cur, h] = jnp.dot(values(h, jc0), p, preferred_element_type=F32)
                mb_ref[cur, h] = m
            fold(nxt)

        def block_pair(t, carry):
            step(2 * t, 0, 1, True)
            step(2 * t + 1, 1, 0, 2 * t + 1 < i)
            return carry

        lax.fori_loop(0, (i + 1) // 2, block_pair, 0)
        fold(1)
        for h in range(heads):
            acc = acc_ref[h]
            o_ref[0, h * hd:(h + 1) * hd, pl.ds(q0, blk)] = acc[:hd] / acc[hd:hd + 1]
        return carry

    lax.fori_loop(0, nb, q_block, 0)


def _moba_attn(qt, kn, vt, sel):
    bsz, _, seq = qt.shape
    nb = seq // MOBA_BLOCK
    hg = ATT_HEADS_PER_STEP
    return pl.pallas_call(
        _moba_attn_kernel,
        grid=(bsz, ATT_HEADS // hg),
        in_specs=[
            pl.BlockSpec((1, hg * ATT_HEAD_DIM, seq), lambda b, h: (b, h, 0)),
            pl.BlockSpec((1, hg, seq, ATT_HEAD_DIM), lambda b, h: (b, h, 0, 0)),
            pl.BlockSpec((1, hg * ATT_V_ROWS, seq), lambda b, h: (b, h, 0)),
            pl.BlockSpec((1, hg, nb, seq), lambda b, h: (b, h, 0, 0)),
        ],
        out_specs=pl.BlockSpec((1, hg * ATT_HEAD_DIM, seq), lambda b, h: (b, h, 0)),
        out_shape=jax.ShapeDtypeStruct((bsz, ATT_WIDTH, seq), F32),
        scratch_shapes=[pltpu.VMEM((hg, 1, MOBA_BLOCK), F32),
                        pltpu.VMEM((hg, ATT_V_ROWS, MOBA_BLOCK), F32),
                        pltpu.VMEM((2, hg, MOBA_BLOCK, MOBA_BLOCK), F32),
                        pltpu.VMEM((2, hg, ATT_V_ROWS, MOBA_BLOCK), F32),
                        pltpu.VMEM((2, hg, 1, MOBA_BLOCK), F32)],
        compiler_params=_cparams(2),
        name="moba_attn",
    )(qt, kn, vt, sel)


def _merge_kernel(x_ref, g_ref, ys_ref, yr_ref, at_ref, wg_ref, ws_ref, wr_ref, wa_ref, wo_ref, o_ref):
    x = x_ref[...]
    xn = _rms_rows(x, g_ref[...]).astype(BF16)

    def gate(k):
        return _sigmoid(jnp.dot(xn, wg_ref[:, k * D_MODEL:(k + 1) * D_MODEL], preferred_element_type=F32))

    merged = gate(0) * jnp.dot(ys_ref[...].astype(BF16), ws_ref[...], preferred_element_type=F32)
    merged += gate(1) * jnp.dot(yr_ref[...].astype(BF16), wr_ref[...], preferred_element_type=F32)
    merged += gate(2) * lax.dot_general(at_ref[0].astype(BF16), wa_ref[...], (((0,), (0,)), ((), ())),
                                        preferred_element_type=F32)
    o_ref[...] = x + jnp.dot(merged.astype(BF16), wo_ref[...], preferred_element_type=F32)


def _merge(x2d, g, y_s, y_r, o_t, w_gates, w_s, w_r, w_a, w_o, bsz, seq):
    tm = TM_MERGE
    per_seq = seq // tm
    row = lambda i: (i, 0)
    const = lambda i: (0, 0)
    return pl.pallas_call(
        _merge_kernel,
        grid=(bsz * per_seq,),
        in_specs=[
            pl.BlockSpec((tm, D_MODEL), row),
            pl.BlockSpec((1, D_MODEL), const),
            pl.BlockSpec((tm, SSM_WIDTH), row),
            pl.BlockSpec((tm, RET_V_WIDTH), row),
            pl.BlockSpec((1, ATT_WIDTH, tm), lambda i: (i // per_seq, 0, i % per_seq)),
            pl.BlockSpec((D_MODEL, N_GATES), const),
            pl.BlockSpec((SSM_WIDTH, D_MODEL), const),
            pl.BlockSpec((RET_V_WIDTH, D_MODEL), const),
            pl.BlockSpec((ATT_WIDTH, D_MODEL), const),
            pl.BlockSpec((D_MODEL, D_MODEL), const),
        ],
        out_specs=pl.BlockSpec((tm, D_MODEL), row),
        out_shape=jax.ShapeDtypeStruct(x2d.shape, F32),
        compiler_params=_cparams(1),
        name="merge",
    )(x2d, g, y_s, y_r, o_t, w_gates, w_s, w_r, w_a, w_o)


def _ffn_kernel(x_ref, halo_ref, g_ref, wg_ref, wu_ref, cwg_ref, cwu_ref, cbg_ref, cbu_ref,
                wd_ref, o_ref, xn_ref, acc_ref, *, tiles_per_seq):
    i = pl.program_id(0)
    c = pl.program_id(1)
    tm = x_ref.shape[0]

    @pl.when(c == 0)
    def _():
        g = g_ref[...]
        xn_ref[0:HALO, :] = _rms_rows(halo_ref[...], g).astype(BF16)
        xn_ref[HALO:, :] = _rms_rows(x_ref[...], g).astype(BF16)
        acc_ref[...] = jnp.zeros(acc_ref.shape, F32)

    rowi = lax.broadcasted_iota(jnp.int32, (tm + HALO, 1), 0)
    live = jnp.logical_or(rowi >= HALO, i % tiles_per_seq != 0)

    def conv(w_ref, cw_ref, cb_ref):
        h = jnp.dot(xn_ref[...], w_ref[...], preferred_element_type=F32)
        h = jnp.where(live, h, 0.0)
        cw = cw_ref[...]
        return (cb_ref[...] + h[HALO - 2:tm + HALO - 2] * cw[0:1] + h[HALO - 1:tm + HALO - 1] * cw[1:2]
                + h[HALO:] * cw[2:3])

    hg = conv(wg_ref, cwg_ref, cbg_ref)
    hu = conv(wu_ref, cwu_ref, cbu_ref)
    act = (hg * _sigmoid(hg) * hu).astype(BF16)
    acc_ref[...] += jnp.dot(act, wd_ref[...], preferred_element_type=F32)

    @pl.when(c == pl.num_programs(1) - 1)
    def _():
        o_ref[...] = x_ref[...] + acc_ref[...]


def _ffn(x2d, g, w_up, conv_w, conv_b, w_down, seq):
    t = x2d.shape[0]
    tm, fc = TM_FFN, FC_FFN
    nfc = FFN_HIDDEN // fc
    halo_blocks = tm // HALO
    return pl.pallas_call(
        functools.partial(_ffn_kernel, tiles_per_seq=seq // tm),
        grid=(t // tm, nfc),
        in_specs=[
            pl.BlockSpec((tm, D_MODEL), lambda i, c: (i, 0)),
            pl.BlockSpec((HALO, D_MODEL), lambda i, c: (jnp.maximum(i * halo_blocks - 1, 0), 0)),
            pl.BlockSpec((1, D_MODEL), lambda i, c: (0, 0)),
            pl.BlockSpec((D_MODEL, fc), lambda i, c: (0, c)),
            pl.BlockSpec((D_MODEL, fc), lambda i, c: (0, c + nfc)),
            pl.BlockSpec((CONV_WIDTH, fc), lambda i, c: (0, c)),
            pl.BlockSpec((CONV_WIDTH, fc), lambda i, c: (0, c + nfc)),
            pl.BlockSpec((1, fc), lambda i, c: (0, c)),
            pl.BlockSpec((1, fc), lambda i, c: (0, c + nfc)),
            pl.BlockSpec((fc, D_MODEL), lambda i, c: (c, 0)),
        ],
        out_specs=pl.BlockSpec((tm, D_MODEL), lambda i, c: (i, 0)),
        out_shape=jax.ShapeDtypeStruct(x2d.shape, F32),
        scratch_shapes=[pltpu.VMEM((tm + HALO, D_MODEL), BF16), pltpu.VMEM((tm, D_MODEL), F32)],
        compiler_params=_cparams(2),
        name="ffn",
    )(x2d, x2d, g, w_up, w_up, conv_w, conv_w, conv_b, conv_b, w_down)


def kernel(x, norm1_g, w_in, ssm_lambda_re, ssm_lambda_im, ssm_log_dt, ssm_b_re, ssm_b_im, ssm_c_re, ssm_c_im, ssm_d, ssm_w_glu, ret_gn_g, attn_qn_g, attn_kn_g, w_br_ssm, w_br_ret, w_br_att, w_o, norm2_g, ffn_w_up, ffn_conv_w, ffn_conv_b, ffn_w_down):
    bsz, seq, _ = x.shape
    depth = w_in.shape[0]
    assert seq % MOBA_BLOCK == 0 and seq % TM_PROJ == 0 and seq % TM_FFN == 0
    ret_tables = _ret_tables(seq)
    moba_tables = _moba_tables(seq)
    eye = jnp.eye(SSM_GROUPS, dtype=F32)
    x2d = x.reshape(bsz * seq, D_MODEL)
    for l in range(depth):
        w = w_in[l]
        w_main = w[:, :N_MAIN].astype(BF16)
        w_att_t = w[:, N_MAIN:N_MAIN + N_ATT].T.astype(BF16)
        w_gates = w[:, N_MAIN + N_ATT:].astype(BF16)
        norm1 = norm1_g[l].reshape(1, D_MODEL)
        cre_d = jnp.einsum('ghp,gk->gpkh', ssm_c_re[l], eye).reshape(SSM_NSTATE, SSM_WIDTH)
        cim_d = jnp.einsum('ghp,gk->gpkh', ssm_c_im[l], eye).reshape(SSM_NSTATE, SSM_WIDTH)
        cmat = jnp.concatenate([cre_d, -cim_d], axis=0).astype(BF16)

        bmat, coef = _s5_prep(ssm_lambda_re[l], ssm_lambda_im[l], ssm_log_dt[l], ssm_b_re[l], ssm_b_im[l])
        proj, att_t = _in_proj(x2d, norm1, w_main, w_att_t)
        y_s = _s5(proj, bmat, coef, cmat, ssm_d[l].reshape(1, SSM_WIDTH), ssm_w_glu[l].astype(BF16), bsz, seq)
        y_r = _retention(proj, ret_gn_g[l].reshape(1, RET_V_WIDTH), ret_tables, bsz, seq)
        qt, kn, vt, sel = _moba_prep(att_t, attn_qn_g[l], attn_kn_g[l], moba_tables, bsz, seq)
        o_t = _moba_attn(qt, kn, vt, sel)
        x2d = _merge(x2d, norm1, y_s, y_r, o_t, w_gates, w_br_ssm[l].astype(BF16), w_br_ret[l].astype(BF16),
                     w_br_att[l].astype(BF16), w_o[l].astype(BF16), bsz, seq)
        x2d = _ffn(x2d, norm2_g[l].reshape(1, D_MODEL), ffn_w_up[l].astype(BF16), ffn_conv_w[l],
                   ffn_conv_b[l].reshape(1, 2 * FFN_HIDDEN), ffn_w_down[l].astype(BF16), seq)
    return x2d.reshape(bsz, seq, D_MODEL)
```

```python
import functools
import math

import numpy as np
import jax
import jax.numpy as jnp
from jax import lax
from jax.experimental import pallas as pl
from jax.experimental.pallas import tpu as pltpu

F32 = jnp.float32
BF16 = jnp.bfloat16

D_MODEL = 1024
SSM_WIDTH = 512
SSM_GROUP = 16
SSM_GROUPS = 32
SSM_STATE = 64
SSM_NSTATE = SSM_GROUPS * SSM_STATE
RET_HEADS = 4
RET_QK_DIM = 64
RET_V_DIM = 128
RET_QK_WIDTH = 256
RET_V_WIDTH = 512
RET_THETA = 10000.0
ATT_HEADS = 8
ATT_HEAD_DIM = 64
ATT_WIDTH = 512
MOBA_BLOCK = 256
MOBA_TOPK = 3
ROPE_THETA = 500000.0
ROPE_HALF = 8
FFN_HIDDEN = 2816
CONV_WIDTH = 3
NORM_EPS = 1e-6
NEG_INF = -1e30

N_MAIN = SSM_WIDTH + 2 * RET_QK_WIDTH + 2 * RET_V_WIDTH
N_ATT = 3 * ATT_WIDTH
N_GATES = 3 * D_MODEL

SUBLANES = 8
LANES = 128
VMEM_LIMIT = 52 * 1024 * 1024

TM_PROJ = 512
LS_S5 = 256
S5_TSUB = LS_S5 // SUBLANES
S5_BLOCKS = 2
S5_BLK_CH = SSM_WIDTH // S5_BLOCKS
S5_BLK_STATES = SSM_NSTATE // S5_BLOCKS
S5_SCAN_COLS = 4
RET_CHUNK = 256
TM_MERGE = 512
TM_FFN = 512
FC_FFN = 1408
HALO = SUBLANES
ATT_HEADS_PER_STEP = 4
ATT_V_ROWS = ATT_HEAD_DIM + 16
ATT_Q_SCALE = math.log2(math.e) * ATT_HEAD_DIM ** -0.5


def _cparams(n_axes):
    return pltpu.CompilerParams(dimension_semantics=("arbitrary",) * n_axes,
                                vmem_limit_bytes=VMEM_LIMIT)


def _sigmoid(x):
    return 1.0 / (1.0 + jnp.exp(-x))


def _rms_rows(x, g):
    ms = jnp.mean(x * x, axis=-1, keepdims=True)
    return x * lax.rsqrt(ms + NORM_EPS) * g


def _s5_prep_kernel(lre_ref, lim_ref, ldt_ref, bre_ref, bim_ref, bmat_ref, ca_ref, cs_ref, pt_ref):
    lre = lre_ref[...]
    lim = lim_ref[...]
    dt = jnp.exp(ldt_ref[...])

    def lam_pow(k):
        mag = jnp.exp(k * lre * dt)
        ang = k * lim * dt
        return mag * jnp.cos(ang), mag * jnp.sin(ang)

    ar, ai = lam_pow(1.0)
    x = ar - 1.0
    den = lre * lre + lim * lim
    f_re = (x * lre + ai * lim) / den
    f_im = (ai * lre - x * lim) / den
    for bk in range(S5_BLOCKS):
        sl = slice(bk * S5_BLK_STATES, (bk + 1) * S5_BLK_STATES)
        bre = bre_ref[bk]
        bim = bim_ref[bk]
        bmat_ref[bk, :, :S5_BLK_STATES] = (f_re[:, sl] * bre - f_im[:, sl] * bim).astype(BF16)
        bmat_ref[bk, :, S5_BLK_STATES:] = (f_re[:, sl] * bim + f_im[:, sl] * bre).astype(BF16)

    tile = (SUBLANES, SSM_NSTATE)
    ca_ref[0:8, :] = jnp.broadcast_to(ar, tile)
    ca_ref[8:16, :] = jnp.broadcast_to(ai, tile)
    row = lax.broadcasted_iota(jnp.int32, tile, 0)
    for idx, k in enumerate((1, 2, 4)):
        pr, pi = lam_pow(float(k * S5_TSUB))
        keep = row >= k
        cs_ref[idx * 16:idx * 16 + 8, :] = jnp.where(keep, jnp.broadcast_to(pr, tile), 0.0)
        cs_ref[idx * 16 + 8:idx * 16 + 16, :] = jnp.where(keep, jnp.broadcast_to(pi, tile), 0.0)
    pr, pi = lam_pow(float(S5_TSUB))
    cs_ref[48:56, :] = jnp.broadcast_to(pr, tile)
    cs_ref[56:64, :] = jnp.broadcast_to(pi, tile)
    steps = (lax.broadcasted_iota(jnp.int32, (S5_TSUB, SSM_NSTATE), 0) + 1).astype(F32)
    pr, pi = lam_pow(steps)
    for t in range(S5_TSUB):
        pt_ref[0, t * SUBLANES:(t + 1) * SUBLANES, :] = jnp.broadcast_to(pr[t:t + 1], tile)
        pt_ref[1, t * SUBLANES:(t + 1) * SUBLANES, :] = jnp.broadcast_to(pi[t:t + 1], tile)


def _s5_prep(lam_re, lam_im, log_dt, b_re, b_im):
    gpb = SSM_GROUPS // S5_BLOCKS
    eye = jnp.eye(gpb, dtype=F32)

    def embed(b):
        b4 = b.reshape(S5_BLOCKS, gpb, SSM_STATE, SSM_GROUP)
        return jnp.einsum('bgph,gk->bghkp', b4, eye).reshape(S5_BLOCKS, S5_BLK_CH, S5_BLK_STATES)

    ldt = jnp.repeat(log_dt, SSM_STATE).reshape(1, SSM_NSTATE)
    return pl.pallas_call(
        _s5_prep_kernel,
        out_shape=(jax.ShapeDtypeStruct((S5_BLOCKS, S5_BLK_CH, 2 * S5_BLK_STATES), BF16),
                   jax.ShapeDtypeStruct((16, SSM_NSTATE), F32),
                   jax.ShapeDtypeStruct((64, SSM_NSTATE), F32),
                   jax.ShapeDtypeStruct((2, LS_S5, SSM_NSTATE), F32)),
        compiler_params=pltpu.CompilerParams(vmem_limit_bytes=VMEM_LIMIT),
        name="s5_prep",
    )(lam_re.reshape(1, SSM_NSTATE), lam_im.reshape(1, SSM_NSTATE), ldt, embed(b_re), embed(b_im))


def _in_proj_kernel(x_ref, g_ref, wm_ref, wat_ref, u_ref, proj_ref, att_ref):
    xn = _rms_rows(x_ref[...], g_ref[...]).astype(BF16)
    att_ref[...] = lax.dot_general(wat_ref[...], xn, (((1,), (1,)), ((), ())),
                                   preferred_element_type=F32)
    proj = jnp.dot(xn, wm_ref[...], preferred_element_type=F32)
    for c in range(SSM_WIDTH // LANES):
        u_ref[c] = proj[:, c * LANES:(c + 1) * LANES]
    proj_ref[...] = proj[:, SSM_WIDTH:]


def _in_proj(x2d, g, w_main, w_att_t):
    t = x2d.shape[0]
    return pl.pallas_call(
        _in_proj_kernel,
        grid=(t // TM_PROJ,),
        in_specs=[
            pl.BlockSpec((TM_PROJ, D_MODEL), lambda i: (i, 0)),
            pl.BlockSpec((1, D_MODEL), lambda i: (0, 0)),
            pl.BlockSpec((D_MODEL, N_MAIN), lambda i: (0, 0)),
            pl.BlockSpec((N_ATT, D_MODEL), lambda i: (0, 0)),
        ],
        out_specs=(
            pl.BlockSpec((SSM_WIDTH // LANES, TM_PROJ, LANES), lambda i: (0, i, 0)),
            pl.BlockSpec((TM_PROJ, N_MAIN - SSM_WIDTH), lambda i: (i, 0)),
            pl.BlockSpec((N_ATT, TM_PROJ), lambda i: (0, i)),
        ),
        out_shape=(jax.ShapeDtypeStruct((SSM_WIDTH // LANES, t, LANES), F32),
                   jax.ShapeDtypeStruct((t, N_MAIN - SSM_WIDTH), F32),
                   jax.ShapeDtypeStruct((N_ATT, t), F32)),
        compiler_params=_cparams(1),
        name="in_proj",
    )(x2d, g, w_main, w_att_t)


def _s5_kernel(u_ref, bmat_ref, ca_ref, cs_ref, pt_ref, cmat_ref, d_ref, wglu_ref, o_ref,
               st_ref, xb_ref, up_ref, cin_ref):
    ls = st_ref.shape[0]
    tsub = ls // SUBLANES
    nslab = SSM_WIDTH // LANES

    @pl.when(pl.program_id(1) == 0)
    def _():
        cin_ref[...] = jnp.zeros(cin_ref.shape, F32)

    for t in range(tsub):
        for c in range(nslab):
            up_ref[t * SUBLANES:(t + 1) * SUBLANES, c * LANES:(c + 1) * LANES] = (
                u_ref[c, pl.ds(t, SUBLANES, stride=tsub), :])
    u = up_ref[...]
    ub = u.astype(BF16)
    for bk in range(S5_BLOCKS):
        st_ref[:, bk * 2 * S5_BLK_STATES:(bk + 1) * 2 * S5_BLK_STATES] = jnp.dot(
            ub[:, bk * S5_BLK_CH:(bk + 1) * S5_BLK_CH], bmat_ref[bk], preferred_element_type=F32)

    row = lax.broadcasted_iota(jnp.int32, (SUBLANES, LANES), 0)
    tile = (SUBLANES, LANES)

    def columns(cb):
        per_blk = S5_BLK_STATES // LANES
        base = (cb // per_blk) * 2 * S5_BLK_STATES + (cb % per_blk) * LANES
        return (slice(base, base + LANES), slice(base + S5_BLK_STATES, base + S5_BLK_STATES + LANES),
                slice(cb * LANES, (cb + 1) * LANES))

    def cmul(ar, ai, xr, xi):
        return ar * xr - ai * xi, ar * xi + ai * xr

    def scan_columns(cbs):
        sl = [columns(cb) for cb in cbs]
        a = [(ca_ref[0:8, s], ca_ref[8:16, s]) for (_, _, s) in sl]
        first = []
        for re, im, _ in sl:
            first += [st_ref[0:SUBLANES, re], st_ref[0:SUBLANES, im]]

        def local_step(t, carry):
            r0 = pl.multiple_of(t * SUBLANES, SUBLANES)
            out = []
            for k, (re, im, _) in enumerate(sl):
                pr, pi = cmul(a[k][0], a[k][1], carry[2 * k], carry[2 * k + 1])
                nr = st_ref[pl.ds(r0, SUBLANES), re] + pr
                ni = st_ref[pl.ds(r0, SUBLANES), im] + pi
                st_ref[pl.ds(r0, SUBLANES), re] = nr
                st_ref[pl.ds(r0, SUBLANES), im] = ni
                out += [nr, ni]
            return tuple(out)

        ends = lax.fori_loop(1, tsub, local_step, tuple(first))

        entering = []
        for k, (re, im, s) in enumerate(sl):
            fr, fi = ends[2 * k], ends[2 * k + 1]
            gr = jnp.where(row == 0, cin_ref[:, re], pltpu.roll(fr, 1, 0))
            gi = jnp.where(row == 0, cin_ref[:, im], pltpu.roll(fi, 1, 0))
            for idx, kk in enumerate((1, 2, 4)):
                pr, pi = cmul(cs_ref[idx * 16:idx * 16 + 8, s], cs_ref[idx * 16 + 8:idx * 16 + 16, s],
                              pltpu.roll(gr, kk, 0), pltpu.roll(gi, kk, 0))
                gr, gi = gr + pr, gi + pi
            pr, pi = cmul(cs_ref[48:56, s], cs_ref[56:64, s], gr, gi)
            tr, ti = fr + pr, fi + pi
            cin_ref[:, re] = jnp.broadcast_to(tr[SUBLANES - 1:], tile)
            cin_ref[:, im] = jnp.broadcast_to(ti[SUBLANES - 1:], tile)
            entering += [gr, gi]

        def fix_step(tp, carry):
            r16 = pl.multiple_of(tp * 2 * SUBLANES, 2 * SUBLANES)
            for k, (re, im, s) in enumerate(sl):
                halves_r, halves_i = [], []
                for half in range(2):
                    r0 = pl.multiple_of(r16 + half * SUBLANES, SUBLANES)
                    pr, pi = cmul(pt_ref[0, pl.ds(r0, SUBLANES), s], pt_ref[1, pl.ds(r0, SUBLANES), s],
                                  entering[2 * k], entering[2 * k + 1])
                    halves_r.append(st_ref[pl.ds(r0, SUBLANES), re] + pr)
                    halves_i.append(st_ref[pl.ds(r0, SUBLANES), im] + pi)
                xb_ref[pl.ds(r16, 2 * SUBLANES), re] = jnp.concatenate(halves_r, axis=0).astype(BF16)
                xb_ref[pl.ds(r16, 2 * SUBLANES), im] = jnp.concatenate(halves_i, axis=0).astype(BF16)
            return carry

        lax.fori_loop(0, tsub // 2, fix_step, 0)

    for grp in range(SSM_NSTATE // LANES // S5_SCAN_COLS):
        scan_columns(range(grp * S5_SCAN_COLS, (grp + 1) * S5_SCAN_COLS))

    y = jnp.concatenate(
        [jnp.dot(xb_ref[:, bk * 2 * S5_BLK_STATES:(bk + 1) * 2 * S5_BLK_STATES], cmat_ref[bk],
                 preferred_element_type=F32) for bk in range(S5_BLOCKS)], axis=1)
    y = y + d_ref[...] * u
    y = 0.5 * y * (1.0 + jnp.tanh(0.7978845608028654 * (y + 0.044715 * (y * y * y))))
    glu = jnp.dot(y.astype(BF16), wglu_ref[...], preferred_element_type=F32)
    out = y * _sigmoid(glu)
    for t in range(tsub):
        for c in range(nslab):
            o_ref[c, pl.ds(t, SUBLANES, stride=tsub), :] = out[t * SUBLANES:(t + 1) * SUBLANES,
                                                               c * LANES:(c + 1) * LANES]


def _s5(u3, bmat, coef_a, coef_s, ptab, cmat, d_skip, w_glu, bsz, seq):
    nchunk = seq // LS_S5
    nslab = SSM_WIDTH // LANES
    const2 = lambda b, c: (0, 0)
    const3 = lambda b, c: (0, 0, 0)
    return pl.pallas_call(
        _s5_kernel,
        grid=(bsz, nchunk),
        in_specs=[
            pl.BlockSpec((nslab, LS_S5, LANES), lambda b, c: (0, b * nchunk + c, 0)),
            pl.BlockSpec((S5_BLOCKS, S5_BLK_CH, 2 * S5_BLK_STATES), const3),
            pl.BlockSpec((16, SSM_NSTATE), const2),
            pl.BlockSpec((64, SSM_NSTATE), const2),
            pl.BlockSpec((2, LS_S5, SSM_NSTATE), const3),
            pl.BlockSpec((S5_BLOCKS, 2 * S5_BLK_STATES, S5_BLK_CH), const3),
            pl.BlockSpec((1, SSM_WIDTH), const2),
            pl.BlockSpec((SSM_WIDTH, SSM_WIDTH), const2),
        ],
        out_specs=pl.BlockSpec((nslab, LS_S5, LANES), lambda b, c: (0, b * nchunk + c, 0)),
        out_shape=jax.ShapeDtypeStruct((nslab, bsz * seq, LANES), F32),
        scratch_shapes=[pltpu.VMEM((LS_S5, 2 * SSM_NSTATE), F32),
                        pltpu.VMEM((LS_S5, 2 * SSM_NSTATE), BF16),
                        pltpu.VMEM((LS_S5, SSM_WIDTH), F32),
                        pltpu.VMEM((SUBLANES, 2 * SSM_NSTATE), F32)],
        compiler_params=_cparams(2),
        name="s5",
    )(u3, bmat, coef_a, coef_s, ptab, cmat, d_skip, w_glu)


def _ret_tables(seq):
    c = RET_CHUNK
    half = RET_QK_DIM // 2
    inv = 1.0 / (RET_THETA ** np.linspace(0.0, 1.0, half))
    ang = np.arange(seq)[:, None] * inv[None, :]
    cos, sin = np.cos(ang), np.sin(ang)
    cos_full = np.tile(np.concatenate([cos, cos], axis=1), (1, RET_HEADS))
    sin_sgn = np.tile(np.concatenate([-sin, sin], axis=1), (1, RET_HEADS))
    log_gamma = np.log1p(-np.exp2(-5.0 - np.arange(RET_HEADS)))
    pos = np.arange(c)
    rel = pos[:, None] - pos[None, :]
    decay = np.where(rel >= 0, np.exp(log_gamma[:, None, None] * np.maximum(rel, 0)[None]), 0.0)
    zeta = np.exp(log_gamma[None, :] * (c - 1.0 - pos)[:, None])
    xi = np.exp(log_gamma[None, :] * (pos + 1.0)[:, None])
    zeta = np.repeat(zeta, RET_QK_DIM, axis=1)
    xi = np.repeat(xi, RET_QK_DIM, axis=1)
    cdec = np.repeat(np.exp(log_gamma * c), RET_V_DIM)[None, :]
    as32 = lambda a: jnp.asarray(a, dtype=F32)
    return as32(cos_full), as32(sin_sgn), as32(decay), as32(zeta), as32(xi), as32(cdec)


def _ret_kernel(q_ref, k_ref, v_ref, g_ref, cos_ref, sin_ref, decay_ref, zeta_ref, xi_ref,
                cdec_ref, gn_ref, o_ref, state_ref):
    @pl.when(pl.program_id(1) == 0)
    def _():
        state_ref[...] = jnp.zeros(state_ref.shape, F32)

    cos = cos_ref[...]
    sin = sin_ref[...]
    lane = lax.broadcasted_iota(jnp.int32, cos.shape, 1)
    first_half = (lane % RET_QK_DIM) < (RET_QK_DIM // 2)

    def rot(x):
        swapped = jnp.where(first_half, pltpu.roll(x, RET_QK_WIDTH - RET_QK_DIM // 2, 1),
                            pltpu.roll(x, RET_QK_DIM // 2, 1))
        return x * cos + swapped * sin

    q = rot(q_ref[...])
    k = rot(k_ref[...]) * (RET_QK_DIM ** -0.5)
    qx = (q * xi_ref[...]).astype(BF16)
    kz = (k * zeta_ref[...]).astype(BF16)
    qb = q.astype(BF16)
    kb = k.astype(BF16)
    vb = v_ref[...].astype(BF16)
    g = g_ref[...]
    gn = gn_ref[...]
    cdec = cdec_ref[...]
    for h in range(RET_HEADS):
        qs = slice(h * RET_QK_DIM, (h + 1) * RET_QK_DIM)
        vs = slice(h * RET_V_DIM, (h + 1) * RET_V_DIM)
        s = lax.dot_general(qb[:, qs], kb[:, qs], (((1,), (1,)), ((), ())),
                            preferred_element_type=F32) * decay_ref[h]
        state = state_ref[h]
        o = (jnp.dot(s.astype(BF16), vb[:, vs], preferred_element_type=F32)
             + jnp.dot(qx[:, qs], state.astype(BF16), preferred_element_type=F32))
        kv = lax.dot_general(kz[:, qs], vb[:, vs], (((0,), (0,)), ((), ())),
                             preferred_element_type=F32)
        state_ref[h] = cdec[:, vs] * state + kv
        mu = jnp.mean(o, axis=-1, keepdims=True)
        oc = o - mu
        var = jnp.mean(oc * oc, axis=-1, keepdims=True)
        on = oc * lax.rsqrt(var + NORM_EPS) * gn[:, vs]
        gh = g[:, vs]
        o_ref[:, vs] = gh * _sigmoid(gh) * on


def _retention(proj, gn_g, tables, bsz, seq):
    c = RET_CHUNK
    nchunk = seq // c
    cos_full, sin_sgn, decay, zeta, xi, cdec = tables
    tok = lambda b, n: b * nchunk + n
    return pl.pallas_call(
        _ret_kernel,
        grid=(bsz, nchunk),
        in_specs=[
            pl.BlockSpec((c, RET_QK_WIDTH), lambda b, n: (tok(b, n), 0)),
            pl.BlockSpec((c, RET_QK_WIDTH), lambda b, n: (tok(b, n), 1)),
            pl.BlockSpec((c, RET_V_WIDTH), lambda b, n: (tok(b, n), 1)),
            pl.BlockSpec((c, RET_V_WIDTH), lambda b, n: (tok(b, n), 2)),
            pl.BlockSpec((c, RET_QK_WIDTH), lambda b, n: (n, 0)),
            pl.BlockSpec((c, RET_QK_WIDTH), lambda b, n: (n, 0)),
            pl.BlockSpec((RET_HEADS, c, c), lambda b, n: (0, 0, 0)),
            pl.BlockSpec((c, RET_QK_WIDTH), lambda b, n: (0, 0)),
            pl.BlockSpec((c, RET_QK_WIDTH), lambda b, n: (0, 0)),
            pl.BlockSpec((1, RET_V_WIDTH), lambda b, n: (0, 0)),
            pl.BlockSpec((1, RET_V_WIDTH), lambda b, n: (0, 0)),
        ],
        out_specs=pl.BlockSpec((c, RET_V_WIDTH), lambda b, n: (tok(b, n), 0)),
        out_shape=jax.ShapeDtypeStruct((bsz * seq, RET_V_WIDTH), F32),
        scratch_shapes=[pltpu.VMEM((RET_HEADS, RET_QK_DIM, RET_V_DIM), F32)],
        compiler_params=_cparams(2),
        name="retention",
    )(proj, proj, proj, proj, cos_full, sin_sgn, decay, zeta, xi, cdec, gn_g)


def _moba_tables(seq):
    inv = ROPE_THETA ** (-np.arange(ROPE_HALF) / ROPE_HALF)
    ang = inv[:, None] * np.arange(seq)[None, :]
    return jnp.asarray(np.cos(ang), dtype=F32), jnp.asarray(np.sin(ang), dtype=F32)


def _moba_prep_kernel(att_ref, qg_ref, kg_ref, cos_ref, sin_ref,
                      qt_ref, kn_ref, vt_ref, sel_ref, kmean_ref):
    i = pl.program_id(1)
    nb = kmean_ref.shape[1]

    @pl.when(i == 0)
    def _():
        kmean_ref[...] = jnp.zeros(kmean_ref.shape, F32)

    cos = cos_ref[...]
    sin = sin_ref[...]

    def norm_rot(x, g):
        ms = jnp.mean(x * x, axis=0, keepdims=True)
        xn = x * lax.rsqrt(ms + NORM_EPS) * g
        x1 = xn[0:ROPE_HALF]
        x2 = xn[ROPE_HALF:2 * ROPE_HALF]
        return jnp.concatenate([x1 * cos - x2 * sin, x1 * sin + x2 * cos, xn[2 * ROPE_HALF:]], axis=0)

    row = lax.broadcasted_iota(jnp.int32, (nb, MOBA_BLOCK), 0)
    past = row < i
    for h in range(ATT_HEADS):
        hs = slice(h * ATT_HEAD_DIM, (h + 1) * ATT_HEAD_DIM)
        q = norm_rot(att_ref[hs, :], qg_ref[...])
        k = norm_rot(att_ref[ATT_WIDTH + h * ATT_HEAD_DIM:ATT_WIDTH + (h + 1) * ATT_HEAD_DIM, :], kg_ref[...])
        kn = k.T
        kn_ref[0, h] = kn.astype(BF16)
        qt_ref[0, hs, :] = (q * ATT_Q_SCALE).astype(BF16)
        v0 = 2 * ATT_WIDTH + h * ATT_HEAD_DIM
        vt_ref[0, h * ATT_V_ROWS:h * ATT_V_ROWS + ATT_HEAD_DIM, :] = att_ref[v0:v0 + ATT_HEAD_DIM, :].astype(BF16)
        vt_ref[0, h * ATT_V_ROWS + ATT_HEAD_DIM:(h + 1) * ATT_V_ROWS, :] = jnp.ones(
            (ATT_V_ROWS - ATT_HEAD_DIM, MOBA_BLOCK), BF16)
        kmean_ref[h, pl.ds(i, 1), :] = jnp.mean(kn, axis=0, keepdims=True)
        gate = jnp.dot(kmean_ref[h], q, preferred_element_type=F32, precision=lax.Precision.HIGHEST)
        gate = jnp.where(past, gate, NEG_INF)
        beaten = jnp.zeros(gate.shape, F32)
        for j in range(nb):
            gj = jnp.broadcast_to(gate[j:j + 1, :], gate.shape)
            ahead = jnp.where(gj > gate, 1.0, jnp.where(gj == gate, jnp.where(row > j, 1.0, 0.0), 0.0))
            beaten = beaten + ahead
        sel_ref[0, h] = jnp.where(past, jnp.where(beaten < MOBA_TOPK, 1.0, 0.0), 0.0)


def _moba_prep(att_t, qn_g, kn_g, tables, bsz, seq):
    nb = seq // MOBA_BLOCK
    cos_t, sin_t = tables
    return pl.pallas_call(
        _moba_prep_kernel,
        grid=(bsz, nb),
        in_specs=[
            pl.BlockSpec((N_ATT, MOBA_BLOCK), lambda b, i: (0, b * nb + i)),
            pl.BlockSpec((ATT_HEAD_DIM, 1), lambda b, i: (0, 0)),
            pl.BlockSpec((ATT_HEAD_DIM, 1), lambda b, i: (0, 0)),
            pl.BlockSpec((ROPE_HALF, MOBA_BLOCK), lambda b, i: (0, i)),
            pl.BlockSpec((ROPE_HALF, MOBA_BLOCK), lambda b, i: (0, i)),
        ],
        out_specs=(
            pl.BlockSpec((1, ATT_WIDTH, MOBA_BLOCK), lambda b, i: (b, 0, i)),
            pl.BlockSpec((1, ATT_HEADS, MOBA_BLOCK, ATT_HEAD_DIM), lambda b, i: (b, 0, i, 0)),
            pl.BlockSpec((1, ATT_HEADS * ATT_V_ROWS, MOBA_BLOCK), lambda b, i: (b, 0, i)),
            pl.BlockSpec((1, ATT_HEADS, nb, MOBA_BLOCK), lambda b, i: (b, 0, 0, i)),
        ),
        out_shape=(
            jax.ShapeDtypeStruct((bsz, ATT_WIDTH, seq), BF16),
            jax.ShapeDtypeStruct((bsz, ATT_HEADS, seq, ATT_HEAD_DIM), BF16),
            jax.ShapeDtypeStruct((bsz, ATT_HEADS * ATT_V_ROWS, seq), BF16),
            jax.ShapeDtypeStruct((bsz, ATT_HEADS, nb, seq), F32),
        ),
        scratch_shapes=[pltpu.VMEM((ATT_HEADS, nb, ATT_HEAD_DIM), F32)],
        compiler_params=_cparams(2),
        name="moba_prep",
    )(att_t, qn_g.reshape(ATT_HEAD_DIM, 1), kn_g.reshape(ATT_HEAD_DIM, 1), cos_t, sin_t)


def _moba_attn_kernel(qt_ref, k_ref, vt_ref, sel_ref, o_ref, m_ref, acc_ref, s_ref, pv_ref, mb_ref):
    blk = MOBA_BLOCK
    hd = ATT_HEAD_DIM
    nb = sel_ref.shape[2]
    heads = k_ref.shape[1]
    vrows = ATT_V_ROWS
    kpos = lax.broadcasted_iota(jnp.int32, (blk, blk), 0)
    qpos = lax.broadcasted_iota(jnp.int32, (blk, blk), 1)
    causal = kpos <= qpos

    def q_block(i, carry):
        q0 = pl.multiple_of(i * blk, blk)

        def scores(h, j0):
            return jnp.dot(k_ref[0, h, pl.ds(j0, blk), :], qt_ref[0, h * hd:(h + 1) * hd, pl.ds(q0, blk)],
                           preferred_element_type=F32)

        def values(h, j0):
            return vt_ref[0, h * vrows:(h + 1) * vrows, pl.ds(j0, blk)]

        def local_softmax(s, m_blk, picked):
            shift = m_blk if picked is None else jnp.where(picked, m_blk, -NEG_INF)
            p = jnp.exp2((s - shift).astype(BF16))
            return p, (m_blk if picked is None else jnp.where(picked, m_blk, NEG_INF))

        own = [scores(h, q0) for h in range(heads)]
        for h in range(heads):
            s_ref[0, h] = scores(h, 0)
            pv_ref[1, h] = jnp.zeros((vrows, blk), F32)
            mb_ref[1, h] = jnp.full((1, blk), NEG_INF, F32)
        for h in range(heads):
            s = jnp.where(causal, own[h], NEG_INF)
            p, m = local_softmax(s, jnp.max(s, axis=0, keepdims=True), None)
            m_ref[h] = m
            acc_ref[h] = jnp.dot(values(h, q0), p, preferred_element_type=F32)

        def fold(slot):
            for h in range(heads):
                m_blk = mb_ref[slot, h]
                m_old = m_ref[h]
                m_new = jnp.maximum(m_old, m_blk)
                m_ref[h] = m_new
                acc_ref[h] = jnp.exp2(m_old - m_new) * acc_ref[h] + jnp.exp2(m_blk - m_new) * pv_ref[slot, h]

        def step(j, cur, nxt, live):
            jn0 = pl.multiple_of(jnp.minimum(j + 1, nb - 1) * blk, blk)
            for h in range(heads):
                s_ref[nxt, h] = scores(h, jn0)
            jc = jnp.minimum(j, nb - 1)
            jc0 = pl.multiple_of(jc * blk, blk)
            for h in range(heads):
                picked = jnp.logical_and(sel_ref[0, h, pl.ds(jc, 1), pl.ds(q0, blk)] > 0.5, live)
                s = s_ref[cur, h]
                p, m = local_softmax(s, jnp.max(s, axis=0, keepdims=True), picked)
                pv_ref[cur, h] = jnp.dot(values(h, jc0), p, preferred_element_type=F32)
                mb_ref[cur, h] = m
            fold(nxt)

        def block_pair(t, carry):
            step(2 * t, 0, 1, True)
            step(2 * t + 1, 1, 0, 2 * t + 1 < i)
            return carry

        lax.fori_loop(0, (i + 1) // 2, block_pair, 0)
        fold(1)
        for h in range(heads):
            acc = acc_ref[h]
            o_ref[0, h * hd:(h + 1) * hd, pl.ds(q0, blk)] = acc[:hd] / acc[hd:hd + 1]
        return carry

    lax.fori_loop(0, nb, q_block, 0)


def _moba_attn(qt, kn, vt, sel):
    bsz, _, seq = qt.shape
    nb = seq // MOBA_BLOCK
    hg = ATT_HEADS_PER_STEP
    return pl.pallas_call(
        _moba_attn_kernel,
        grid=(bsz, ATT_HEADS // hg),
        in_specs=[
            pl.BlockSpec((1, hg * ATT_HEAD_DIM, seq), lambda b, h: (b, h, 0)),
            pl.BlockSpec((1, hg, seq, ATT_HEAD_DIM), lambda b, h: (b, h, 0, 0)),
            pl.BlockSpec((1, hg * ATT_V_ROWS, seq), lambda b, h: (b, h, 0)),
            pl.BlockSpec((1, hg, nb, seq), lambda b, h: (b, h, 0, 0)),
        ],
        out_specs=pl.BlockSpec((1, hg * ATT_HEAD_DIM, seq), lambda b, h: (b, h, 0)),
        out_shape=jax.ShapeDtypeStruct((bsz, ATT_WIDTH, seq), F32),
        scratch_shapes=[pltpu.VMEM((hg, 1, MOBA_BLOCK), F32),
                        pltpu.VMEM((hg, ATT_V_ROWS, MOBA_BLOCK), F32),
                        pltpu.VMEM((2, hg, MOBA_BLOCK, MOBA_BLOCK), F32),
                        pltpu.VMEM((2, hg, ATT_V_ROWS, MOBA_BLOCK), F32),
                        pltpu.VMEM((2, hg, 1, MOBA_BLOCK), F32)],
        compiler_params=_cparams(2),
        name="moba_attn",
    )(qt, kn, vt, sel)


def _merge_kernel(x_ref, g_ref, ys_ref, yr_ref, at_ref, wg_ref, ws_ref, wr_ref, wa_ref, wo_ref, o_ref):
    x = x_ref[...]
    xn = _rms_rows(x, g_ref[...]).astype(BF16)

    def gate(k):
        return _sigmoid(jnp.dot(xn, wg_ref[:, k * D_MODEL:(k + 1) * D_MODEL], preferred_element_type=F32))

    ys = jnp.concatenate([ys_ref[c] for c in range(ys_ref.shape[0])], axis=1)
    merged = gate(0) * jnp.dot(ys.astype(BF16), ws_ref[...], preferred_element_type=F32)
    merged += gate(1) * jnp.dot(yr_ref[...].astype(BF16), wr_ref[...], preferred_element_type=F32)
    merged += gate(2) * lax.dot_general(at_ref[0].astype(BF16), wa_ref[...], (((0,), (0,)), ((), ())),
                                        preferred_element_type=F32)
    o_ref[...] = x + jnp.dot(merged.astype(BF16), wo_ref[...], preferred_element_type=F32)


def _merge(x2d, g, y_s, y_r, o_t, w_gates, w_s, w_r, w_a, w_o, bsz, seq):
    tm = TM_MERGE
    per_seq = seq // tm
    row = lambda i: (i, 0)
    const = lambda i: (0, 0)
    return pl.pallas_call(
        _merge_kernel,
        grid=(bsz * per_seq,),
        in_specs=[
            pl.BlockSpec((tm, D_MODEL), row),
            pl.BlockSpec((1, D_MODEL), const),
            pl.BlockSpec((SSM_WIDTH // LANES, tm, LANES), lambda i: (0, i, 0)),
            pl.BlockSpec((tm, RET_V_WIDTH), row),
            pl.BlockSpec((1, ATT_WIDTH, tm), lambda i: (i // per_seq, 0, i % per_seq)),
            pl.BlockSpec((D_MODEL, N_GATES), const),
            pl.BlockSpec((SSM_WIDTH, D_MODEL), const),
            pl.BlockSpec((RET_V_WIDTH, D_MODEL), const),
            pl.BlockSpec((ATT_WIDTH, D_MODEL), const),
            pl.BlockSpec((D_MODEL, D_MODEL), const),
        ],
        out_specs=pl.BlockSpec((tm, D_MODEL), row),
        out_shape=jax.ShapeDtypeStruct(x2d.shape, F32),
        compiler_params=_cparams(1),
        name="merge",
    )(x2d, g, y_s, y_r, o_t, w_gates, w_s, w_r, w_a, w_o)


def _ffn_kernel(x_ref, halo_ref, g_ref, wg_ref, wu_ref, cwg_ref, cwu_ref, cbg_ref, cbu_ref,
                wd_ref, o_ref, xn_ref, acc_ref, *, tiles_per_seq):
    i = pl.program_id(0)
    c = pl.program_id(1)
    tm = x_ref.shape[0]

    @pl.when(c == 0)
    def _():
        g = g_ref[...]
        xn_ref[0:HALO, :] = _rms_rows(halo_ref[...], g).astype(BF16)
        xn_ref[HALO:, :] = _rms_rows(x_ref[...], g).astype(BF16)
        acc_ref[...] = jnp.zeros(acc_ref.shape, F32)

    rowi = lax.broadcasted_iota(jnp.int32, (tm + HALO, 1), 0)
    live = jnp.logical_or(rowi >= HALO, i % tiles_per_seq != 0)

    def conv(w_ref, cw_ref, cb_ref):
        h = jnp.dot(xn_ref[...], w_ref[...], preferred_element_type=F32)
        h = jnp.where(live, h, 0.0)
        cw = cw_ref[...]
        return (cb_ref[...] + h[HALO - 2:tm + HALO - 2] * cw[0:1] + h[HALO - 1:tm + HALO - 1] * cw[1:2]
                + h[HALO:] * cw[2:3])

    hg = conv(wg_ref, cwg_ref, cbg_ref)
    hu = conv(wu_ref, cwu_ref, cbu_ref)
    act = (hg * _sigmoid(hg) * hu).astype(BF16)
    acc_ref[...] += jnp.dot(act, wd_ref[...], preferred_element_type=F32)

    @pl.when(c == pl.num_programs(1) - 1)
    def _():
        o_ref[...] = x_ref[...] + acc_ref[...]


def _ffn(x2d, g, w_up, conv_w, conv_b, w_down, seq):
    t = x2d.shape[0]
    tm, fc = TM_FFN, FC_FFN
    nfc = FFN_HIDDEN // fc
    halo_blocks = tm // HALO
    return pl.pallas_call(
        functools.partial(_ffn_kernel, tiles_per_seq=seq // tm),
        grid=(t // tm, nfc),
        in_specs=[
            pl.BlockSpec((tm, D_MODEL), lambda i, c: (i, 0)),
            pl.BlockSpec((HALO, D_MODEL), lambda i, c: (jnp.maximum(i * halo_blocks - 1, 0), 0)),
            pl.BlockSpec((1, D_MODEL), lambda i, c: (0, 0)),
            pl.BlockSpec((D_MODEL, fc), lambda i, c: (0, c)),
            pl.BlockSpec((D_MODEL, fc), lambda i, c: (0, c + nfc)),
            pl.BlockSpec((CONV_WIDTH, fc), lambda i, c: (0, c)),
            pl.BlockSpec((CONV_WIDTH, fc), lambda i, c: (0, c + nfc)),
            pl.BlockSpec((1, fc), lambda i, c: (0, c)),
            pl.BlockSpec((1, fc), lambda i, c: (0, c + nfc)),
            pl.BlockSpec((fc, D_MODEL), lambda i, c: (c, 0)),
        ],
        out_specs=pl.BlockSpec((tm, D_MODEL), lambda i, c: (i, 0)),
        out_shape=jax.ShapeDtypeStruct(x2d.shape, F32),
        scratch_shapes=[pltpu.VMEM((tm + HALO, D_MODEL), BF16), pltpu.VMEM((tm, D_MODEL), F32)],
        compiler_params=_cparams(2),
        name="ffn",
    )(x2d, x2d, g, w_up, w_up, conv_w, conv_w, conv_b, conv_b, w_down)


def kernel(x, norm1_g, w_in, ssm_lambda_re, ssm_lambda_im, ssm_log_dt, ssm_b_re, ssm_b_im, ssm_c_re, ssm_c_im, ssm_d, ssm_w_glu, ret_gn_g, attn_qn_g, attn_kn_g, w_br_ssm, w_br_ret, w_br_att, w_o, norm2_g, ffn_w_up, ffn_conv_w, ffn_conv_b, ffn_w_down):
    bsz, seq, _ = x.shape
    depth = w_in.shape[0]
    assert seq % MOBA_BLOCK == 0 and seq % TM_PROJ == 0 and seq % TM_FFN == 0
    ret_tables = _ret_tables(seq)
    moba_tables = _moba_tables(seq)
    gpb = SSM_GROUPS // S5_BLOCKS
    eye = jnp.eye(gpb, dtype=F32)

    def embed_c(c):
        c4 = c.reshape(S5_BLOCKS, gpb, SSM_GROUP, SSM_STATE)
        return jnp.einsum('bghp,gk->bgpkh', c4, eye).reshape(S5_BLOCKS, S5_BLK_STATES, S5_BLK_CH)

    x2d = x.reshape(bsz * seq, D_MODEL)
    for l in range(depth):
        w = w_in[l]
        w_main = w[:, :N_MAIN].astype(BF16)
        w_att_t = w[:, N_MAIN:N_MAIN + N_ATT].T.astype(BF16)
        w_gates = w[:, N_MAIN + N_ATT:].astype(BF16)
        norm1 = norm1_g[l].reshape(1, D_MODEL)
        cmat = jnp.concatenate([embed_c(ssm_c_re[l]), -embed_c(ssm_c_im[l])], axis=1).astype(BF16)

        bmat, coef_a, coef_s, ptab = _s5_prep(ssm_lambda_re[l], ssm_lambda_im[l], ssm_log_dt[l],
                                              ssm_b_re[l], ssm_b_im[l])
        u3, proj, att_t = _in_proj(x2d, norm1, w_main, w_att_t)
        y_s = _s5(u3, bmat, coef_a, coef_s, ptab, cmat, ssm_d[l].reshape(1, SSM_WIDTH),
                  ssm_w_glu[l].astype(BF16), bsz, seq)
        y_r = _retention(proj, ret_gn_g[l].reshape(1, RET_V_WIDTH), ret_tables, bsz, seq)
        qt, kn, vt, sel = _moba_prep(att_t, attn_qn_g[l], attn_kn_g[l], moba_tables, bsz, seq)
        o_t = _moba_attn(qt, kn, vt, sel)
        x2d = _merge(x2d, norm1, y_s, y_r, o_t, w_gates, w_br_ssm[l].astype(BF16), w_br_ret[l].astype(BF16),
                     w_br_att[l].astype(BF16), w_o[l].astype(BF16), bsz, seq)
        x2d = _ffn(x2d, norm2_g[l].reshape(1, D_MODEL), ffn_w_up[l].astype(BF16), ffn_conv_w[l],
                   ffn_conv_b[l].reshape(1, 2 * FFN_HIDDEN), ffn_w_down[l].astype(BF16), seq)
    return x2d.reshape(bsz, seq, D_MODEL)
```

```python
import functools
import math

import numpy as np
import jax
import jax.numpy as jnp
from jax import lax
from jax.experimental import pallas as pl
from jax.experimental.pallas import tpu as pltpu

F32 = jnp.float32
BF16 = jnp.bfloat16

D_MODEL = 1024
SSM_WIDTH = 512
SSM_GROUP = 16
SSM_GROUPS = 32
SSM_STATE = 64
SSM_NSTATE = SSM_GROUPS * SSM_STATE
RET_HEADS = 4
RET_QK_DIM = 64
RET_V_DIM = 128
RET_QK_WIDTH = 256
RET_V_WIDTH = 512
RET_THETA = 10000.0
ATT_HEADS = 8
ATT_HEAD_DIM = 64
ATT_WIDTH = 512
MOBA_BLOCK = 256
MOBA_TOPK = 3
ROPE_THETA = 500000.0
ROPE_HALF = 8
FFN_HIDDEN = 2816
CONV_WIDTH = 3
NORM_EPS = 1e-6
NEG_INF = -1e30

N_MAIN = SSM_WIDTH + 2 * RET_QK_WIDTH + 2 * RET_V_WIDTH
N_ATT = 3 * ATT_WIDTH
N_GATES = 3 * D_MODEL

SUBLANES = 8
LANES = 128
VMEM_LIMIT = 52 * 1024 * 1024

TM_PROJ = 512
LS_S5 = 256
S5_TSUB = LS_S5 // SUBLANES
S5_BLOCKS = 2
S5_BLK_CH = SSM_WIDTH // S5_BLOCKS
S5_BLK_STATES = SSM_NSTATE // S5_BLOCKS
S5_SCAN_COLS = 4
RET_CHUNK = 256
TM_MERGE = 512
TM_FFN = 512
FC_FFN = 2816
HALO = SUBLANES
ATT_HEADS_PER_STEP = 4
ATT_V_ROWS = ATT_HEAD_DIM + 16
ATT_Q_SCALE = math.log2(math.e) * ATT_HEAD_DIM ** -0.5


def _cparams(n_axes):
    return pltpu.CompilerParams(dimension_semantics=("arbitrary",) * n_axes,
                                vmem_limit_bytes=VMEM_LIMIT)


def _sigmoid(x):
    return 1.0 / (1.0 + jnp.exp(-x))


def _rms_rows(x, g):
    ms = jnp.mean(x * x, axis=-1, keepdims=True)
    return x * lax.rsqrt(ms + NORM_EPS) * g


def _s5_prep_kernel(lre_ref, lim_ref, ldt_ref, bre_ref, bim_ref, bmat_ref, ca_ref, cs_ref, pt_ref):
    lre = lre_ref[...]
    lim = lim_ref[...]
    dt = jnp.exp(ldt_ref[...])

    def lam_pow(k):
        mag = jnp.exp(k * lre * dt)
        ang = k * lim * dt
        return mag * jnp.cos(ang), mag * jnp.sin(ang)

    ar, ai = lam_pow(1.0)
    x = ar - 1.0
    den = lre * lre + lim * lim
    f_re = (x * lre + ai * lim) / den
    f_im = (ai * lre - x * lim) / den
    for bk in range(S5_BLOCKS):
        sl = slice(bk * S5_BLK_STATES, (bk + 1) * S5_BLK_STATES)
        bre = bre_ref[bk]
        bim = bim_ref[bk]
        bmat_ref[bk, :, :S5_BLK_STATES] = (f_re[:, sl] * bre - f_im[:, sl] * bim).astype(BF16)
        bmat_ref[bk, :, S5_BLK_STATES:] = (f_re[:, sl] * bim + f_im[:, sl] * bre).astype(BF16)

    tile = (SUBLANES, SSM_NSTATE)
    ca_ref[0:8, :] = jnp.broadcast_to(ar, tile)
    ca_ref[8:16, :] = jnp.broadcast_to(ai, tile)
    row = lax.broadcasted_iota(jnp.int32, tile, 0)
    for idx, k in enumerate((1, 2, 4)):
        pr, pi = lam_pow(float(k * S5_TSUB))
        keep = row >= k
        cs_ref[idx * 16:idx * 16 + 8, :] = jnp.where(keep, jnp.broadcast_to(pr, tile), 0.0)
        cs_ref[idx * 16 + 8:idx * 16 + 16, :] = jnp.where(keep, jnp.broadcast_to(pi, tile), 0.0)
    pr, pi = lam_pow(float(S5_TSUB))
    cs_ref[48:56, :] = jnp.broadcast_to(pr, tile)
    cs_ref[56:64, :] = jnp.broadcast_to(pi, tile)
    steps = (lax.broadcasted_iota(jnp.int32, (S5_TSUB, SSM_NSTATE), 0) + 1).astype(F32)
    pr, pi = lam_pow(steps)
    for t in range(S5_TSUB):
        pt_ref[0, t * SUBLANES:(t + 1) * SUBLANES, :] = jnp.broadcast_to(pr[t:t + 1], tile)
        pt_ref[1, t * SUBLANES:(t + 1) * SUBLANES, :] = jnp.broadcast_to(pi[t:t + 1], tile)


def _s5_prep(lam_re, lam_im, log_dt, b_re, b_im):
    gpb = SSM_GROUPS // S5_BLOCKS
    eye = jnp.eye(gpb, dtype=F32)

    def embed(b):
        b4 = b.reshape(S5_BLOCKS, gpb, SSM_STATE, SSM_GROUP)
        return jnp.einsum('bgph,gk->bghkp', b4, eye).reshape(S5_BLOCKS, S5_BLK_CH, S5_BLK_STATES)

    ldt = jnp.repeat(log_dt, SSM_STATE).reshape(1, SSM_NSTATE)
    return pl.pallas_call(
        _s5_prep_kernel,
        out_shape=(jax.ShapeDtypeStruct((S5_BLOCKS, S5_BLK_CH, 2 * S5_BLK_STATES), BF16),
                   jax.ShapeDtypeStruct((16, SSM_NSTATE), F32),
                   jax.ShapeDtypeStruct((64, SSM_NSTATE), F32),
                   jax.ShapeDtypeStruct((2, LS_S5, SSM_NSTATE), F32)),
        compiler_params=pltpu.CompilerParams(vmem_limit_bytes=VMEM_LIMIT),
        name="s5_prep",
    )(lam_re.reshape(1, SSM_NSTATE), lam_im.reshape(1, SSM_NSTATE), ldt, embed(b_re), embed(b_im))


def _in_proj_kernel(x_ref, g_ref, wm_ref, wat_ref, u_ref, proj_ref, att_ref, us_ref):
    tm = x_ref.shape[0]
    xn = _rms_rows(x_ref[...], g_ref[...]).astype(BF16)
    att_ref[...] = lax.dot_general(wat_ref[...], xn, (((1,), (1,)), ((), ())),
                                   preferred_element_type=F32)
    proj = jnp.dot(xn, wm_ref[...], preferred_element_type=F32)
    proj_ref[...] = proj[:, SSM_WIDTH:]
    for c in range(SSM_WIDTH // LANES):
        us_ref[c] = proj[:, c * LANES:(c + 1) * LANES]
    for chunk in range(tm // LS_S5):
        for t in range(S5_TSUB):
            r = chunk * LS_S5 + t * SUBLANES
            for c in range(SSM_WIDTH // LANES):
                u_ref[c, r:r + SUBLANES, :] = us_ref[c, pl.ds(chunk * LS_S5 + t, SUBLANES, stride=S5_TSUB), :]


def _in_proj(x2d, g, w_main, w_att_t):
    t = x2d.shape[0]
    return pl.pallas_call(
        _in_proj_kernel,
        grid=(t // TM_PROJ,),
        in_specs=[
            pl.BlockSpec((TM_PROJ, D_MODEL), lambda i: (i, 0)),
            pl.BlockSpec((1, D_MODEL), lambda i: (0, 0)),
            pl.BlockSpec((D_MODEL, N_MAIN), lambda i: (0, 0)),
            pl.BlockSpec((N_ATT, D_MODEL), lambda i: (0, 0)),
        ],
        out_specs=(
            pl.BlockSpec((SSM_WIDTH // LANES, TM_PROJ, LANES), lambda i: (0, i, 0)),
            pl.BlockSpec((TM_PROJ, N_MAIN - SSM_WIDTH), lambda i: (i, 0)),
            pl.BlockSpec((N_ATT, TM_PROJ), lambda i: (0, i)),
        ),
        out_shape=(jax.ShapeDtypeStruct((SSM_WIDTH // LANES, t, LANES), F32),
                   jax.ShapeDtypeStruct((t, N_MAIN - SSM_WIDTH), F32),
                   jax.ShapeDtypeStruct((N_ATT, t), F32)),
        scratch_shapes=[pltpu.VMEM((SSM_WIDTH // LANES, TM_PROJ, LANES), F32)],
        compiler_params=_cparams(1),
        name="in_proj",
    )(x2d, g, w_main, w_att_t)


def _s5_kernel(u_ref, bmat_ref, ca_ref, cs_ref, pt_ref, cmat_ref, d_ref, wglu_ref, o_ref,
               st_ref, xb_ref, cin_ref):
    ls = st_ref.shape[0]
    tsub = ls // SUBLANES
    nslab = SSM_WIDTH // LANES

    @pl.when(pl.program_id(1) == 0)
    def _():
        cin_ref[...] = jnp.zeros(cin_ref.shape, F32)

    u = jnp.concatenate([u_ref[c] for c in range(nslab)], axis=1)
    ub = u.astype(BF16)
    for bk in range(S5_BLOCKS):
        st_ref[:, bk * 2 * S5_BLK_STATES:(bk + 1) * 2 * S5_BLK_STATES] = jnp.dot(
            ub[:, bk * S5_BLK_CH:(bk + 1) * S5_BLK_CH], bmat_ref[bk], preferred_element_type=F32)

    row = lax.broadcasted_iota(jnp.int32, (SUBLANES, LANES), 0)
    tile = (SUBLANES, LANES)

    def columns(cb):
        per_blk = S5_BLK_STATES // LANES
        base = (cb // per_blk) * 2 * S5_BLK_STATES + (cb % per_blk) * LANES
        return (slice(base, base + LANES), slice(base + S5_BLK_STATES, base + S5_BLK_STATES + LANES),
                slice(cb * LANES, (cb + 1) * LANES))

    def cmul(ar, ai, xr, xi):
        return ar * xr - ai * xi, ar * xi + ai * xr

    def scan_columns(cbs):
        sl = [columns(cb) for cb in cbs]
        a = [(ca_ref[0:8, s], ca_ref[8:16, s]) for (_, _, s) in sl]
        first = []
        for re, im, _ in sl:
            first += [st_ref[0:SUBLANES, re], st_ref[0:SUBLANES, im]]

        def local_step(t, carry):
            r0 = pl.multiple_of(t * SUBLANES, SUBLANES)
            out = []
            for k, (re, im, _) in enumerate(sl):
                pr, pi = cmul(a[k][0], a[k][1], carry[2 * k], carry[2 * k + 1])
                nr = st_ref[pl.ds(r0, SUBLANES), re] + pr
                ni = st_ref[pl.ds(r0, SUBLANES), im] + pi
                st_ref[pl.ds(r0, SUBLANES), re] = nr
                st_ref[pl.ds(r0, SUBLANES), im] = ni
                out += [nr, ni]
            return tuple(out)

        ends = lax.fori_loop(1, tsub, local_step, tuple(first))

        entering = []
        for k, (re, im, s) in enumerate(sl):
            fr, fi = ends[2 * k], ends[2 * k + 1]
            gr = jnp.where(row == 0, cin_ref[:, re], pltpu.roll(fr, 1, 0))
            gi = jnp.where(row == 0, cin_ref[:, im], pltpu.roll(fi, 1, 0))
            for idx, kk in enumerate((1, 2, 4)):
                pr, pi = cmul(cs_ref[idx * 16:idx * 16 + 8, s], cs_ref[idx * 16 + 8:idx * 16 + 16, s],
                              pltpu.roll(gr, kk, 0), pltpu.roll(gi, kk, 0))
                gr, gi = gr + pr, gi + pi
            pr, pi = cmul(cs_ref[48:56, s], cs_ref[56:64, s], gr, gi)
            tr, ti = fr + pr, fi + pi
            cin_ref[:, re] = jnp.broadcast_to(tr[SUBLANES - 1:], tile)
            cin_ref[:, im] = jnp.broadcast_to(ti[SUBLANES - 1:], tile)
            entering += [gr, gi]

        def fix_step(tp, carry):
            r16 = pl.multiple_of(tp * 2 * SUBLANES, 2 * SUBLANES)
            for k, (re, im, s) in enumerate(sl):
                halves_r, halves_i = [], []
                for half in range(2):
                    r0 = pl.multiple_of(r16 + half * SUBLANES, SUBLANES)
                    pr, pi = cmul(pt_ref[0, pl.ds(r0, SUBLANES), s], pt_ref[1, pl.ds(r0, SUBLANES), s],
                                  entering[2 * k], entering[2 * k + 1])
                    halves_r.append(st_ref[pl.ds(r0, SUBLANES), re] + pr)
                    halves_i.append(st_ref[pl.ds(r0, SUBLANES), im] + pi)
                xb_ref[pl.ds(r16, 2 * SUBLANES), re] = jnp.concatenate(halves_r, axis=0).astype(BF16)
                xb_ref[pl.ds(r16, 2 * SUBLANES), im] = jnp.concatenate(halves_i, axis=0).astype(BF16)
            return carry

        lax.fori_loop(0, tsub // 2, fix_step, 0)

    for grp in range(SSM_NSTATE // LANES // S5_SCAN_COLS):
        scan_columns(range(grp * S5_SCAN_COLS, (grp + 1) * S5_SCAN_COLS))

    y = jnp.concatenate(
        [jnp.dot(xb_ref[:, bk * 2 * S5_BLK_STATES:(bk + 1) * 2 * S5_BLK_STATES], cmat_ref[bk],
                 preferred_element_type=F32) for bk in range(S5_BLOCKS)], axis=1)
    y = y + d_ref[...] * u
    y = 0.5 * y * (1.0 + jnp.tanh(0.7978845608028654 * (y + 0.044715 * (y * y * y))))
    glu = jnp.dot(y.astype(BF16), wglu_ref[...], preferred_element_type=F32)
    out = y * _sigmoid(glu)
    for c in range(nslab):
        o_ref[c] = out[:, c * LANES:(c + 1) * LANES]


def _s5(u3, bmat, coef_a, coef_s, ptab, cmat, d_skip, w_glu, bsz, seq):
    nchunk = seq // LS_S5
    nslab = SSM_WIDTH // LANES
    const2 = lambda b, c: (0, 0)
    const3 = lambda b, c: (0, 0, 0)
    return pl.pallas_call(
        _s5_kernel,
        grid=(bsz, nchunk),
        in_specs=[
            pl.BlockSpec((nslab, LS_S5, LANES), lambda b, c: (0, b * nchunk + c, 0)),
            pl.BlockSpec((S5_BLOCKS, S5_BLK_CH, 2 * S5_BLK_STATES), const3),
            pl.BlockSpec((16, SSM_NSTATE), const2),
            pl.BlockSpec((64, SSM_NSTATE), const2),
            pl.BlockSpec((2, LS_S5, SSM_NSTATE), const3),
            pl.BlockSpec((S5_BLOCKS, 2 * S5_BLK_STATES, S5_BLK_CH), const3),
            pl.BlockSpec((1, SSM_WIDTH), const2),
            pl.BlockSpec((SSM_WIDTH, SSM_WIDTH), const2),
        ],
        out_specs=pl.BlockSpec((nslab, LS_S5, LANES), lambda b, c: (0, b * nchunk + c, 0)),
        out_shape=jax.ShapeDtypeStruct((nslab, bsz * seq, LANES), F32),
        scratch_shapes=[pltpu.VMEM((LS_S5, 2 * SSM_NSTATE), F32),
                        pltpu.VMEM((LS_S5, 2 * SSM_NSTATE), BF16),
                        pltpu.VMEM((SUBLANES, 2 * SSM_NSTATE), F32)],
        compiler_params=_cparams(2),
        name="s5",
    )(u3, bmat, coef_a, coef_s, ptab, cmat, d_skip, w_glu)


def _ret_tables(seq):
    c = RET_CHUNK
    half = RET_QK_DIM // 2
    inv = 1.0 / (RET_THETA ** np.linspace(0.0, 1.0, half))
    ang = np.arange(seq)[:, None] * inv[None, :]
    cos, sin = np.cos(ang), np.sin(ang)
    cos_full = np.tile(np.concatenate([cos, cos], axis=1), (1, RET_HEADS))
    sin_sgn = np.tile(np.concatenate([-sin, sin], axis=1), (1, RET_HEADS))
    log_gamma = np.log1p(-np.exp2(-5.0 - np.arange(RET_HEADS)))
    pos = np.arange(c)
    rel = pos[:, None] - pos[None, :]
    decay = np.where(rel >= 0, np.exp(log_gamma[:, None, None] * np.maximum(rel, 0)[None]), 0.0)
    zeta = np.exp(log_gamma[None, :] * (c - 1.0 - pos)[:, None])
    xi = np.exp(log_gamma[None, :] * (pos + 1.0)[:, None])
    zeta = np.repeat(zeta, RET_QK_DIM, axis=1)
    xi = np.repeat(xi, RET_QK_DIM, axis=1)
    cdec = np.repeat(np.exp(log_gamma * c), RET_V_DIM)[None, :]
    as32 = lambda a: jnp.asarray(a, dtype=F32)
    return as32(cos_full), as32(sin_sgn), as32(decay), as32(zeta), as32(xi), as32(cdec)


def _ret_kernel(q_ref, k_ref, v_ref, g_ref, cos_ref, sin_ref, decay_ref, zeta_ref, xi_ref,
                cdec_ref, gn_ref, o_ref, state_ref):
    @pl.when(pl.program_id(1) == 0)
    def _():
        state_ref[...] = jnp.zeros(state_ref.shape, F32)

    cos = cos_ref[...]
    sin = sin_ref[...]
    lane = lax.broadcasted_iota(jnp.int32, cos.shape, 1)
    first_half = (lane % RET_QK_DIM) < (RET_QK_DIM // 2)

    def rot(x):
        swapped = jnp.where(first_half, pltpu.roll(x, RET_QK_WIDTH - RET_QK_DIM // 2, 1),
                            pltpu.roll(x, RET_QK_DIM // 2, 1))
        return x * cos + swapped * sin

    q = rot(q_ref[...])
    k = rot(k_ref[...]) * (RET_QK_DIM ** -0.5)
    qx = (q * xi_ref[...]).astype(BF16)
    kz = (k * zeta_ref[...]).astype(BF16)
    qb = q.astype(BF16)
    kb = k.astype(BF16)
    vb = v_ref[...].astype(BF16)
    g = g_ref[...]
    gn = gn_ref[...]
    cdec = cdec_ref[...]
    for h in range(RET_HEADS):
        qs = slice(h * RET_QK_DIM, (h + 1) * RET_QK_DIM)
        vs = slice(h * RET_V_DIM, (h + 1) * RET_V_DIM)
        s = lax.dot_general(qb[:, qs], kb[:, qs], (((1,), (1,)), ((), ())),
                            preferred_element_type=F32) * decay_ref[h]
        state = state_ref[h]
        o = (jnp.dot(s.astype(BF16), vb[:, vs], preferred_element_type=F32)
             + jnp.dot(qx[:, qs], state.astype(BF16), preferred_element_type=F32))
        kv = lax.dot_general(kz[:, qs], vb[:, vs], (((0,), (0,)), ((), ())),
                             preferred_element_type=F32)
        state_ref[h] = cdec[:, vs] * state + kv
        mu = jnp.mean(o, axis=-1, keepdims=True)
        oc = o - mu
        var = jnp.mean(oc * oc, axis=-1, keepdims=True)
        on = oc * lax.rsqrt(var + NORM_EPS) * gn[:, vs]
        gh = g[:, vs]
        o_ref[:, vs] = gh * _sigmoid(gh) * on


def _retention(proj, gn_g, tables, bsz, seq):
    c = RET_CHUNK
    nchunk = seq // c
    cos_full, sin_sgn, decay, zeta, xi, cdec = tables
    tok = lambda b, n: b * nchunk + n
    return pl.pallas_call(
        _ret_kernel,
        grid=(bsz, nchunk),
        in_specs=[
            pl.BlockSpec((c, RET_QK_WIDTH), lambda b, n: (tok(b, n), 0)),
            pl.BlockSpec((c, RET_QK_WIDTH), lambda b, n: (tok(b, n), 1)),
            pl.BlockSpec((c, RET_V_WIDTH), lambda b, n: (tok(b, n), 1)),
            pl.BlockSpec((c, RET_V_WIDTH), lambda b, n: (tok(b, n), 2)),
            pl.BlockSpec((c, RET_QK_WIDTH), lambda b, n: (n, 0)),
            pl.BlockSpec((c, RET_QK_WIDTH), lambda b, n: (n, 0)),
            pl.BlockSpec((RET_HEADS, c, c), lambda b, n: (0, 0, 0)),
            pl.BlockSpec((c, RET_QK_WIDTH), lambda b, n: (0, 0)),
            pl.BlockSpec((c, RET_QK_WIDTH), lambda b, n: (0, 0)),
            pl.BlockSpec((1, RET_V_WIDTH), lambda b, n: (0, 0)),
            pl.BlockSpec((1, RET_V_WIDTH), lambda b, n: (0, 0)),
        ],
        out_specs=pl.BlockSpec((c, RET_V_WIDTH), lambda b, n: (tok(b, n), 0)),
        out_shape=jax.ShapeDtypeStruct((bsz * seq, RET_V_WIDTH), F32),
        scratch_shapes=[pltpu.VMEM((RET_HEADS, RET_QK_DIM, RET_V_DIM), F32)],
        compiler_params=_cparams(2),
        name="retention",
    )(proj, proj, proj, proj, cos_full, sin_sgn, decay, zeta, xi, cdec, gn_g)


def _moba_tables(seq):
    inv = ROPE_THETA ** (-np.arange(ROPE_HALF) / ROPE_HALF)
    ang = inv[:, None] * np.arange(seq)[None, :]
    return jnp.asarray(np.cos(ang), dtype=F32), jnp.asarray(np.sin(ang), dtype=F32)


def _moba_prep_kernel(att_ref, qg_ref, kg_ref, cos_ref, sin_ref,
                      qt_ref, kn_ref, vt_ref, sel_ref, kmean_ref):
    i = pl.program_id(1)
    nb = kmean_ref.shape[1]

    @pl.when(i == 0)
    def _():
        kmean_ref[...] = jnp.zeros(kmean_ref.shape, F32)

    cos = cos_ref[...]
    sin = sin_ref[...]

    def norm_rot(x, g):
        ms = jnp.mean(x * x, axis=0, keepdims=True)
        xn = x * lax.rsqrt(ms + NORM_EPS) * g
        x1 = xn[0:ROPE_HALF]
        x2 = xn[ROPE_HALF:2 * ROPE_HALF]
        return jnp.concatenate([x1 * cos - x2 * sin, x1 * sin + x2 * cos, xn[2 * ROPE_HALF:]], axis=0)

    row = lax.broadcasted_iota(jnp.int32, (nb, MOBA_BLOCK), 0)
    past = row < i
    for h in range(ATT_HEADS):
        hs = slice(h * ATT_HEAD_DIM, (h + 1) * ATT_HEAD_DIM)
        q = norm_rot(att_ref[hs, :], qg_ref[...])
        k = norm_rot(att_ref[ATT_WIDTH + h * ATT_HEAD_DIM:ATT_WIDTH + (h + 1) * ATT_HEAD_DIM, :], kg_ref[...])
        kn = k.T
        kn_ref[0, h] = kn.astype(BF16)
        qt_ref[0, hs, :] = (q * ATT_Q_SCALE).astype(BF16)
        v0 = 2 * ATT_WIDTH + h * ATT_HEAD_DIM
        vt_ref[0, h * ATT_V_ROWS:h * ATT_V_ROWS + ATT_HEAD_DIM, :] = att_ref[v0:v0 + ATT_HEAD_DIM, :].astype(BF16)
        vt_ref[0, h * ATT_V_ROWS + ATT_HEAD_DIM:(h + 1) * ATT_V_ROWS, :] = jnp.ones(
            (ATT_V_ROWS - ATT_HEAD_DIM, MOBA_BLOCK), BF16)
        kmean_ref[h, pl.ds(i, 1), :] = jnp.mean(kn, axis=0, keepdims=True)
        gate = jnp.dot(kmean_ref[h], q, preferred_element_type=F32, precision=lax.Precision.HIGHEST)
        gate = jnp.where(past, gate, NEG_INF)
        beaten = jnp.zeros(gate.shape, F32)
        for j in range(nb):
            gj = jnp.broadcast_to(gate[j:j + 1, :], gate.shape)
            ahead = jnp.where(gj > gate, 1.0, jnp.where(gj == gate, jnp.where(row > j, 1.0, 0.0), 0.0))
            beaten = beaten + ahead
        sel_ref[0, h] = jnp.where(past, jnp.where(beaten < MOBA_TOPK, 1.0, 0.0), 0.0)


def _moba_prep(att_t, qn_g, kn_g, tables, bsz, seq):
    nb = seq // MOBA_BLOCK
    cos_t, sin_t = tables
    return pl.pallas_call(
        _moba_prep_kernel,
        grid=(bsz, nb),
        in_specs=[
            pl.BlockSpec((N_ATT, MOBA_BLOCK), lambda b, i: (0, b * nb + i)),
            pl.BlockSpec((ATT_HEAD_DIM, 1), lambda b, i: (0, 0)),
            pl.BlockSpec((ATT_HEAD_DIM, 1), lambda b, i: (0, 0)),
            pl.BlockSpec((ROPE_HALF, MOBA_BLOCK), lambda b, i: (0, i)),
            pl.BlockSpec((ROPE_HALF, MOBA_BLOCK), lambda b, i: (0, i)),
        ],
        out_specs=(
            pl.BlockSpec((1, ATT_WIDTH, MOBA_BLOCK), lambda b, i: (b, 0, i)),
            pl.BlockSpec((1, ATT_HEADS, MOBA_BLOCK, ATT_HEAD_DIM), lambda b, i: (b, 0, i, 0)),
            pl.BlockSpec((1, ATT_HEADS * ATT_V_ROWS, MOBA_BLOCK), lambda b, i: (b, 0, i)),
            pl.BlockSpec((1, ATT_HEADS, nb, MOBA_BLOCK), lambda b, i: (b, 0, 0, i)),
        ),
        out_shape=(
            jax.ShapeDtypeStruct((bsz, ATT_WIDTH, seq), BF16),
            jax.ShapeDtypeStruct((bsz, ATT_HEADS, seq, ATT_HEAD_DIM), BF16),
            jax.ShapeDtypeStruct((bsz, ATT_HEADS * ATT_V_ROWS, seq), BF16),
            jax.ShapeDtypeStruct((bsz, ATT_HEADS, nb, seq), F32),
        ),
        scratch_shapes=[pltpu.VMEM((ATT_HEADS, nb, ATT_HEAD_DIM), F32)],
        compiler_params=_cparams(2),
        name="moba_prep",
    )(att_t, qn_g.reshape(ATT_HEAD_DIM, 1), kn_g.reshape(ATT_HEAD_DIM, 1), cos_t, sin_t)


def _moba_attn_kernel(qt_ref, k_ref, vt_ref, sel_ref, o_ref, m_ref, acc_ref, s_ref, pv_ref, mb_ref):
    blk = MOBA_BLOCK
    hd = ATT_HEAD_DIM
    nb = sel_ref.shape[2]
    heads = k_ref.shape[1]
    vrows = ATT_V_ROWS
    kpos = lax.broadcasted_iota(jnp.int32, (blk, blk), 0)
    qpos = lax.broadcasted_iota(jnp.int32, (blk, blk), 1)
    causal = kpos <= qpos

    def q_block(i, carry):
        q0 = pl.multiple_of(i * blk, blk)

        def scores(h, j0):
            return jnp.dot(k_ref[0, h, pl.ds(j0, blk), :], qt_ref[0, h * hd:(h + 1) * hd, pl.ds(q0, blk)],
                           preferred_element_type=F32)

        def values(h, j0):
            return vt_ref[0, h * vrows:(h + 1) * vrows, pl.ds(j0, blk)]

        def local_softmax(s, m_blk, picked):
            shift = m_blk if picked is None else jnp.where(picked, m_blk, -NEG_INF)
            p = jnp.exp2((s - shift).astype(BF16))
            return p, (m_blk if picked is None else jnp.where(picked, m_blk, NEG_INF))

        own = [scores(h, q0) for h in range(heads)]
        for h in range(heads):
            s_ref[0, h] = scores(h, 0)
            pv_ref[1, h] = jnp.zeros((vrows, blk), F32)
            mb_ref[1, h] = jnp.full((1, blk), NEG_INF, F32)
        for h in range(heads):
            s = jnp.where(causal, own[h], NEG_INF)
            p, m = local_softmax(s, jnp.max(s, axis=0, keepdims=True), None)
            m_ref[h] = m
            acc_ref[h] = jnp.dot(values(h, q0), p, preferred_element_type=F32)

        def fold(slot):
            for h in range(heads):
                m_blk = mb_ref[slot, h]
                m_old = m_ref[h]
                m_new = jnp.maximum(m_old, m_blk)
                m_ref[h] = m_new
                acc_ref[h] = jnp.exp2(m_old - m_new) * acc_ref[h] + jnp.exp2(m_blk - m_new) * pv_ref[slot, h]

        def step(j, cur, nxt, live):
            jn0 = pl.multiple_of(jnp.minimum(j + 1, nb - 1) * blk, blk)
            for h in range(heads):
                s_ref[nxt, h] = scores(h, jn0)
            jc = jnp.minimum(j, nb - 1)
            jc0 = pl.multiple_of(jc * blk, blk)
            for h in range(heads):
                picked = jnp.logical_and(sel_ref[0, h, pl.ds(jc, 1), pl.ds(q0, blk)] > 0.5, live)
                s = s_ref[cur, h]
                p, m = local_softmax(s, jnp.max(s, axis=0, keepdims=True), picked)
                pv_ref[cur, h] = jnp.dot(values(h, jc0), p, preferred_element_type=F32)
                mb_ref[cur, h] = m
            fold(nxt)

        def block_pair(t, carry):
            step(2 * t, 0, 1, True)
            step(2 * t + 1, 1, 0, 2 * t + 1 < i)
            return carry

        lax.fori_loop(0, (i + 1) // 2, block_pair, 0)
        fold(1)
        for h in range(heads):
            acc = acc_ref[h]
            o_ref[0, h * hd:(h + 1) * hd, pl.ds(q0, blk)] = acc[:hd] / acc[hd:hd + 1]
        return carry

    lax.fori_loop(0, nb, q_block, 0)


def _moba_attn(qt, kn, vt, sel):
    bsz, _, seq = qt.shape
    nb = seq // MOBA_BLOCK
    hg = ATT_HEADS_PER_STEP
    return pl.pallas_call(
        _moba_attn_kernel,
        grid=(bsz, ATT_HEADS // hg),
        in_specs=[
            pl.BlockSpec((1, hg * ATT_HEAD_DIM, seq), lambda b, h: (b, h, 0)),
            pl.BlockSpec((1, hg, seq, ATT_HEAD_DIM), lambda b, h: (b, h, 0, 0)),
            pl.BlockSpec((1, hg * ATT_V_ROWS, seq), lambda b, h: (b, h, 0)),
            pl.BlockSpec((1, hg, nb, seq), lambda b, h: (b, h, 0, 0)),
        ],
        out_specs=pl.BlockSpec((1, hg * ATT_HEAD_DIM, seq), lambda b, h: (b, h, 0)),
        out_shape=jax.ShapeDtypeStruct((bsz, ATT_WIDTH, seq), F32),
        scratch_shapes=[pltpu.VMEM((hg, 1, MOBA_BLOCK), F32),
                        pltpu.VMEM((hg, ATT_V_ROWS, MOBA_BLOCK), F32),
                        pltpu.VMEM((2, hg, MOBA_BLOCK, MOBA_BLOCK), F32),
                        pltpu.VMEM((2, hg, ATT_V_ROWS, MOBA_BLOCK), F32),
                        pltpu.VMEM((2, hg, 1, MOBA_BLOCK), F32)],
        compiler_params=_cparams(2),
        name="moba_attn",
    )(qt, kn, vt, sel)


def _merge_kernel(x_ref, g_ref, ys_ref, yr_ref, at_ref, wg_ref, ws_ref, wr_ref, wa_ref, wo_ref, o_ref,
                  ysp_ref):
    x = x_ref[...]
    tm = x.shape[0]
    xn = _rms_rows(x, g_ref[...]).astype(BF16)
    for chunk in range(tm // LS_S5):
        for s in range(SUBLANES):
            for t0 in range(0, S5_TSUB, SUBLANES):
                r = chunk * LS_S5 + s * S5_TSUB + t0
                for c in range(SSM_WIDTH // LANES):
                    ysp_ref[r:r + SUBLANES, c * LANES:(c + 1) * LANES] = ys_ref[
                        c, pl.ds(chunk * LS_S5 + SUBLANES * t0 + s, SUBLANES, stride=SUBLANES), :]

    def gate(k):
        return _sigmoid(jnp.dot(xn, wg_ref[:, k * D_MODEL:(k + 1) * D_MODEL], preferred_element_type=F32))

    merged = gate(0) * jnp.dot(ysp_ref[...].astype(BF16), ws_ref[...], preferred_element_type=F32)
    merged += gate(1) * jnp.dot(yr_ref[...].astype(BF16), wr_ref[...], preferred_element_type=F32)
    merged += gate(2) * lax.dot_general(at_ref[0].astype(BF16), wa_ref[...], (((0,), (0,)), ((), ())),
                                        preferred_element_type=F32)
    o_ref[...] = x + jnp.dot(merged.astype(BF16), wo_ref[...], preferred_element_type=F32)


def _merge(x2d, g, y_s, y_r, o_t, w_gates, w_s, w_r, w_a, w_o, bsz, seq):
    tm = TM_MERGE
    per_seq = seq // tm
    row = lambda i: (i, 0)
    const = lambda i: (0, 0)
    return pl.pallas_call(
        _merge_kernel,
        grid=(bsz * per_seq,),
        in_specs=[
            pl.BlockSpec((tm, D_MODEL), row),
            pl.BlockSpec((1, D_MODEL), const),
            pl.BlockSpec((SSM_WIDTH // LANES, tm, LANES), lambda i: (0, i, 0)),
            pl.BlockSpec((tm, RET_V_WIDTH), row),
            pl.BlockSpec((1, ATT_WIDTH, tm), lambda i: (i // per_seq, 0, i % per_seq)),
            pl.BlockSpec((D_MODEL, N_GATES), const),
            pl.BlockSpec((SSM_WIDTH, D_MODEL), const),
            pl.BlockSpec((RET_V_WIDTH, D_MODEL), const),
            pl.BlockSpec((ATT_WIDTH, D_MODEL), const),
            pl.BlockSpec((D_MODEL, D_MODEL), const),
        ],
        out_specs=pl.BlockSpec((tm, D_MODEL), row),
        out_shape=jax.ShapeDtypeStruct(x2d.shape, F32),
        scratch_shapes=[pltpu.VMEM((tm, SSM_WIDTH), F32)],
        compiler_params=_cparams(1),
        name="merge",
    )(x2d, g, y_s, y_r, o_t, w_gates, w_s, w_r, w_a, w_o)


def _ffn_kernel(x_ref, halo_ref, g_ref, wg_ref, wu_ref, cwg_ref, cwu_ref, cbg_ref, cbu_ref,
                wd_ref, o_ref, xn_ref, acc_ref, *, tiles_per_seq):
    i = pl.program_id(0)
    c = pl.program_id(1)
    tm = x_ref.shape[0]

    @pl.when(c == 0)
    def _():
        g = g_ref[...]
        xn_ref[0:HALO, :] = _rms_rows(halo_ref[...], g).astype(BF16)
        xn_ref[HALO:, :] = _rms_rows(x_ref[...], g).astype(BF16)
        acc_ref[...] = jnp.zeros(acc_ref.shape, F32)

    rowi = lax.broadcasted_iota(jnp.int32, (tm + HALO, 1), 0)
    live = jnp.logical_or(rowi >= HALO, i % tiles_per_seq != 0)

    def conv(w_ref, cw_ref, cb_ref):
        h = jnp.dot(xn_ref[...], w_ref[...], preferred_element_type=F32)
        h = jnp.where(live, h, 0.0)
        cw = cw_ref[...]
        return (cb_ref[...] + h[HALO - 2:tm + HALO - 2] * cw[0:1] + h[HALO - 1:tm + HALO - 1] * cw[1:2]
                + h[HALO:] * cw[2:3])

    hg = conv(wg_ref, cwg_ref, cbg_ref)
    hu = conv(wu_ref, cwu_ref, cbu_ref)
    act = (hg * _sigmoid(hg) * hu).astype(BF16)
    acc_ref[...] += jnp.dot(act, wd_ref[...], preferred_element_type=F32)

    @pl.when(c == pl.num_programs(1) - 1)
    def _():
        o_ref[...] = x_ref[...] + acc_ref[...]


def _ffn(x2d, g, w_up, conv_w, conv_b, w_down, seq):
    t = x2d.shape[0]
    tm, fc = TM_FFN, FC_FFN
    nfc = FFN_HIDDEN // fc
    halo_blocks = tm // HALO
    wmode = dict(pipeline_mode=pl.Buffered(1)) if nfc == 1 else {}
    return pl.pallas_call(
        functools.partial(_ffn_kernel, tiles_per_seq=seq // tm),
        grid=(t // tm, nfc),
        in_specs=[
            pl.BlockSpec((tm, D_MODEL), lambda i, c: (i, 0)),
            pl.BlockSpec((HALO, D_MODEL), lambda i, c: (jnp.maximum(i * halo_blocks - 1, 0), 0)),
            pl.BlockSpec((1, D_MODEL), lambda i, c: (0, 0)),
            pl.BlockSpec((D_MODEL, fc), lambda i, c: (0, c), **wmode),
            pl.BlockSpec((D_MODEL, fc), lambda i, c: (0, c + nfc), **wmode),
            pl.BlockSpec((CONV_WIDTH, fc), lambda i, c: (0, c)),
            pl.BlockSpec((CONV_WIDTH, fc), lambda i, c: (0, c + nfc)),
            pl.BlockSpec((1, fc), lambda i, c: (0, c)),
            pl.BlockSpec((1, fc), lambda i, c: (0, c + nfc)),
            pl.BlockSpec((fc, D_MODEL), lambda i, c: (c, 0), **wmode),
        ],
        out_specs=pl.BlockSpec((tm, D_MODEL), lambda i, c: (i, 0)),
        out_shape=jax.ShapeDtypeStruct(x2d.shape, F32),
        scratch_shapes=[pltpu.VMEM((tm + HALO, D_MODEL), BF16), pltpu.VMEM((tm, D_MODEL), F32)],
        compiler_params=_cparams(2),
        name="ffn",
    )(x2d, x2d, g, w_up, w_up, conv_w, conv_w, conv_b, conv_b, w_down)


def kernel(x, norm1_g, w_in, ssm_lambda_re, ssm_lambda_im, ssm_log_dt, ssm_b_re, ssm_b_im, ssm_c_re, ssm_c_im, ssm_d, ssm_w_glu, ret_gn_g, attn_qn_g, attn_kn_g, w_br_ssm, w_br_ret, w_br_att, w_o, norm2_g, ffn_w_up, ffn_conv_w, ffn_conv_b, ffn_w_down):
    bsz, seq, _ = x.shape
    depth = w_in.shape[0]
    assert seq % MOBA_BLOCK == 0 and seq % TM_PROJ == 0 and seq % TM_FFN == 0
    ret_tables = _ret_tables(seq)
    moba_tables = _moba_tables(seq)
    gpb = SSM_GROUPS // S5_BLOCKS
    eye = jnp.eye(gpb, dtype=F32)

    def embed_c(c):
        c4 = c.reshape(S5_BLOCKS, gpb, SSM_GROUP, SSM_STATE)
        return jnp.einsum('bghp,gk->bgpkh', c4, eye).reshape(S5_BLOCKS, S5_BLK_STATES, S5_BLK_CH)

    x2d = x.reshape(bsz * seq, D_MODEL)
    for l in range(depth):
        w = w_in[l]
        w_main = w[:, :N_MAIN].astype(BF16)
        w_att_t = w[:, N_MAIN:N_MAIN + N_ATT].T.astype(BF16)
        w_gates = w[:, N_MAIN + N_ATT:].astype(BF16)
        norm1 = norm1_g[l].reshape(1, D_MODEL)
        cmat = jnp.concatenate([embed_c(ssm_c_re[l]), -embed_c(ssm_c_im[l])], axis=1).astype(BF16)

        bmat, coef_a, coef_s, ptab = _s5_prep(ssm_lambda_re[l], ssm_lambda_im[l], ssm_log_dt[l],
                                              ssm_b_re[l], ssm_b_im[l])
        u3, proj, att_t = _in_proj(x2d, norm1, w_main, w_att_t)
        y_s = _s5(u3, bmat, coef_a, coef_s, ptab, cmat, ssm_d[l].reshape(1, SSM_WIDTH),
                  ssm_w_glu[l].astype(BF16), bsz, seq)
        y_r = _retention(proj, ret_gn_g[l].reshape(1, RET_V_WIDTH), ret_tables, bsz, seq)
        qt, kn, vt, sel = _moba_prep(att_t, attn_qn_g[l], attn_kn_g[l], moba_tables, bsz, seq)
        o_t = _moba_attn(qt, kn, vt, sel)
        x2d = _merge(x2d, norm1, y_s, y_r, o_t, w_gates, w_br_ssm[l].astype(BF16), w_br_ret[l].astype(BF16),
                     w_br_att[l].astype(BF16), w_o[l].astype(BF16), bsz, seq)
        x2d = _ffn(x2d, norm2_g[l].reshape(1, D_MODEL), ffn_w_up[l].astype(BF16), ffn_conv_w[l],
                   ffn_conv_b[l].reshape(1, 2 * FFN_HIDDEN), ffn_w_down[l].astype(BF16), seq)
    return x2d.reshape(bsz, seq, D_MODEL)
```

```python
import functools
import math

import numpy as np
import jax
import jax.numpy as jnp
from jax import lax
from jax.experimental import pallas as pl
from jax.experimental.pallas import tpu as pltpu

F32 = jnp.float32
BF16 = jnp.bfloat16

D_MODEL = 1024
SSM_WIDTH = 512
SSM_GROUP = 16
SSM_GROUPS = 32
SSM_STATE = 64
SSM_NSTATE = SSM_GROUPS * SSM_STATE
RET_HEADS = 4
RET_QK_DIM = 64
RET_V_DIM = 128
RET_QK_WIDTH = 256
RET_V_WIDTH = 512
RET_THETA = 10000.0
ATT_HEADS = 8
ATT_HEAD_DIM = 64
ATT_WIDTH = 512
MOBA_BLOCK = 256
MOBA_TOPK = 3
ROPE_THETA = 500000.0
ROPE_HALF = 8
FFN_HIDDEN = 2816
CONV_WIDTH = 3
NORM_EPS = 1e-6
NEG_INF = -1e30

N_MAIN = SSM_WIDTH + 2 * RET_QK_WIDTH + 2 * RET_V_WIDTH
N_ATT = 3 * ATT_WIDTH
N_GATES = 3 * D_MODEL

SUBLANES = 8
LANES = 128
VMEM_LIMIT = 52 * 1024 * 1024

TM_PROJ = 512
LS_S5 = 256
S5_TSUB = LS_S5 // SUBLANES
S5_BLOCKS = 2
S5_BLK_CH = SSM_WIDTH // S5_BLOCKS
S5_BLK_STATES = SSM_NSTATE // S5_BLOCKS
S5_SCAN_COLS = 4
RET_CHUNK = 256
TM_MERGE = 512
TM_FFN = 512
FC_FFN = 2816
HALO = SUBLANES
ATT_HEADS_PER_STEP = 4
ATT_V_ROWS = ATT_HEAD_DIM + 16
ATT_Q_SCALE = math.log2(math.e) * ATT_HEAD_DIM ** -0.5


def _cparams(n_axes):
    return pltpu.CompilerParams(dimension_semantics=("arbitrary",) * n_axes,
                                vmem_limit_bytes=VMEM_LIMIT)


def _sigmoid(x):
    return 1.0 / (1.0 + jnp.exp(-x))


def _rms_rows(x, g):
    ms = jnp.mean(x * x, axis=-1, keepdims=True)
    return x * lax.rsqrt(ms + NORM_EPS) * g


def _s5_prep_kernel(lre_ref, lim_ref, ldt_ref, bre_ref, bim_ref, bmat_ref, ca_ref, cs_ref, pt_ref):
    lre = lre_ref[...]
    lim = lim_ref[...]
    dt = jnp.exp(ldt_ref[...])

    def lam_pow(k):
        mag = jnp.exp(k * lre * dt)
        ang = k * lim * dt
        return mag * jnp.cos(ang), mag * jnp.sin(ang)

    ar, ai = lam_pow(1.0)
    x = ar - 1.0
    den = lre * lre + lim * lim
    f_re = (x * lre + ai * lim) / den
    f_im = (ai * lre - x * lim) / den
    for bk in range(S5_BLOCKS):
        sl = slice(bk * S5_BLK_STATES, (bk + 1) * S5_BLK_STATES)
        bre = bre_ref[bk]
        bim = bim_ref[bk]
        bmat_ref[bk, :, :S5_BLK_STATES] = (f_re[:, sl] * bre - f_im[:, sl] * bim).astype(BF16)
        bmat_ref[bk, :, S5_BLK_STATES:] = (f_re[:, sl] * bim + f_im[:, sl] * bre).astype(BF16)

    tile = (SUBLANES, SSM_NSTATE)
    ca_ref[0:8, :] = jnp.broadcast_to(ar, tile)
    ca_ref[8:16, :] = jnp.broadcast_to(ai, tile)
    row = lax.broadcasted_iota(jnp.int32, tile, 0)
    for idx, k in enumerate((1, 2, 4)):
        pr, pi = lam_pow(float(k * S5_TSUB))
        keep = row >= k
        cs_ref[idx * 16:idx * 16 + 8, :] = jnp.where(keep, jnp.broadcast_to(pr, tile), 0.0)
        cs_ref[idx * 16 + 8:idx * 16 + 16, :] = jnp.where(keep, jnp.broadcast_to(pi, tile), 0.0)
    pr, pi = lam_pow(float(S5_TSUB))
    cs_ref[48:56, :] = jnp.broadcast_to(pr, tile)
    cs_ref[56:64, :] = jnp.broadcast_to(pi, tile)
    steps = (lax.broadcasted_iota(jnp.int32, (S5_TSUB, SSM_NSTATE), 0) + 1).astype(F32)
    pr, pi = lam_pow(steps)
    for t in range(S5_TSUB):
        pt_ref[0, t * SUBLANES:(t + 1) * SUBLANES, :] = jnp.broadcast_to(pr[t:t + 1], tile)
        pt_ref[1, t * SUBLANES:(t + 1) * SUBLANES, :] = jnp.broadcast_to(pi[t:t + 1], tile)


def _s5_prep(lam_re, lam_im, log_dt, b_re, b_im):
    gpb = SSM_GROUPS // S5_BLOCKS
    eye = jnp.eye(gpb, dtype=F32)

    def embed(b):
        b4 = b.reshape(S5_BLOCKS, gpb, SSM_STATE, SSM_GROUP)
        return jnp.einsum('bgph,gk->bghkp', b4, eye).reshape(S5_BLOCKS, S5_BLK_CH, S5_BLK_STATES)

    ldt = jnp.repeat(log_dt, SSM_STATE).reshape(1, SSM_NSTATE)
    return pl.pallas_call(
        _s5_prep_kernel,
        out_shape=(jax.ShapeDtypeStruct((S5_BLOCKS, S5_BLK_CH, 2 * S5_BLK_STATES), BF16),
                   jax.ShapeDtypeStruct((16, SSM_NSTATE), F32),
                   jax.ShapeDtypeStruct((64, SSM_NSTATE), F32),
                   jax.ShapeDtypeStruct((2, LS_S5, SSM_NSTATE), F32)),
        compiler_params=pltpu.CompilerParams(vmem_limit_bytes=VMEM_LIMIT),
        name="s5_prep",
    )(lam_re.reshape(1, SSM_NSTATE), lam_im.reshape(1, SSM_NSTATE), ldt, embed(b_re), embed(b_im))


def _in_proj_kernel(x_ref, g_ref, wm_ref, wat_ref, u_ref, proj_ref, att_ref):
    tm = x_ref.shape[0]
    xn = _rms_rows(x_ref[...], g_ref[...]).astype(BF16)
    att_ref[...] = lax.dot_general(wat_ref[...], xn, (((1,), (1,)), ((), ())),
                                   preferred_element_type=F32)
    proj = jnp.dot(xn, wm_ref[...], preferred_element_type=F32)
    proj_ref[...] = proj[:, SSM_WIDTH:]
    for chunk in range(tm // LS_S5):
        for s in range(SUBLANES):
            for t0 in range(0, S5_TSUB, SUBLANES):
                r = chunk * LS_S5 + s * S5_TSUB + t0
                for c in range(SSM_WIDTH // LANES):
                    u_ref[c, pl.ds(chunk * LS_S5 + SUBLANES * t0 + s, SUBLANES, stride=SUBLANES), :] = (
                        proj[r:r + SUBLANES, c * LANES:(c + 1) * LANES])


def _in_proj(x2d, g, w_main, w_att_t):
    t = x2d.shape[0]
    return pl.pallas_call(
        _in_proj_kernel,
        grid=(t // TM_PROJ,),
        in_specs=[
            pl.BlockSpec((TM_PROJ, D_MODEL), lambda i: (i, 0)),
            pl.BlockSpec((1, D_MODEL), lambda i: (0, 0)),
            pl.BlockSpec((D_MODEL, N_MAIN), lambda i: (0, 0)),
            pl.BlockSpec((N_ATT, D_MODEL), lambda i: (0, 0)),
        ],
        out_specs=(
            pl.BlockSpec((SSM_WIDTH // LANES, TM_PROJ, LANES), lambda i: (0, i, 0)),
            pl.BlockSpec((TM_PROJ, N_MAIN - SSM_WIDTH), lambda i: (i, 0)),
            pl.BlockSpec((N_ATT, TM_PROJ), lambda i: (0, i)),
        ),
        out_shape=(jax.ShapeDtypeStruct((SSM_WIDTH // LANES, t, LANES), F32),
                   jax.ShapeDtypeStruct((t, N_MAIN - SSM_WIDTH), F32),
                   jax.ShapeDtypeStruct((N_ATT, t), F32)),
        compiler_params=_cparams(1),
        name="in_proj",
    )(x2d, g, w_main, w_att_t)


def _s5_kernel(u_ref, bmat_ref, ca_ref, cs_ref, pt_ref, cmat_ref, d_ref, wglu_ref, o_ref,
               st_ref, xb_ref, cin_ref):
    ls = st_ref.shape[0]
    tsub = ls // SUBLANES
    nslab = SSM_WIDTH // LANES

    @pl.when(pl.program_id(1) == 0)
    def _():
        cin_ref[...] = jnp.zeros(cin_ref.shape, F32)

    u = jnp.concatenate([u_ref[c] for c in range(nslab)], axis=1)
    ub = u.astype(BF16)
    for bk in range(S5_BLOCKS):
        st_ref[:, bk * 2 * S5_BLK_STATES:(bk + 1) * 2 * S5_BLK_STATES] = jnp.dot(
            ub[:, bk * S5_BLK_CH:(bk + 1) * S5_BLK_CH], bmat_ref[bk], preferred_element_type=F32)

    row = lax.broadcasted_iota(jnp.int32, (SUBLANES, LANES), 0)
    tile = (SUBLANES, LANES)

    def columns(cb):
        per_blk = S5_BLK_STATES // LANES
        base = (cb // per_blk) * 2 * S5_BLK_STATES + (cb % per_blk) * LANES
        return (slice(base, base + LANES), slice(base + S5_BLK_STATES, base + S5_BLK_STATES + LANES),
                slice(cb * LANES, (cb + 1) * LANES))

    def cmul(ar, ai, xr, xi):
        return ar * xr - ai * xi, ar * xi + ai * xr

    def scan_columns(cbs):
        sl = [columns(cb) for cb in cbs]
        a = [(ca_ref[0:8, s], ca_ref[8:16, s]) for (_, _, s) in sl]
        first = []
        for re, im, _ in sl:
            first += [st_ref[0:SUBLANES, re], st_ref[0:SUBLANES, im]]

        def local_step(t, carry):
            r0 = t * SUBLANES
            out = []
            for k, (re, im, _) in enumerate(sl):
                pr, pi = cmul(a[k][0], a[k][1], carry[2 * k], carry[2 * k + 1])
                nr = st_ref[pl.ds(r0, SUBLANES), re] + pr
                ni = st_ref[pl.ds(r0, SUBLANES), im] + pi
                st_ref[pl.ds(r0, SUBLANES), re] = nr
                st_ref[pl.ds(r0, SUBLANES), im] = ni
                out += [nr, ni]
            return tuple(out)

        ends = tuple(first)
        for t in range(1, tsub):
            ends = local_step(t, ends)

        entering = []
        for k, (re, im, s) in enumerate(sl):
            fr, fi = ends[2 * k], ends[2 * k + 1]
            gr = jnp.where(row == 0, cin_ref[:, re], pltpu.roll(fr, 1, 0))
            gi = jnp.where(row == 0, cin_ref[:, im], pltpu.roll(fi, 1, 0))
            for idx, kk in enumerate((1, 2, 4)):
                pr, pi = cmul(cs_ref[idx * 16:idx * 16 + 8, s], cs_ref[idx * 16 + 8:idx * 16 + 16, s],
                              pltpu.roll(gr, kk, 0), pltpu.roll(gi, kk, 0))
                gr, gi = gr + pr, gi + pi
            pr, pi = cmul(cs_ref[48:56, s], cs_ref[56:64, s], gr, gi)
            tr, ti = fr + pr, fi + pi
            cin_ref[:, re] = jnp.broadcast_to(tr[SUBLANES - 1:], tile)
            cin_ref[:, im] = jnp.broadcast_to(ti[SUBLANES - 1:], tile)
            entering += [gr, gi]

        def fix_step(tp):
            r16 = tp * 2 * SUBLANES
            for k, (re, im, s) in enumerate(sl):
                halves_r, halves_i = [], []
                for half in range(2):
                    r0 = r16 + half * SUBLANES
                    pr, pi = cmul(pt_ref[0, pl.ds(r0, SUBLANES), s], pt_ref[1, pl.ds(r0, SUBLANES), s],
                                  entering[2 * k], entering[2 * k + 1])
                    halves_r.append(st_ref[pl.ds(r0, SUBLANES), re] + pr)
                    halves_i.append(st_ref[pl.ds(r0, SUBLANES), im] + pi)
                xb_ref[pl.ds(r16, 2 * SUBLANES), re] = jnp.concatenate(halves_r, axis=0).astype(BF16)
                xb_ref[pl.ds(r16, 2 * SUBLANES), im] = jnp.concatenate(halves_i, axis=0).astype(BF16)

        for tp in range(tsub // 2):
            fix_step(tp)

    groups_per_blk = S5_BLK_STATES // LANES // S5_SCAN_COLS
    y_blocks = []
    for bk in range(S5_BLOCKS):
        for grp in range(bk * groups_per_blk, (bk + 1) * groups_per_blk):
            scan_columns(range(grp * S5_SCAN_COLS, (grp + 1) * S5_SCAN_COLS))
        y_blocks.append(jnp.dot(xb_ref[:, bk * 2 * S5_BLK_STATES:(bk + 1) * 2 * S5_BLK_STATES], cmat_ref[bk],
                                preferred_element_type=F32))
    y = jnp.concatenate(y_blocks, axis=1)
    y = y + d_ref[...] * u
    y = 0.5 * y * (1.0 + jnp.tanh(0.7978845608028654 * (y + 0.044715 * (y * y * y))))
    glu = jnp.dot(y.astype(BF16), wglu_ref[...], preferred_element_type=F32)
    out = y * _sigmoid(glu)
    for c in range(nslab):
        o_ref[c] = out[:, c * LANES:(c + 1) * LANES]


def _s5(u3, bmat, coef_a, coef_s, ptab, cmat, d_skip, w_glu, bsz, seq):
    nchunk = seq // LS_S5
    nslab = SSM_WIDTH // LANES
    const2 = lambda b, c: (0, 0)
    const3 = lambda b, c: (0, 0, 0)
    return pl.pallas_call(
        _s5_kernel,
        grid=(bsz, nchunk),
        in_specs=[
            pl.BlockSpec((nslab, LS_S5, LANES), lambda b, c: (0, b * nchunk + c, 0)),
            pl.BlockSpec((S5_BLOCKS, S5_BLK_CH, 2 * S5_BLK_STATES), const3),
            pl.BlockSpec((16, SSM_NSTATE), const2),
            pl.BlockSpec((64, SSM_NSTATE), const2),
            pl.BlockSpec((2, LS_S5, SSM_NSTATE), const3),
            pl.BlockSpec((S5_BLOCKS, 2 * S5_BLK_STATES, S5_BLK_CH), const3),
            pl.BlockSpec((1, SSM_WIDTH), const2),
            pl.BlockSpec((SSM_WIDTH, SSM_WIDTH), const2),
        ],
        out_specs=pl.BlockSpec((nslab, LS_S5, LANES), lambda b, c: (0, b * nchunk + c, 0)),
        out_shape=jax.ShapeDtypeStruct((nslab, bsz * seq, LANES), F32),
        scratch_shapes=[pltpu.VMEM((LS_S5, 2 * SSM_NSTATE), F32),
                        pltpu.VMEM((LS_S5, 2 * SSM_NSTATE), BF16),
                        pltpu.VMEM((SUBLANES, 2 * SSM_NSTATE), F32)],
        compiler_params=_cparams(2),
        name="s5",
    )(u3, bmat, coef_a, coef_s, ptab, cmat, d_skip, w_glu)


def _ret_tables(seq):
    c = RET_CHUNK
    half = RET_QK_DIM // 2
    inv = 1.0 / (RET_THETA ** np.linspace(0.0, 1.0, half))
    ang = np.arange(seq)[:, None] * inv[None, :]
    cos, sin = np.cos(ang), np.sin(ang)
    cos_full = np.tile(np.concatenate([cos, cos], axis=1), (1, RET_HEADS))
    sin_sgn = np.tile(np.concatenate([-sin, sin], axis=1), (1, RET_HEADS))
    log_gamma = np.log1p(-np.exp2(-5.0 - np.arange(RET_HEADS)))
    pos = np.arange(c)
    rel = pos[:, None] - pos[None, :]
    decay = np.where(rel >= 0, np.exp(log_gamma[:, None, None] * np.maximum(rel, 0)[None]), 0.0)
    zeta = np.exp(log_gamma[None, :] * (c - 1.0 - pos)[:, None])
    xi = np.exp(log_gamma[None, :] * (pos + 1.0)[:, None])
    zeta = np.repeat(zeta, RET_QK_DIM, axis=1)
    xi = np.repeat(xi, RET_QK_DIM, axis=1)
    cdec = np.repeat(np.exp(log_gamma * c), RET_V_DIM)[None, :]
    as32 = lambda a: jnp.asarray(a, dtype=F32)
    return as32(cos_full), as32(sin_sgn), as32(decay), as32(zeta), as32(xi), as32(cdec)


def _ret_kernel(q_ref, k_ref, v_ref, g_ref, cos_ref, sin_ref, decay_ref, zeta_ref, xi_ref,
                cdec_ref, gn_ref, o_ref, state_ref):
    @pl.when(pl.program_id(1) == 0)
    def _():
        state_ref[...] = jnp.zeros(state_ref.shape, F32)

    cos = cos_ref[...]
    sin = sin_ref[...]
    lane = lax.broadcasted_iota(jnp.int32, cos.shape, 1)
    first_half = (lane % RET_QK_DIM) < (RET_QK_DIM // 2)

    def rot(x):
        swapped = jnp.where(first_half, pltpu.roll(x, RET_QK_WIDTH - RET_QK_DIM // 2, 1),
                            pltpu.roll(x, RET_QK_DIM // 2, 1))
        return x * cos + swapped * sin

    q = rot(q_ref[...])
    k = rot(k_ref[...]) * (RET_QK_DIM ** -0.5)
    qx = (q * xi_ref[...]).astype(BF16)
    kz = (k * zeta_ref[...]).astype(BF16)
    qb = q.astype(BF16)
    kb = k.astype(BF16)
    vb = v_ref[...].astype(BF16)
    g = g_ref[...]
    gn = gn_ref[...]
    cdec = cdec_ref[...]
    for h in range(RET_HEADS):
        qs = slice(h * RET_QK_DIM, (h + 1) * RET_QK_DIM)
        vs = slice(h * RET_V_DIM, (h + 1) * RET_V_DIM)
        s = lax.dot_general(qb[:, qs], kb[:, qs], (((1,), (1,)), ((), ())),
                            preferred_element_type=F32) * decay_ref[h]
        state = state_ref[h]
        o = (jnp.dot(s.astype(BF16), vb[:, vs], preferred_element_type=F32)
             + jnp.dot(qx[:, qs], state.astype(BF16), preferred_element_type=F32))
        kv = lax.dot_general(kz[:, qs], vb[:, vs], (((0,), (0,)), ((), ())),
                             preferred_element_type=F32)
        state_ref[h] = cdec[:, vs] * state + kv
        mu = jnp.mean(o, axis=-1, keepdims=True)
        oc = o - mu
        var = jnp.mean(oc * oc, axis=-1, keepdims=True)
        on = oc * lax.rsqrt(var + NORM_EPS) * gn[:, vs]
        gh = g[:, vs]
        o_ref[:, vs] = gh * _sigmoid(gh) * on


def _retention(proj, gn_g, tables, bsz, seq):
    c = RET_CHUNK
    nchunk = seq // c
    cos_full, sin_sgn, decay, zeta, xi, cdec = tables
    tok = lambda b, n: b * nchunk + n
    return pl.pallas_call(
        _ret_kernel,
        grid=(bsz, nchunk),
        in_specs=[
            pl.BlockSpec((c, RET_QK_WIDTH), lambda b, n: (tok(b, n), 0)),
            pl.BlockSpec((c, RET_QK_WIDTH), lambda b, n: (tok(b, n), 1)),
            pl.BlockSpec((c, RET_V_WIDTH), lambda b, n: (tok(b, n), 1)),
            pl.BlockSpec((c, RET_V_WIDTH), lambda b, n: (tok(b, n), 2)),
            pl.BlockSpec((c, RET_QK_WIDTH), lambda b, n: (n, 0)),
            pl.BlockSpec((c, RET_QK_WIDTH), lambda b, n: (n, 0)),
            pl.BlockSpec((RET_HEADS, c, c), lambda b, n: (0, 0, 0)),
            pl.BlockSpec((c, RET_QK_WIDTH), lambda b, n: (0, 0)),
            pl.BlockSpec((c, RET_QK_WIDTH), lambda b, n: (0, 0)),
            pl.BlockSpec((1, RET_V_WIDTH), lambda b, n: (0, 0)),
            pl.BlockSpec((1, RET_V_WIDTH), lambda b, n: (0, 0)),
        ],
        out_specs=pl.BlockSpec((c, RET_V_WIDTH), lambda b, n: (tok(b, n), 0)),
        out_shape=jax.ShapeDtypeStruct((bsz * seq, RET_V_WIDTH), F32),
        scratch_shapes=[pltpu.VMEM((RET_HEADS, RET_QK_DIM, RET_V_DIM), F32)],
        compiler_params=_cparams(2),
        name="retention",
    )(proj, proj, proj, proj, cos_full, sin_sgn, decay, zeta, xi, cdec, gn_g)


def _moba_tables(seq):
    inv = ROPE_THETA ** (-np.arange(ROPE_HALF) / ROPE_HALF)
    ang = inv[:, None] * np.arange(seq)[None, :]
    return jnp.asarray(np.cos(ang), dtype=F32), jnp.asarray(np.sin(ang), dtype=F32)


def _moba_prep_kernel(att_ref, qg_ref, kg_ref, cos_ref, sin_ref,
                      qt_ref, kn_ref, vt_ref, sel_ref, kmean_ref):
    i = pl.program_id(1)
    nb = kmean_ref.shape[1]

    @pl.when(i == 0)
    def _():
        kmean_ref[...] = jnp.zeros(kmean_ref.shape, F32)

    cos = cos_ref[...]
    sin = sin_ref[...]

    def norm_rot(x, g):
        ms = jnp.mean(x * x, axis=0, keepdims=True)
        xn = x * lax.rsqrt(ms + NORM_EPS) * g
        x1 = xn[0:ROPE_HALF]
        x2 = xn[ROPE_HALF:2 * ROPE_HALF]
        return jnp.concatenate([x1 * cos - x2 * sin, x1 * sin + x2 * cos, xn[2 * ROPE_HALF:]], axis=0)

    row = lax.broadcasted_iota(jnp.int32, (nb, MOBA_BLOCK), 0)
    past = row < i
    for h in range(ATT_HEADS):
        hs = slice(h * ATT_HEAD_DIM, (h + 1) * ATT_HEAD_DIM)
        q = norm_rot(att_ref[hs, :], qg_ref[...])
        k = norm_rot(att_ref[ATT_WIDTH + h * ATT_HEAD_DIM:ATT_WIDTH + (h + 1) * ATT_HEAD_DIM, :], kg_ref[...])
        kn = k.T
        kn_ref[0, h] = kn.astype(BF16)
        qt_ref[0, hs, :] = (q * ATT_Q_SCALE).astype(BF16)
        v0 = 2 * ATT_WIDTH + h * ATT_HEAD_DIM
        vt_ref[0, h * ATT_V_ROWS:h * ATT_V_ROWS + ATT_HEAD_DIM, :] = att_ref[v0:v0 + ATT_HEAD_DIM, :].astype(BF16)
        vt_ref[0, h * ATT_V_ROWS + ATT_HEAD_DIM:(h + 1) * ATT_V_ROWS, :] = jnp.ones(
            (ATT_V_ROWS - ATT_HEAD_DIM, MOBA_BLOCK), BF16)
        kmean_ref[h, pl.ds(i, 1), :] = jnp.mean(kn, axis=0, keepdims=True)
        gate = jnp.dot(kmean_ref[h], q, preferred_element_type=F32, precision=lax.Precision.HIGHEST)
        gate = jnp.where(past, gate, NEG_INF)
        beaten = jnp.zeros(gate.shape, F32)
        for j in range(nb):
            gj = jnp.broadcast_to(gate[j:j + 1, :], gate.shape)
            ahead = jnp.where(gj > gate, 1.0, jnp.where(gj == gate, jnp.where(row > j, 1.0, 0.0), 0.0))
            beaten = beaten + ahead
        sel_ref[0, h] = jnp.where(past, jnp.where(beaten < MOBA_TOPK, 1.0, 0.0), 0.0)


def _moba_prep(att_t, qn_g, kn_g, tables, bsz, seq):
    nb = seq // MOBA_BLOCK
    cos_t, sin_t = tables
    return pl.pallas_call(
        _moba_prep_kernel,
        grid=(bsz, nb),
        in_specs=[
            pl.BlockSpec((N_ATT, MOBA_BLOCK), lambda b, i: (0, b * nb + i)),
            pl.BlockSpec((ATT_HEAD_DIM, 1), lambda b, i: (0, 0)),
            pl.BlockSpec((ATT_HEAD_DIM, 1), lambda b, i: (0, 0)),
            pl.BlockSpec((ROPE_HALF, MOBA_BLOCK), lambda b, i: (0, i)),
            pl.BlockSpec((ROPE_HALF, MOBA_BLOCK), lambda b, i: (0, i)),
        ],
        out_specs=(
            pl.BlockSpec((1, ATT_WIDTH, MOBA_BLOCK), lambda b, i: (b, 0, i)),
            pl.BlockSpec((1, ATT_HEADS, MOBA_BLOCK, ATT_HEAD_DIM), lambda b, i: (b, 0, i, 0)),
            pl.BlockSpec((1, ATT_HEADS * ATT_V_ROWS, MOBA_BLOCK), lambda b, i: (b, 0, i)),
            pl.BlockSpec((1, ATT_HEADS, nb, MOBA_BLOCK), lambda b, i: (b, 0, 0, i)),
        ),
        out_shape=(
            jax.ShapeDtypeStruct((bsz, ATT_WIDTH, seq), BF16),
            jax.ShapeDtypeStruct((bsz, ATT_HEADS, seq, ATT_HEAD_DIM), BF16),
            jax.ShapeDtypeStruct((bsz, ATT_HEADS * ATT_V_ROWS, seq), BF16),
            jax.ShapeDtypeStruct((bsz, ATT_HEADS, nb, seq), F32),
        ),
        scratch_shapes=[pltpu.VMEM((ATT_HEADS, nb, ATT_HEAD_DIM), F32)],
        compiler_params=_cparams(2),
        name="moba_prep",
    )(att_t, qn_g.reshape(ATT_HEAD_DIM, 1), kn_g.reshape(ATT_HEAD_DIM, 1), cos_t, sin_t)


def _moba_attn_kernel(qt_ref, k_ref, vt_ref, sel_ref, o_ref, m_ref, acc_ref, s_ref, pv_ref, mb_ref):
    blk = MOBA_BLOCK
    hd = ATT_HEAD_DIM
    nb = sel_ref.shape[2]
    heads = k_ref.shape[1]
    vrows = ATT_V_ROWS
    kpos = lax.broadcasted_iota(jnp.int32, (blk, blk), 0)
    qpos = lax.broadcasted_iota(jnp.int32, (blk, blk), 1)
    causal = kpos <= qpos

    def q_block(i, carry):
        q0 = pl.multiple_of(i * blk, blk)

        def scores(h, j0):
            return jnp.dot(k_ref[0, h, pl.ds(j0, blk), :], qt_ref[0, h * hd:(h + 1) * hd, pl.ds(q0, blk)],
                           preferred_element_type=F32)

        def values(h, j0):
            return vt_ref[0, h * vrows:(h + 1) * vrows, pl.ds(j0, blk)]

        def local_softmax(s, m_blk, picked):
            shift = m_blk if picked is None else jnp.where(picked, m_blk, -NEG_INF)
            p = jnp.exp2((s - shift).astype(BF16))
            return p, (m_blk if picked is None else jnp.where(picked, m_blk, NEG_INF))

        own = [scores(h, q0) for h in range(heads)]
        for h in range(heads):
            s_ref[0, h] = scores(h, 0)
        for h in range(heads):
            s = jnp.where(causal, own[h], NEG_INF)
            p, m = local_softmax(s, jnp.max(s, axis=0, keepdims=True), None)
            mb_ref[1, h] = m
            pv_ref[1, h] = jnp.dot(values(h, q0), p, preferred_element_type=F32)
            m_ref[h] = jnp.full((1, blk), NEG_INF, F32)
            acc_ref[h] = jnp.zeros((vrows, blk), F32)

        def fold(slot):
            for h in range(heads):
                m_blk = mb_ref[slot, h]
                m_old = m_ref[h]
                m_new = jnp.maximum(m_old, m_blk)
                m_ref[h] = m_new
                acc_ref[h] = jnp.exp2(m_old - m_new) * acc_ref[h] + jnp.exp2(m_blk - m_new) * pv_ref[slot, h]

        def step(j, cur, nxt, live):
            jn0 = pl.multiple_of(jnp.minimum(j + 1, nb - 1) * blk, blk)
            for h in range(heads):
                s_ref[nxt, h] = scores(h, jn0)
            jc = jnp.minimum(j, nb - 1)
            jc0 = pl.multiple_of(jc * blk, blk)
            for h in range(heads):
                picked = jnp.logical_and(sel_ref[0, h, pl.ds(jc, 1), pl.ds(q0, blk)] > 0.5, live)
                s = s_ref[cur, h]
                p, m = local_softmax(s, jnp.max(s, axis=0, keepdims=True), picked)
                pv_ref[cur, h] = jnp.dot(values(h, jc0), p, preferred_element_type=F32)
                mb_ref[cur, h] = m
            fold(nxt)

        def block_pair(t, carry):
            step(2 * t, 0, 1, True)
            step(2 * t + 1, 1, 0, 2 * t + 1 < i)
            return carry

        lax.fori_loop(0, (i + 1) // 2, block_pair, 0)
        fold(1)
        for h in range(heads):
            acc = acc_ref[h]
            o_ref[0, h * hd:(h + 1) * hd, pl.ds(q0, blk)] = acc[:hd] / acc[hd:hd + 1]
        return carry

    lax.fori_loop(0, nb, q_block, 0)


def _moba_attn(qt, kn, vt, sel):
    bsz, _, seq = qt.shape
    nb = seq // MOBA_BLOCK
    hg = ATT_HEADS_PER_STEP
    return pl.pallas_call(
        _moba_attn_kernel,
        grid=(bsz, ATT_HEADS // hg),
        in_specs=[
            pl.BlockSpec((1, hg * ATT_HEAD_DIM, seq), lambda b, h: (b, h, 0)),
            pl.BlockSpec((1, hg, seq, ATT_HEAD_DIM), lambda b, h: (b, h, 0, 0)),
            pl.BlockSpec((1, hg * ATT_V_ROWS, seq), lambda b, h: (b, h, 0)),
            pl.BlockSpec((1, hg, nb, seq), lambda b, h: (b, h, 0, 0)),
        ],
        out_specs=pl.BlockSpec((1, hg * ATT_HEAD_DIM, seq), lambda b, h: (b, h, 0)),
        out_shape=jax.ShapeDtypeStruct((bsz, ATT_WIDTH, seq), F32),
        scratch_shapes=[pltpu.VMEM((hg, 1, MOBA_BLOCK), F32),
                        pltpu.VMEM((hg, ATT_V_ROWS, MOBA_BLOCK), F32),
                        pltpu.VMEM((2, hg, MOBA_BLOCK, MOBA_BLOCK), F32),
                        pltpu.VMEM((2, hg, ATT_V_ROWS, MOBA_BLOCK), F32),
                        pltpu.VMEM((2, hg, 1, MOBA_BLOCK), F32)],
        compiler_params=_cparams(2),
        name="moba_attn",
    )(qt, kn, vt, sel)


def _merge_kernel(x_ref, g_ref, ys_ref, yr_ref, at_ref, wg_ref, ws_ref, wr_ref, wa_ref, wo_ref, o_ref,
                  ysp_ref):
    x = x_ref[...]
    tm = x.shape[0]
    xn = _rms_rows(x, g_ref[...]).astype(BF16)
    for chunk in range(tm // LS_S5):
        for s in range(SUBLANES):
            for t0 in range(0, S5_TSUB, SUBLANES):
                r = chunk * LS_S5 + s * S5_TSUB + t0
                for c in range(SSM_WIDTH // LANES):
                    ysp_ref[r:r + SUBLANES, c * LANES:(c + 1) * LANES] = ys_ref[
                        c, pl.ds(chunk * LS_S5 + SUBLANES * t0 + s, SUBLANES, stride=SUBLANES), :]

    def gate(k):
        return _sigmoid(jnp.dot(xn, wg_ref[:, k * D_MODEL:(k + 1) * D_MODEL], preferred_element_type=F32))

    merged = gate(0) * jnp.dot(ysp_ref[...].astype(BF16), ws_ref[...], preferred_element_type=F32)
    merged += gate(1) * jnp.dot(yr_ref[...].astype(BF16), wr_ref[...], preferred_element_type=F32)
    merged += gate(2) * lax.dot_general(at_ref[0].astype(BF16), wa_ref[...], (((0,), (0,)), ((), ())),
                                        preferred_element_type=F32)
    o_ref[...] = x + jnp.dot(merged.astype(BF16), wo_ref[...], preferred_element_type=F32)


def _merge(x2d, g, y_s, y_r, o_t, w_gates, w_s, w_r, w_a, w_o, bsz, seq):
    tm = TM_MERGE
    per_seq = seq // tm
    row = lambda i: (i, 0)
    const = lambda i: (0, 0)
    return pl.pallas_call(
        _merge_kernel,
        grid=(bsz * per_seq,),
        in_specs=[
            pl.BlockSpec((tm, D_MODEL), row),
            pl.BlockSpec((1, D_MODEL), const),
            pl.BlockSpec((SSM_WIDTH // LANES, tm, LANES), lambda i: (0, i, 0)),
            pl.BlockSpec((tm, RET_V_WIDTH), row),
            pl.BlockSpec((1, ATT_WIDTH, tm), lambda i: (i // per_seq, 0, i % per_seq)),
            pl.BlockSpec((D_MODEL, N_GATES), const),
            pl.BlockSpec((SSM_WIDTH, D_MODEL), const),
            pl.BlockSpec((RET_V_WIDTH, D_MODEL), const),
            pl.BlockSpec((ATT_WIDTH, D_MODEL), const),
            pl.BlockSpec((D_MODEL, D_MODEL), const),
        ],
        out_specs=pl.BlockSpec((tm, D_MODEL), row),
        out_shape=jax.ShapeDtypeStruct(x2d.shape, F32),
        scratch_shapes=[pltpu.VMEM((tm, SSM_WIDTH), F32)],
        compiler_params=_cparams(1),
        name="merge",
    )(x2d, g, y_s, y_r, o_t, w_gates, w_s, w_r, w_a, w_o)


def _ffn_kernel(x_ref, halo_ref, g_ref, wg_ref, wu_ref, cwg_ref, cwu_ref, cbg_ref, cbu_ref,
                wd_ref, o_ref, xn_ref, acc_ref, *, tiles_per_seq):
    i = pl.program_id(0)
    c = pl.program_id(1)
    tm = x_ref.shape[0]

    @pl.when(c == 0)
    def _():
        g = g_ref[...]
        xn_ref[0:HALO, :] = _rms_rows(halo_ref[...], g).astype(BF16)
        xn_ref[HALO:, :] = _rms_rows(x_ref[...], g).astype(BF16)
        acc_ref[...] = jnp.zeros(acc_ref.shape, F32)

    inside_seq = i % tiles_per_seq != 0

    def conv(w_ref, cw_ref, cb_ref):
        h = jnp.dot(xn_ref[...], w_ref[...], preferred_element_type=F32)
        h = jnp.concatenate([jnp.where(inside_seq, h[:HALO], 0.0), h[HALO:]], axis=0)
        cw = cw_ref[...]
        return (cb_ref[...] + h[HALO - 2:tm + HALO - 2] * cw[0:1] + h[HALO - 1:tm + HALO - 1] * cw[1:2]
                + h[HALO:] * cw[2:3])

    hg = conv(wg_ref, cwg_ref, cbg_ref)
    hu = conv(wu_ref, cwu_ref, cbu_ref)
    act = (hg * _sigmoid(hg) * hu).astype(BF16)
    acc_ref[...] += jnp.dot(act, wd_ref[...], preferred_element_type=F32)

    @pl.when(c == pl.num_programs(1) - 1)
    def _():
        o_ref[...] = x_ref[...] + acc_ref[...]


def _ffn(x2d, g, w_up, conv_w, conv_b, w_down, seq):
    t = x2d.shape[0]
    tm, fc = TM_FFN, FC_FFN
    nfc = FFN_HIDDEN // fc
    halo_blocks = tm // HALO
    wmode = dict(pipeline_mode=pl.Buffered(1)) if nfc == 1 else {}
    return pl.pallas_call(
        functools.partial(_ffn_kernel, tiles_per_seq=seq // tm),
        grid=(t // tm, nfc),
        in_specs=[
            pl.BlockSpec((tm, D_MODEL), lambda i, c: (i, 0)),
            pl.BlockSpec((HALO, D_MODEL), lambda i, c: (jnp.maximum(i * halo_blocks - 1, 0), 0)),
            pl.BlockSpec((1, D_MODEL), lambda i, c: (0, 0)),
            pl.BlockSpec((D_MODEL, fc), lambda i, c: (0, c), **wmode),
            pl.BlockSpec((D_MODEL, fc), lambda i, c: (0, c + nfc), **wmode),
            pl.BlockSpec((CONV_WIDTH, fc), lambda i, c: (0, c)),
            pl.BlockSpec((CONV_WIDTH, fc), lambda i, c: (0, c + nfc)),
            pl.BlockSpec((1, fc), lambda i, c: (0, c)),
            pl.BlockSpec((1, fc), lambda i, c: (0, c + nfc)),
            pl.BlockSpec((fc, D_MODEL), lambda i, c: (c, 0), **wmode),
        ],
        out_specs=pl.BlockSpec((tm, D_MODEL), lambda i, c: (i, 0)),
        out_shape=jax.ShapeDtypeStruct(x2d.shape, F32),
        scratch_shapes=[pltpu.VMEM((tm + HALO, D_MODEL), BF16), pltpu.VMEM((tm, D_MODEL), F32)],
        compiler_params=_cparams(2),
        name="ffn",
    )(x2d, x2d, g, w_up, w_up, conv_w, conv_w, conv_b, conv_b, w_down)


def kernel(x, norm1_g, w_in, ssm_lambda_re, ssm_lambda_im, ssm_log_dt, ssm_b_re, ssm_b_im, ssm_c_re, ssm_c_im, ssm_d, ssm_w_glu, ret_gn_g, attn_qn_g, attn_kn_g, w_br_ssm, w_br_ret, w_br_att, w_o, norm2_g, ffn_w_up, ffn_conv_w, ffn_conv_b, ffn_w_down):
    bsz, seq, _ = x.shape
    depth = w_in.shape[0]
    assert seq % MOBA_BLOCK == 0 and seq % TM_PROJ == 0 and seq % TM_FFN == 0
    ret_tables = _ret_tables(seq)
    moba_tables = _moba_tables(seq)
    gpb = SSM_GROUPS // S5_BLOCKS
    eye = jnp.eye(gpb, dtype=F32)

    def embed_c(c):
        c4 = c.reshape(S5_BLOCKS, gpb, SSM_GROUP, SSM_STATE)
        return jnp.einsum('bghp,gk->bgpkh', c4, eye).reshape(S5_BLOCKS, S5_BLK_STATES, S5_BLK_CH)

    x2d = x.reshape(bsz * seq, D_MODEL)
    for l in range(depth):
        w = w_in[l]
        w_main = w[:, :N_MAIN].astype(BF16)
        w_att_t = w[:, N_MAIN:N_MAIN + N_ATT].T.astype(BF16)
        w_gates = w[:, N_MAIN + N_ATT:].astype(BF16)
        norm1 = norm1_g[l].reshape(1, D_MODEL)
        cmat = jnp.concatenate([embed_c(ssm_c_re[l]), -embed_c(ssm_c_im[l])], axis=1).astype(BF16)

        bmat, coef_a, coef_s, ptab = _s5_prep(ssm_lambda_re[l], ssm_lambda_im[l], ssm_log_dt[l],
                                              ssm_b_re[l], ssm_b_im[l])
        u3, proj, att_t = _in_proj(x2d, norm1, w_main, w_att_t)
        y_s = _s5(u3, bmat, coef_a, coef_s, ptab, cmat, ssm_d[l].reshape(1, SSM_WIDTH),
                  ssm_w_glu[l].astype(BF16), bsz, seq)
        y_r = _retention(proj, ret_gn_g[l].reshape(1, RET_V_WIDTH), ret_tables, bsz, seq)
        qt, kn, vt, sel = _moba_prep(att_t, attn_qn_g[l], attn_kn_g[l], moba_tables, bsz, seq)
        o_t = _moba_attn(qt, kn, vt, sel)
        x2d = _merge(x2d, norm1, y_s, y_r, o_t, w_gates, w_br_ssm[l].astype(BF16), w_br_ret[l].astype(BF16),
                     w_br_att[l].astype(BF16), w_o[l].astype(BF16), bsz, seq)
        x2d = _ffn(x2d, norm2_g[l].reshape(1, D_MODEL), ffn_w_up[l].astype(BF16), ffn_conv_w[l],
                   ffn_conv_b[l].reshape(1, 2 * FFN_HIDDEN), ffn_w_down[l].astype(BF16), seq)
    return x2d.reshape(bsz, seq, D_MODEL)
```

```python
import functools
import math

import numpy as np
import jax
import jax.numpy as jnp
from jax import lax
from jax.experimental import pallas as pl
from jax.experimental.pallas import tpu as pltpu

F32 = jnp.float32
BF16 = jnp.bfloat16

D_MODEL = 1024
SSM_WIDTH = 512
SSM_GROUP = 16
SSM_GROUPS = 32
SSM_STATE = 64
SSM_NSTATE = SSM_GROUPS * SSM_STATE
RET_HEADS = 4
RET_QK_DIM = 64
RET_V_DIM = 128
RET_QK_WIDTH = 256
RET_V_WIDTH = 512
RET_THETA = 10000.0
ATT_HEADS = 8
ATT_HEAD_DIM = 64
ATT_WIDTH = 512
MOBA_BLOCK = 256
MOBA_TOPK = 3
ROPE_THETA = 500000.0
ROPE_HALF = 8
FFN_HIDDEN = 2816
CONV_WIDTH = 3
NORM_EPS = 1e-6
NEG_INF = -1e30

N_MAIN = SSM_WIDTH + 2 * RET_QK_WIDTH + 2 * RET_V_WIDTH
N_ATT = 3 * ATT_WIDTH
N_GATES = 3 * D_MODEL

SUBLANES = 8
LANES = 128
VMEM_LIMIT = 52 * 1024 * 1024

TM_PROJ = 512
LS_S5 = 256
S5_TSUB = LS_S5 // SUBLANES
S5_BLOCKS = 2
S5_BLK_CH = SSM_WIDTH // S5_BLOCKS
S5_BLK_STATES = SSM_NSTATE // S5_BLOCKS
S5_SCAN_COLS = 4
RET_CHUNK = 256
TM_MERGE = 512
TM_FFN = 512
FC_FFN = 2816
HALO = SUBLANES
ATT_HEADS_PER_STEP = 4
ATT_V_ROWS = ATT_HEAD_DIM + 16
ATT_Q_SCALE = math.log2(math.e) * ATT_HEAD_DIM ** -0.5


def _cparams(n_axes):
    return pltpu.CompilerParams(dimension_semantics=("arbitrary",) * n_axes,
                                vmem_limit_bytes=VMEM_LIMIT)


def _sigmoid(x):
    return 1.0 / (1.0 + jnp.exp(-x))


def _rms_rows(x, g):
    ms = jnp.mean(x * x, axis=-1, keepdims=True)
    return x * lax.rsqrt(ms + NORM_EPS) * g


def _s5_prep_kernel(lre_ref, lim_ref, ldt_ref, bre_ref, bim_ref, bmat_ref, ca_ref, cs_ref, pt_ref):
    lre = lre_ref[...]
    lim = lim_ref[...]
    dt = jnp.exp(ldt_ref[...])

    def lam_pow(k):
        mag = jnp.exp(k * lre * dt)
        ang = k * lim * dt
        return mag * jnp.cos(ang), mag * jnp.sin(ang)

    ar, ai = lam_pow(1.0)
    x = ar - 1.0
    den = lre * lre + lim * lim
    f_re = (x * lre + ai * lim) / den
    f_im = (ai * lre - x * lim) / den
    for bk in range(S5_BLOCKS):
        sl = slice(bk * S5_BLK_STATES, (bk + 1) * S5_BLK_STATES)
        bre = bre_ref[bk]
        bim = bim_ref[bk]
        bmat_ref[bk, :, :S5_BLK_STATES] = (f_re[:, sl] * bre - f_im[:, sl] * bim).astype(BF16)
        bmat_ref[bk, :, S5_BLK_STATES:] = (f_re[:, sl] * bim + f_im[:, sl] * bre).astype(BF16)

    tile = (SUBLANES, SSM_NSTATE)
    ca_ref[0:8, :] = jnp.broadcast_to(ar, tile)
    ca_ref[8:16, :] = jnp.broadcast_to(ai, tile)
    row = lax.broadcasted_iota(jnp.int32, tile, 0)
    for idx, k in enumerate((1, 2, 4)):
        pr, pi = lam_pow(float(k * S5_TSUB))
        keep = row >= k
        cs_ref[idx * 16:idx * 16 + 8, :] = jnp.where(keep, jnp.broadcast_to(pr, tile), 0.0)
        cs_ref[idx * 16 + 8:idx * 16 + 16, :] = jnp.where(keep, jnp.broadcast_to(pi, tile), 0.0)
    pr, pi = lam_pow(float(S5_TSUB))
    cs_ref[48:56, :] = jnp.broadcast_to(pr, tile)
    cs_ref[56:64, :] = jnp.broadcast_to(pi, tile)
    steps = (lax.broadcasted_iota(jnp.int32, (S5_TSUB, SSM_NSTATE), 0) + 1).astype(F32)
    pr, pi = lam_pow(steps)
    for t in range(S5_TSUB):
        pt_ref[0, t * SUBLANES:(t + 1) * SUBLANES, :] = jnp.broadcast_to(pr[t:t + 1], tile)
        pt_ref[1, t * SUBLANES:(t + 1) * SUBLANES, :] = jnp.broadcast_to(pi[t:t + 1], tile)


def _s5_prep(lam_re, lam_im, log_dt, b_re, b_im):
    gpb = SSM_GROUPS // S5_BLOCKS
    eye = jnp.eye(gpb, dtype=F32)

    def embed(b):
        b4 = b.reshape(S5_BLOCKS, gpb, SSM_STATE, SSM_GROUP)
        return jnp.einsum('bgph,gk->bghkp', b4, eye).reshape(S5_BLOCKS, S5_BLK_CH, S5_BLK_STATES)

    ldt = jnp.repeat(log_dt, SSM_STATE).reshape(1, SSM_NSTATE)
    return pl.pallas_call(
        _s5_prep_kernel,
        out_shape=(jax.ShapeDtypeStruct((S5_BLOCKS, S5_BLK_CH, 2 * S5_BLK_STATES), BF16),
                   jax.ShapeDtypeStruct((16, SSM_NSTATE), F32),
                   jax.ShapeDtypeStruct((64, SSM_NSTATE), F32),
                   jax.ShapeDtypeStruct((2, LS_S5, SSM_NSTATE), F32)),
        compiler_params=pltpu.CompilerParams(vmem_limit_bytes=VMEM_LIMIT),
        name="s5_prep",
    )(lam_re.reshape(1, SSM_NSTATE), lam_im.reshape(1, SSM_NSTATE), ldt, embed(b_re), embed(b_im))


def _in_proj_kernel(x_ref, g_ref, wm_ref, wat_ref, qg_ref, kg_ref, cos_ref, sin_ref,
                    u_ref, proj_ref, qt_ref, kn_ref, vt_ref, sel_ref, att_ref, kmean_ref, *, tiles_per_seq):
    tm = x_ref.shape[0]
    tile_in_seq = pl.program_id(0) % tiles_per_seq

    @pl.when(tile_in_seq == 0)
    def _():
        kmean_ref[...] = jnp.zeros(kmean_ref.shape, F32)

    xn = _rms_rows(x_ref[...], g_ref[...]).astype(BF16)
    att_ref[...] = lax.dot_general(wat_ref[...], xn, (((1,), (1,)), ((), ())),
                                   preferred_element_type=F32)
    proj = jnp.dot(xn, wm_ref[...], preferred_element_type=F32)
    proj_ref[...] = proj[:, SSM_WIDTH:]
    blocks_per_tile = tm // MOBA_BLOCK
    for blk in range(blocks_per_tile):
        _moba_prep_block(att_ref, slice(blk * MOBA_BLOCK, (blk + 1) * MOBA_BLOCK),
                         tile_in_seq * blocks_per_tile + blk, qg_ref[...], kg_ref[...], cos_ref, sin_ref,
                         qt_ref, kn_ref, vt_ref, sel_ref, kmean_ref)
    for chunk in range(tm // LS_S5):
        for s in range(SUBLANES):
            for t0 in range(0, S5_TSUB, SUBLANES):
                r = chunk * LS_S5 + s * S5_TSUB + t0
                for c in range(SSM_WIDTH // LANES):
                    u_ref[c, pl.ds(chunk * LS_S5 + SUBLANES * t0 + s, SUBLANES, stride=SUBLANES), :] = (
                        proj[r:r + SUBLANES, c * LANES:(c + 1) * LANES])


def _in_proj(x2d, g, w_main, w_att_t, qn_g, kn_g, tables, bsz, seq):
    t = x2d.shape[0]
    tm = TM_PROJ
    tps = seq // tm
    nb = seq // MOBA_BLOCK
    cos_t, sin_t = tables
    const = lambda i: (0, 0)
    return pl.pallas_call(
        functools.partial(_in_proj_kernel, tiles_per_seq=tps),
        grid=(t // tm,),
        in_specs=[
            pl.BlockSpec((tm, D_MODEL), lambda i: (i, 0)),
            pl.BlockSpec((1, D_MODEL), const),
            pl.BlockSpec((D_MODEL, N_MAIN), const),
            pl.BlockSpec((N_ATT, D_MODEL), const),
            pl.BlockSpec((ATT_HEAD_DIM, 1), const),
            pl.BlockSpec((ATT_HEAD_DIM, 1), const),
            pl.BlockSpec((ROPE_HALF, tm), lambda i: (0, i % tps)),
            pl.BlockSpec((ROPE_HALF, tm), lambda i: (0, i % tps)),
        ],
        out_specs=(
            pl.BlockSpec((SSM_WIDTH // LANES, tm, LANES), lambda i: (0, i, 0)),
            pl.BlockSpec((tm, N_MAIN - SSM_WIDTH), lambda i: (i, 0)),
            pl.BlockSpec((1, ATT_WIDTH, tm), lambda i: (i // tps, 0, i % tps)),
            pl.BlockSpec((1, ATT_HEADS, tm, ATT_HEAD_DIM), lambda i: (i // tps, 0, i % tps, 0)),
            pl.BlockSpec((1, ATT_HEADS * ATT_V_ROWS, tm), lambda i: (i // tps, 0, i % tps)),
            pl.BlockSpec((1, ATT_HEADS, nb, tm), lambda i: (i // tps, 0, 0, i % tps)),
        ),
        out_shape=(
            jax.ShapeDtypeStruct((SSM_WIDTH // LANES, t, LANES), F32),
            jax.ShapeDtypeStruct((t, N_MAIN - SSM_WIDTH), F32),
            jax.ShapeDtypeStruct((bsz, ATT_WIDTH, seq), BF16),
            jax.ShapeDtypeStruct((bsz, ATT_HEADS, seq, ATT_HEAD_DIM), BF16),
            jax.ShapeDtypeStruct((bsz, ATT_HEADS * ATT_V_ROWS, seq), BF16),
            jax.ShapeDtypeStruct((bsz, ATT_HEADS, nb, seq), F32),
        ),
        scratch_shapes=[pltpu.VMEM((N_ATT, tm), F32),
                        pltpu.VMEM((ATT_HEADS, nb, ATT_HEAD_DIM), F32)],
        compiler_params=_cparams(1),
        name="in_proj",
    )(x2d, g, w_main, w_att_t, qn_g.reshape(ATT_HEAD_DIM, 1), kn_g.reshape(ATT_HEAD_DIM, 1), cos_t, sin_t)


def _s5_kernel(u_ref, bmat_ref, ca_ref, cs_ref, pt_ref, cmat_ref, d_ref, wglu_ref, o_ref,
               st_ref, xb_ref, cin_ref):
    ls = st_ref.shape[0]
    tsub = ls // SUBLANES
    nslab = SSM_WIDTH // LANES

    @pl.when(pl.program_id(1) == 0)
    def _():
        cin_ref[...] = jnp.zeros(cin_ref.shape, F32)

    u = jnp.concatenate([u_ref[c] for c in range(nslab)], axis=1)
    ub = u.astype(BF16)
    for bk in range(S5_BLOCKS):
        st_ref[:, bk * 2 * S5_BLK_STATES:(bk + 1) * 2 * S5_BLK_STATES] = jnp.dot(
            ub[:, bk * S5_BLK_CH:(bk + 1) * S5_BLK_CH], bmat_ref[bk], preferred_element_type=F32)

    row = lax.broadcasted_iota(jnp.int32, (SUBLANES, LANES), 0)
    tile = (SUBLANES, LANES)

    def columns(cb):
        per_blk = S5_BLK_STATES // LANES
        base = (cb // per_blk) * 2 * S5_BLK_STATES + (cb % per_blk) * LANES
        return (slice(base, base + LANES), slice(base + S5_BLK_STATES, base + S5_BLK_STATES + LANES),
                slice(cb * LANES, (cb + 1) * LANES))

    def cmul(ar, ai, xr, xi):
        return ar * xr - ai * xi, ar * xi + ai * xr

    def scan_columns(cbs):
        sl = [columns(cb) for cb in cbs]
        a = [(ca_ref[0:8, s], ca_ref[8:16, s]) for (_, _, s) in sl]
        first = []
        for re, im, _ in sl:
            first += [st_ref[0:SUBLANES, re], st_ref[0:SUBLANES, im]]

        def local_step(t, carry):
            r0 = t * SUBLANES
            out = []
            for k, (re, im, _) in enumerate(sl):
                pr, pi = cmul(a[k][0], a[k][1], carry[2 * k], carry[2 * k + 1])
                nr = st_ref[pl.ds(r0, SUBLANES), re] + pr
                ni = st_ref[pl.ds(r0, SUBLANES), im] + pi
                st_ref[pl.ds(r0, SUBLANES), re] = nr
                st_ref[pl.ds(r0, SUBLANES), im] = ni
                out += [nr, ni]
            return tuple(out)

        ends = tuple(first)
        for t in range(1, tsub):
            ends = local_step(t, ends)

        entering = []
        for k, (re, im, s) in enumerate(sl):
            fr, fi = ends[2 * k], ends[2 * k + 1]
            gr = jnp.where(row == 0, cin_ref[:, re], pltpu.roll(fr, 1, 0))
            gi = jnp.where(row == 0, cin_ref[:, im], pltpu.roll(fi, 1, 0))
            for idx, kk in enumerate((1, 2, 4)):
                pr, pi = cmul(cs_ref[idx * 16:idx * 16 + 8, s], cs_ref[idx * 16 + 8:idx * 16 + 16, s],
                              pltpu.roll(gr, kk, 0), pltpu.roll(gi, kk, 0))
                gr, gi = gr + pr, gi + pi
            pr, pi = cmul(cs_ref[48:56, s], cs_ref[56:64, s], gr, gi)
            tr, ti = fr + pr, fi + pi
            cin_ref[:, re] = jnp.broadcast_to(tr[SUBLANES - 1:], tile)
            cin_ref[:, im] = jnp.broadcast_to(ti[SUBLANES - 1:], tile)
            entering += [gr, gi]

        def fix_step(tp):
            r16 = tp * 2 * SUBLANES
            for k, (re, im, s) in enumerate(sl):
                halves_r, halves_i = [], []
                for half in range(2):
                    r0 = r16 + half * SUBLANES
                    pr, pi = cmul(pt_ref[0, pl.ds(r0, SUBLANES), s], pt_ref[1, pl.ds(r0, SUBLANES), s],
                                  entering[2 * k], entering[2 * k + 1])
                    halves_r.append(st_ref[pl.ds(r0, SUBLANES), re] + pr)
                    halves_i.append(st_ref[pl.ds(r0, SUBLANES), im] + pi)
                xb_ref[pl.ds(r16, 2 * SUBLANES), re] = jnp.concatenate(halves_r, axis=0).astype(BF16)
                xb_ref[pl.ds(r16, 2 * SUBLANES), im] = jnp.concatenate(halves_i, axis=0).astype(BF16)

        for tp in range(tsub // 2):
            fix_step(tp)

    groups_per_blk = S5_BLK_STATES // LANES // S5_SCAN_COLS
    y_blocks = []
    for bk in range(S5_BLOCKS):
        for grp in range(bk * groups_per_blk, (bk + 1) * groups_per_blk):
            scan_columns(range(grp * S5_SCAN_COLS, (grp + 1) * S5_SCAN_COLS))
        y_blocks.append(jnp.dot(xb_ref[:, bk * 2 * S5_BLK_STATES:(bk + 1) * 2 * S5_BLK_STATES], cmat_ref[bk],
                                preferred_element_type=F32))
    y = jnp.concatenate(y_blocks, axis=1)
    y = y + d_ref[...] * u
    y = 0.5 * y * (1.0 + jnp.tanh(0.7978845608028654 * (y + 0.044715 * (y * y * y))))
    glu = jnp.dot(y.astype(BF16), wglu_ref[...], preferred_element_type=F32)
    out = y * _sigmoid(glu)
    for c in range(nslab):
        o_ref[c] = out[:, c * LANES:(c + 1) * LANES]


def _s5(u3, bmat, coef_a, coef_s, ptab, cmat, d_skip, w_glu, bsz, seq):
    nchunk = seq // LS_S5
    nslab = SSM_WIDTH // LANES
    const2 = lambda b, c: (0, 0)
    const3 = lambda b, c: (0, 0, 0)
    return pl.pallas_call(
        _s5_kernel,
        grid=(bsz, nchunk),
        in_specs=[
            pl.BlockSpec((nslab, LS_S5, LANES), lambda b, c: (0, b * nchunk + c, 0)),
            pl.BlockSpec((S5_BLOCKS, S5_BLK_CH, 2 * S5_BLK_STATES), const3),
            pl.BlockSpec((16, SSM_NSTATE), const2),
            pl.BlockSpec((64, SSM_NSTATE), const2),
            pl.BlockSpec((2, LS_S5, SSM_NSTATE), const3),
            pl.BlockSpec((S5_BLOCKS, 2 * S5_BLK_STATES, S5_BLK_CH), const3),
            pl.BlockSpec((1, SSM_WIDTH), const2),
            pl.BlockSpec((SSM_WIDTH, SSM_WIDTH), const2),
        ],
        out_specs=pl.BlockSpec((nslab, LS_S5, LANES), lambda b, c: (0, b * nchunk + c, 0)),
        out_shape=jax.ShapeDtypeStruct((nslab, bsz * seq, LANES), F32),
        scratch_shapes=[pltpu.VMEM((LS_S5, 2 * SSM_NSTATE), F32),
                        pltpu.VMEM((LS_S5, 2 * SSM_NSTATE), BF16),
                        pltpu.VMEM((SUBLANES, 2 * SSM_NSTATE), F32)],
        compiler_params=_cparams(2),
        name="s5",
    )(u3, bmat, coef_a, coef_s, ptab, cmat, d_skip, w_glu)


def _ret_tables(seq):
    c = RET_CHUNK
    half = RET_QK_DIM // 2
    inv = 1.0 / (RET_THETA ** np.linspace(0.0, 1.0, half))
    ang = np.arange(seq)[:, None] * inv[None, :]
    cos, sin = np.cos(ang), np.sin(ang)
    cos_full = np.tile(np.concatenate([cos, cos], axis=1), (1, RET_HEADS))
    sin_sgn = np.tile(np.concatenate([-sin, sin], axis=1), (1, RET_HEADS))
    log_gamma = np.log1p(-np.exp2(-5.0 - np.arange(RET_HEADS)))
    pos = np.arange(c)
    rel = pos[:, None] - pos[None, :]
    decay = np.where(rel >= 0, np.exp(log_gamma[:, None, None] * np.maximum(rel, 0)[None]), 0.0)
    zeta = np.exp(log_gamma[None, :] * (c - 1.0 - pos)[:, None])
    xi = np.exp(log_gamma[None, :] * (pos + 1.0)[:, None])
    zeta = np.repeat(zeta, RET_QK_DIM, axis=1)
    xi = np.repeat(xi, RET_QK_DIM, axis=1)
    cdec = np.repeat(np.exp(log_gamma * c), RET_V_DIM)[None, :]
    as32 = lambda a: jnp.asarray(a, dtype=F32)
    return as32(cos_full), as32(sin_sgn), as32(decay), as32(zeta), as32(xi), as32(cdec)


def _ret_kernel(q_ref, k_ref, v_ref, g_ref, cos_ref, sin_ref, decay_ref, zeta_ref, xi_ref,
                cdec_ref, gn_ref, o_ref, state_ref):
    @pl.when(pl.program_id(1) == 0)
    def _():
        state_ref[...] = jnp.zeros(state_ref.shape, F32)

    cos = cos_ref[...]
    sin = sin_ref[...]
    lane = lax.broadcasted_iota(jnp.int32, cos.shape, 1)
    first_half = (lane % RET_QK_DIM) < (RET_QK_DIM // 2)

    def rot(x):
        swapped = jnp.where(first_half, pltpu.roll(x, RET_QK_WIDTH - RET_QK_DIM // 2, 1),
                            pltpu.roll(x, RET_QK_DIM // 2, 1))
        return x * cos + swapped * sin

    q = rot(q_ref[...])
    k = rot(k_ref[...]) * (RET_QK_DIM ** -0.5)
    qx = (q * xi_ref[...]).astype(BF16)
    kz = (k * zeta_ref[...]).astype(BF16)
    qb = q.astype(BF16)
    kb = k.astype(BF16)
    vb = v_ref[...].astype(BF16)
    g = g_ref[...]
    gn = gn_ref[...]
    cdec = cdec_ref[...]
    for h in range(RET_HEADS):
        qs = slice(h * RET_QK_DIM, (h + 1) * RET_QK_DIM)
        vs = slice(h * RET_V_DIM, (h + 1) * RET_V_DIM)
        s = lax.dot_general(qb[:, qs], kb[:, qs], (((1,), (1,)), ((), ())),
                            preferred_element_type=F32) * decay_ref[h]
        state = state_ref[h]
        o = (jnp.dot(s.astype(BF16), vb[:, vs], preferred_element_type=F32)
             + jnp.dot(qx[:, qs], state.astype(BF16), preferred_element_type=F32))
        kv = lax.dot_general(kz[:, qs], vb[:, vs], (((0,), (0,)), ((), ())),
                             preferred_element_type=F32)
        state_ref[h] = cdec[:, vs] * state + kv
        mu = jnp.mean(o, axis=-1, keepdims=True)
        oc = o - mu
        var = jnp.mean(oc * oc, axis=-1, keepdims=True)
        on = oc * lax.rsqrt(var + NORM_EPS) * gn[:, vs]
        gh = g[:, vs]
        o_ref[:, vs] = gh * _sigmoid(gh) * on


def _retention(proj, gn_g, tables, bsz, seq):
    c = RET_CHUNK
    nchunk = seq // c
    cos_full, sin_sgn, decay, zeta, xi, cdec = tables
    tok = lambda b, n: b * nchunk + n
    return pl.pallas_call(
        _ret_kernel,
        grid=(bsz, nchunk),
        in_specs=[
            pl.BlockSpec((c, RET_QK_WIDTH), lambda b, n: (tok(b, n), 0)),
            pl.BlockSpec((c, RET_QK_WIDTH), lambda b, n: (tok(b, n), 1)),
            pl.BlockSpec((c, RET_V_WIDTH), lambda b, n: (tok(b, n), 1)),
            pl.BlockSpec((c, RET_V_WIDTH), lambda b, n: (tok(b, n), 2)),
            pl.BlockSpec((c, RET_QK_WIDTH), lambda b, n: (n, 0)),
            pl.BlockSpec((c, RET_QK_WIDTH), lambda b, n: (n, 0)),
            pl.BlockSpec((RET_HEADS, c, c), lambda b, n: (0, 0, 0)),
            pl.BlockSpec((c, RET_QK_WIDTH), lambda b, n: (0, 0)),
            pl.BlockSpec((c, RET_QK_WIDTH), lambda b, n: (0, 0)),
            pl.BlockSpec((1, RET_V_WIDTH), lambda b, n: (0, 0)),
            pl.BlockSpec((1, RET_V_WIDTH), lambda b, n: (0, 0)),
        ],
        out_specs=pl.BlockSpec((c, RET_V_WIDTH), lambda b, n: (tok(b, n), 0)),
        out_shape=jax.ShapeDtypeStruct((bsz * seq, RET_V_WIDTH), F32),
        scratch_shapes=[pltpu.VMEM((RET_HEADS, RET_QK_DIM, RET_V_DIM), F32)],
        compiler_params=_cparams(2),
        name="retention",
    )(proj, proj, proj, proj, cos_full, sin_sgn, decay, zeta, xi, cdec, gn_g)


def _moba_tables(seq):
    inv = ROPE_THETA ** (-np.arange(ROPE_HALF) / ROPE_HALF)
    ang = inv[:, None] * np.arange(seq)[None, :]
    return jnp.asarray(np.cos(ang), dtype=F32), jnp.asarray(np.sin(ang), dtype=F32)


def _moba_prep_block(att_ref, cols, i_blk, qg, kg, cos_ref, sin_ref, qt_ref, kn_ref, vt_ref, sel_ref, kmean_ref):
    nb = kmean_ref.shape[1]
    cos = cos_ref[:, cols]
    sin = sin_ref[:, cols]

    def norm_rot(x, g):
        ms = jnp.mean(x * x, axis=0, keepdims=True)
        xn = x * lax.rsqrt(ms + NORM_EPS) * g
        x1 = xn[0:ROPE_HALF]
        x2 = xn[ROPE_HALF:2 * ROPE_HALF]
        return jnp.concatenate([x1 * cos - x2 * sin, x1 * sin + x2 * cos, xn[2 * ROPE_HALF:]], axis=0)

    def split(a):
        hi = a.astype(BF16)
        return hi, (a - hi.astype(F32)).astype(BF16)

    row = lax.broadcasted_iota(jnp.int32, (nb, MOBA_BLOCK), 0)
    past = row < i_blk
    for h in range(ATT_HEADS):
        hs = slice(h * ATT_HEAD_DIM, (h + 1) * ATT_HEAD_DIM)
        q = norm_rot(att_ref[hs, cols], qg)
        k = norm_rot(att_ref[ATT_WIDTH + h * ATT_HEAD_DIM:ATT_WIDTH + (h + 1) * ATT_HEAD_DIM, cols], kg)
        kn = k.T
        kn_ref[0, h, cols, :] = kn.astype(BF16)
        qt_ref[0, hs, cols] = (q * ATT_Q_SCALE).astype(BF16)
        v0 = 2 * ATT_WIDTH + h * ATT_HEAD_DIM
        vt_ref[0, h * ATT_V_ROWS:h * ATT_V_ROWS + ATT_HEAD_DIM, cols] = att_ref[v0:v0 + ATT_HEAD_DIM, cols].astype(BF16)
        vt_ref[0, h * ATT_V_ROWS + ATT_HEAD_DIM:(h + 1) * ATT_V_ROWS, cols] = jnp.ones(
            (ATT_V_ROWS - ATT_HEAD_DIM, MOBA_BLOCK), BF16)
        kmean_ref[h, pl.ds(i_blk, 1), :] = jnp.mean(kn, axis=0, keepdims=True)
        mh, ml = split(kmean_ref[h])
        qh, ql = split(q)
        gate = (jnp.dot(mh, qh, preferred_element_type=F32) + jnp.dot(mh, ql, preferred_element_type=F32)
                + jnp.dot(ml, qh, preferred_element_type=F32))
        gate = jnp.where(past, gate, NEG_INF)
        beaten = jnp.zeros(gate.shape, F32)
        for j in range(nb):
            gj = jnp.broadcast_to(gate[j:j + 1, :], gate.shape)
            ahead = jnp.where(gj > gate, 1.0, jnp.where(gj == gate, jnp.where(row > j, 1.0, 0.0), 0.0))
            beaten = beaten + ahead
        sel_ref[0, h, :, cols] = jnp.where(past, jnp.where(beaten < MOBA_TOPK, 1.0, 0.0), 0.0)


def _moba_attn_kernel(qt_ref, k_ref, vt_ref, sel_ref, o_ref, m_ref, acc_ref, s_ref, pv_ref, mb_ref):
    blk = MOBA_BLOCK
    hd = ATT_HEAD_DIM
    nb = sel_ref.shape[2]
    heads = k_ref.shape[1]
    vrows = ATT_V_ROWS
    kpos = lax.broadcasted_iota(jnp.int32, (blk, blk), 0)
    qpos = lax.broadcasted_iota(jnp.int32, (blk, blk), 1)
    causal = kpos <= qpos

    def q_block(i, carry):
        q0 = pl.multiple_of(i * blk, blk)

        def scores(h, j0):
            return jnp.dot(k_ref[0, h, pl.ds(j0, blk), :], qt_ref[0, h * hd:(h + 1) * hd, pl.ds(q0, blk)],
                           preferred_element_type=F32)

        def values(h, j0):
            return vt_ref[0, h * vrows:(h + 1) * vrows, pl.ds(j0, blk)]

        def local_softmax(s, m_blk, picked):
            shift = m_blk if picked is None else jnp.where(picked, m_blk, -NEG_INF)
            p = jnp.exp2((s - shift).astype(BF16))
            return p, (m_blk if picked is None else jnp.where(picked, m_blk, NEG_INF))

        own = [scores(h, q0) for h in range(heads)]
        for h in range(heads):
            s_ref[0, h] = scores(h, 0)
        for h in range(heads):
            s = jnp.where(causal, own[h], NEG_INF)
            p, m = local_softmax(s, jnp.max(s, axis=0, keepdims=True), None)
            mb_ref[1, h] = m
            pv_ref[1, h] = jnp.dot(values(h, q0), p, preferred_element_type=F32)
            m_ref[h] = jnp.full((1, blk), NEG_INF, F32)
            acc_ref[h] = jnp.zeros((vrows, blk), F32)

        def fold(slot):
            for h in range(heads):
                m_blk = mb_ref[slot, h]
                m_old = m_ref[h]
                m_new = jnp.maximum(m_old, m_blk)
                m_ref[h] = m_new
                acc_ref[h] = jnp.exp2(m_old - m_new) * acc_ref[h] + jnp.exp2(m_blk - m_new) * pv_ref[slot, h]

        def step(j, cur, nxt, live):
            jn0 = pl.multiple_of(jnp.minimum(j + 1, nb - 1) * blk, blk)
            for h in range(heads):
                s_ref[nxt, h] = scores(h, jn0)
            jc = jnp.minimum(j, nb - 1)
            jc0 = pl.multiple_of(jc * blk, blk)
            for h in range(heads):
                picked = jnp.logical_and(sel_ref[0, h, pl.ds(jc, 1), pl.ds(q0, blk)] > 0.5, live)
                s = s_ref[cur, h]
                p, m = local_softmax(s, jnp.max(s, axis=0, keepdims=True), picked)
                pv_ref[cur, h] = jnp.dot(values(h, jc0), p, preferred_element_type=F32)
                mb_ref[cur, h] = m
            fold(nxt)

        def block_pair(t, carry):
            step(2 * t, 0, 1, True)
            step(2 * t + 1, 1, 0, 2 * t + 1 < i)
            return carry

        lax.fori_loop(0, (i + 1) // 2, block_pair, 0)
        fold(1)
        for h in range(heads):
            acc = acc_ref[h]
            o_ref[0, h * hd:(h + 1) * hd, pl.ds(q0, blk)] = acc[:hd] / acc[hd:hd + 1]
        return carry

    lax.fori_loop(0, nb, q_block, 0)


def _moba_attn(qt, kn, vt, sel):
    bsz, _, seq = qt.shape
    nb = seq // MOBA_BLOCK
    hg = ATT_HEADS_PER_STEP
    return pl.pallas_call(
        _moba_attn_kernel,
        grid=(bsz, ATT_HEADS // hg),
        in_specs=[
            pl.BlockSpec((1, hg * ATT_HEAD_DIM, seq), lambda b, h: (b, h, 0)),
            pl.BlockSpec((1, hg, seq, ATT_HEAD_DIM), lambda b, h: (b, h, 0, 0)),
            pl.BlockSpec((1, hg * ATT_V_ROWS, seq), lambda b, h: (b, h, 0)),
            pl.BlockSpec((1, hg, nb, seq), lambda b, h: (b, h, 0, 0)),
        ],
        out_specs=pl.BlockSpec((1, hg * ATT_HEAD_DIM, seq), lambda b, h: (b, h, 0)),
        out_shape=jax.ShapeDtypeStruct((bsz, ATT_WIDTH, seq), F32),
        scratch_shapes=[pltpu.VMEM((hg, 1, MOBA_BLOCK), F32),
                        pltpu.VMEM((hg, ATT_V_ROWS, MOBA_BLOCK), F32),
                        pltpu.VMEM((2, hg, MOBA_BLOCK, MOBA_BLOCK), F32),
                        pltpu.VMEM((2, hg, ATT_V_ROWS, MOBA_BLOCK), F32),
                        pltpu.VMEM((2, hg, 1, MOBA_BLOCK), F32)],
        compiler_params=_cparams(2),
        name="moba_attn",
    )(qt, kn, vt, sel)


def _merge_kernel(x_ref, g_ref, ys_ref, yr_ref, at_ref, wg_ref, ws_ref, wr_ref, wa_ref, wo_ref, o_ref,
                  ysp_ref):
    x = x_ref[...]
    tm = x.shape[0]
    xn = _rms_rows(x, g_ref[...]).astype(BF16)
    for chunk in range(tm // LS_S5):
        for s in range(SUBLANES):
            for t0 in range(0, S5_TSUB, SUBLANES):
                r = chunk * LS_S5 + s * S5_TSUB + t0
                for c in range(SSM_WIDTH // LANES):
                    ysp_ref[r:r + SUBLANES, c * LANES:(c + 1) * LANES] = ys_ref[
                        c, pl.ds(chunk * LS_S5 + SUBLANES * t0 + s, SUBLANES, stride=SUBLANES), :]

    def gate(k):
        return _sigmoid(jnp.dot(xn, wg_ref[:, k * D_MODEL:(k + 1) * D_MODEL], preferred_element_type=F32))

    merged = gate(0) * jnp.dot(ysp_ref[...].astype(BF16), ws_ref[...], preferred_element_type=F32)
    merged += gate(1) * jnp.dot(yr_ref[...].astype(BF16), wr_ref[...], preferred_element_type=F32)
    merged += gate(2) * lax.dot_general(at_ref[0].astype(BF16), wa_ref[...], (((0,), (0,)), ((), ())),
                                        preferred_element_type=F32)
    o_ref[...] = x + jnp.dot(merged.astype(BF16), wo_ref[...], preferred_element_type=F32)


def _merge(x2d, g, y_s, y_r, o_t, w_gates, w_s, w_r, w_a, w_o, bsz, seq):
    tm = TM_MERGE
    per_seq = seq // tm
    row = lambda i: (i, 0)
    const = lambda i: (0, 0)
    return pl.pallas_call(
        _merge_kernel,
        grid=(bsz * per_seq,),
        in_specs=[
            pl.BlockSpec((tm, D_MODEL), row),
            pl.BlockSpec((1, D_MODEL), const),
            pl.BlockSpec((SSM_WIDTH // LANES, tm, LANES), lambda i: (0, i, 0)),
            pl.BlockSpec((tm, RET_V_WIDTH), row),
            pl.BlockSpec((1, ATT_WIDTH, tm), lambda i: (i // per_seq, 0, i % per_seq)),
            pl.BlockSpec((D_MODEL, N_GATES), const),
            pl.BlockSpec((SSM_WIDTH, D_MODEL), const),
            pl.BlockSpec((RET_V_WIDTH, D_MODEL), const),
            pl.BlockSpec((ATT_WIDTH, D_MODEL), const),
            pl.BlockSpec((D_MODEL, D_MODEL), const),
        ],
        out_specs=pl.BlockSpec((tm, D_MODEL), row),
        out_shape=jax.ShapeDtypeStruct(x2d.shape, F32),
        scratch_shapes=[pltpu.VMEM((tm, SSM_WIDTH), F32)],
        compiler_params=_cparams(1),
        name="merge",
    )(x2d, g, y_s, y_r, o_t, w_gates, w_s, w_r, w_a, w_o)


def _ffn_kernel(x_ref, halo_ref, g_ref, wg_ref, wu_ref, cwg_ref, cwu_ref, cbg_ref, cbu_ref,
                wd_ref, o_ref, xn_ref, acc_ref, *, tiles_per_seq):
    i = pl.program_id(0)
    c = pl.program_id(1)
    tm = x_ref.shape[0]

    @pl.when(c == 0)
    def _():
        g = g_ref[...]
        xn_ref[0:HALO, :] = _rms_rows(halo_ref[...], g).astype(BF16)
        xn_ref[HALO:, :] = _rms_rows(x_ref[...], g).astype(BF16)
        acc_ref[...] = jnp.zeros(acc_ref.shape, F32)

    inside_seq = i % tiles_per_seq != 0

    def conv(w_ref, cw_ref, cb_ref):
        h = jnp.dot(xn_ref[...], w_ref[...], preferred_element_type=F32)
        h = jnp.concatenate([jnp.where(inside_seq, h[:HALO], 0.0), h[HALO:]], axis=0)
        cw = cw_ref[...]
        return (cb_ref[...] + h[HALO - 2:tm + HALO - 2] * cw[0:1] + h[HALO - 1:tm + HALO - 1] * cw[1:2]
                + h[HALO:] * cw[2:3])

    hg = conv(wg_ref, cwg_ref, cbg_ref)
    hu = conv(wu_ref, cwu_ref, cbu_ref)
    act = (hg * _sigmoid(hg) * hu).astype(BF16)
    acc_ref[...] += jnp.dot(act, wd_ref[...], preferred_element_type=F32)

    @pl.when(c == pl.num_programs(1) - 1)
    def _():
        o_ref[...] = x_ref[...] + acc_ref[...]


def _ffn(x2d, g, w_up, conv_w, conv_b, w_down, seq):
    t = x2d.shape[0]
    tm, fc = TM_FFN, FC_FFN
    nfc = FFN_HIDDEN // fc
    halo_blocks = tm // HALO
    wmode = dict(pipeline_mode=pl.Buffered(1)) if nfc == 1 else {}
    return pl.pallas_call(
        functools.partial(_ffn_kernel, tiles_per_seq=seq // tm),
        grid=(t // tm, nfc),
        in_specs=[
            pl.BlockSpec((tm, D_MODEL), lambda i, c: (i, 0)),
            pl.BlockSpec((HALO, D_MODEL), lambda i, c: (jnp.maximum(i * halo_blocks - 1, 0), 0)),
            pl.BlockSpec((1, D_MODEL), lambda i, c: (0, 0)),
            pl.BlockSpec((D_MODEL, fc), lambda i, c: (0, c), **wmode),
            pl.BlockSpec((D_MODEL, fc), lambda i, c: (0, c + nfc), **wmode),
            pl.BlockSpec((CONV_WIDTH, fc), lambda i, c: (0, c)),
            pl.BlockSpec((CONV_WIDTH, fc), lambda i, c: (0, c + nfc)),
            pl.BlockSpec((1, fc), lambda i, c: (0, c)),
            pl.BlockSpec((1, fc), lambda i, c: (0, c + nfc)),
            pl.BlockSpec((fc, D_MODEL), lambda i, c: (c, 0), **wmode),
        ],
        out_specs=pl.BlockSpec((tm, D_MODEL), lambda i, c: (i, 0)),
        out_shape=jax.ShapeDtypeStruct(x2d.shape, F32),
        scratch_shapes=[pltpu.VMEM((tm + HALO, D_MODEL), BF16), pltpu.VMEM((tm, D_MODEL), F32)],
        compiler_params=_cparams(2),
        name="ffn",
    )(x2d, x2d, g, w_up, w_up, conv_w, conv_w, conv_b, conv_b, w_down)


def kernel(x, norm1_g, w_in, ssm_lambda_re, ssm_lambda_im, ssm_log_dt, ssm_b_re, ssm_b_im, ssm_c_re, ssm_c_im, ssm_d, ssm_w_glu, ret_gn_g, attn_qn_g, attn_kn_g, w_br_ssm, w_br_ret, w_br_att, w_o, norm2_g, ffn_w_up, ffn_conv_w, ffn_conv_b, ffn_w_down):
    bsz, seq, _ = x.shape
    depth = w_in.shape[0]
    assert seq % MOBA_BLOCK == 0 and seq % TM_PROJ == 0 and seq % TM_FFN == 0
    ret_tables = _ret_tables(seq)
    moba_tables = _moba_tables(seq)
    gpb = SSM_GROUPS // S5_BLOCKS
    eye = jnp.eye(gpb, dtype=F32)

    def embed_c(c):
        c4 = c.reshape(S5_BLOCKS, gpb, SSM_GROUP, SSM_STATE)
        return jnp.einsum('bghp,gk->bgpkh', c4, eye).reshape(S5_BLOCKS, S5_BLK_STATES, S5_BLK_CH)

    x2d = x.reshape(bsz * seq, D_MODEL)
    for l in range(depth):
        w = w_in[l]
        w_main = w[:, :N_MAIN].astype(BF16)
        w_att_t = w[:, N_MAIN:N_MAIN + N_ATT].T.astype(BF16)
        w_gates = w[:, N_MAIN + N_ATT:].astype(BF16)
        norm1 = norm1_g[l].reshape(1, D_MODEL)
        cmat = jnp.concatenate([embed_c(ssm_c_re[l]), -embed_c(ssm_c_im[l])], axis=1).astype(BF16)

        bmat, coef_a, coef_s, ptab = _s5_prep(ssm_lambda_re[l], ssm_lambda_im[l], ssm_log_dt[l],
                                              ssm_b_re[l], ssm_b_im[l])
        u3, proj, qt, kn, vt, sel = _in_proj(x2d, norm1, w_main, w_att_t, attn_qn_g[l], attn_kn_g[l],
                                             moba_tables, bsz, seq)
        y_s = _s5(u3, bmat, coef_a, coef_s, ptab, cmat, ssm_d[l].reshape(1, SSM_WIDTH),
                  ssm_w_glu[l].astype(BF16), bsz, seq)
        y_r = _retention(proj, ret_gn_g[l].reshape(1, RET_V_WIDTH), ret_tables, bsz, seq)
        o_t = _moba_attn(qt, kn, vt, sel)
        x2d = _merge(x2d, norm1, y_s, y_r, o_t, w_gates, w_br_ssm[l].astype(BF16), w_br_ret[l].astype(BF16),
                     w_br_att[l].astype(BF16), w_o[l].astype(BF16), bsz, seq)
        x2d = _ffn(x2d, norm2_g[l].reshape(1, D_MODEL), ffn_w_up[l].astype(BF16), ffn_conv_w[l],
                   ffn_conv_b[l].reshape(1, 2 * FFN_HIDDEN), ffn_w_down[l].astype(BF16), seq)
    return x2d.reshape(bsz, seq, D_MODEL)
```

```python
import functools
import math

import numpy as np
import jax
import jax.numpy as jnp
from jax import lax
from jax.experimental import pallas as pl
from jax.experimental.pallas import tpu as pltpu

F32 = jnp.float32
BF16 = jnp.bfloat16

D_MODEL = 1024
SSM_WIDTH = 512
SSM_GROUP = 16
SSM_GROUPS = 32
SSM_STATE = 64
SSM_NSTATE = SSM_GROUPS * SSM_STATE
RET_HEADS = 4
RET_QK_DIM = 64
RET_V_DIM = 128
RET_QK_WIDTH = 256
RET_V_WIDTH = 512
RET_THETA = 10000.0
ATT_HEADS = 8
ATT_HEAD_DIM = 64
ATT_WIDTH = 512
MOBA_BLOCK = 256
MOBA_TOPK = 3
ROPE_THETA = 500000.0
ROPE_HALF = 8
FFN_HIDDEN = 2816
CONV_WIDTH = 3
NORM_EPS = 1e-6
NEG_INF = -1e30

N_MAIN = SSM_WIDTH + 2 * RET_QK_WIDTH + 2 * RET_V_WIDTH
N_ATT = 3 * ATT_WIDTH
N_GATES = 3 * D_MODEL

SUBLANES = 8
LANES = 128
VMEM_LIMIT = 52 * 1024 * 1024

TM_PROJ = 512
LS_S5 = 256
S5_TSUB = LS_S5 // SUBLANES
S5_BLOCKS = 2
S5_BLK_CH = SSM_WIDTH // S5_BLOCKS
S5_BLK_STATES = SSM_NSTATE // S5_BLOCKS
S5_SCAN_COLS = 4
RET_CHUNK = 256
TM_MERGE = 512
TM_FFN = 512
FC_FFN = 2816
HALO = SUBLANES
ATT_HEADS_PER_STEP = 4
ATT_V_ROWS = ATT_HEAD_DIM + 16
ATT_Q_SCALE = math.log2(math.e) * ATT_HEAD_DIM ** -0.5


def _cparams(n_axes):
    return pltpu.CompilerParams(dimension_semantics=("arbitrary",) * n_axes,
                                vmem_limit_bytes=VMEM_LIMIT)


def _sigmoid(x):
    return 1.0 / (1.0 + jnp.exp(-x))


def _rms_rows(x, g):
    ms = jnp.mean(x * x, axis=-1, keepdims=True)
    return x * lax.rsqrt(ms + NORM_EPS) * g


def _s5_prep_kernel(lre_ref, lim_ref, ldt_ref, bre_ref, bim_ref, bmat_ref, ca_ref, cs_ref, pt_ref):
    lre = lre_ref[...]
    lim = lim_ref[...]
    dt = jnp.exp(ldt_ref[...])

    def lam_pow(k):
        mag = jnp.exp(k * lre * dt)
        ang = k * lim * dt
        return mag * jnp.cos(ang), mag * jnp.sin(ang)

    ar, ai = lam_pow(1.0)
    x = ar - 1.0
    den = lre * lre + lim * lim
    f_re = (x * lre + ai * lim) / den
    f_im = (ai * lre - x * lim) / den
    for bk in range(S5_BLOCKS):
        sl = slice(bk * S5_BLK_STATES, (bk + 1) * S5_BLK_STATES)
        bre = bre_ref[bk]
        bim = bim_ref[bk]
        bmat_ref[bk, :, :S5_BLK_STATES] = (f_re[:, sl] * bre - f_im[:, sl] * bim).astype(BF16)
        bmat_ref[bk, :, S5_BLK_STATES:] = (f_re[:, sl] * bim + f_im[:, sl] * bre).astype(BF16)

    tile = (SUBLANES, SSM_NSTATE)
    ca_ref[0:8, :] = jnp.broadcast_to(ar, tile)
    ca_ref[8:16, :] = jnp.broadcast_to(ai, tile)
    row = lax.broadcasted_iota(jnp.int32, tile, 0)
    for idx, k in enumerate((1, 2, 4)):
        pr, pi = lam_pow(float(k * S5_TSUB))
        keep = row >= k
        cs_ref[idx * 16:idx * 16 + 8, :] = jnp.where(keep, jnp.broadcast_to(pr, tile), 0.0)
        cs_ref[idx * 16 + 8:idx * 16 + 16, :] = jnp.where(keep, jnp.broadcast_to(pi, tile), 0.0)
    pr, pi = lam_pow(float(S5_TSUB))
    cs_ref[48:56, :] = jnp.broadcast_to(pr, tile)
    cs_ref[56:64, :] = jnp.broadcast_to(pi, tile)
    steps = (lax.broadcasted_iota(jnp.int32, (S5_TSUB, SSM_NSTATE), 0) + 1).astype(F32)
    pr, pi = lam_pow(steps)
    for t in range(S5_TSUB):
        pt_ref[0, t * SUBLANES:(t + 1) * SUBLANES, :] = jnp.broadcast_to(pr[t:t + 1], tile)
        pt_ref[1, t * SUBLANES:(t + 1) * SUBLANES, :] = jnp.broadcast_to(pi[t:t + 1], tile)


def _s5_prep(lam_re, lam_im, log_dt, b_re, b_im):
    gpb = SSM_GROUPS // S5_BLOCKS
    eye = jnp.eye(gpb, dtype=F32)

    def embed(b):
        b4 = b.reshape(S5_BLOCKS, gpb, SSM_STATE, SSM_GROUP)
        return jnp.einsum('bgph,gk->bghkp', b4, eye).reshape(S5_BLOCKS, S5_BLK_CH, S5_BLK_STATES)

    ldt = jnp.repeat(log_dt, SSM_STATE).reshape(1, SSM_NSTATE)
    return pl.pallas_call(
        _s5_prep_kernel,
        out_shape=(jax.ShapeDtypeStruct((S5_BLOCKS, S5_BLK_CH, 2 * S5_BLK_STATES), BF16),
                   jax.ShapeDtypeStruct((16, SSM_NSTATE), F32),
                   jax.ShapeDtypeStruct((64, SSM_NSTATE), F32),
                   jax.ShapeDtypeStruct((2, LS_S5, SSM_NSTATE), F32)),
        compiler_params=pltpu.CompilerParams(vmem_limit_bytes=VMEM_LIMIT),
        name="s5_prep",
    )(lam_re.reshape(1, SSM_NSTATE), lam_im.reshape(1, SSM_NSTATE), ldt, embed(b_re), embed(b_im))


def _in_proj_kernel(x_ref, g_ref, wm_ref, wat_ref, qg_ref, kg_ref, cos_ref, sin_ref,
                    rcos_ref, rsin_ref, decay_ref, zeta_ref, xi_ref, cdec_ref, gn_ref,
                    u_ref, yr_ref, qt_ref, kn_ref, vt_ref, sel_ref, att_ref, kmean_ref, state_ref,
                    *, tiles_per_seq):
    tm = x_ref.shape[0]
    tile_in_seq = pl.program_id(0) % tiles_per_seq

    @pl.when(tile_in_seq == 0)
    def _():
        kmean_ref[...] = jnp.zeros(kmean_ref.shape, F32)
        state_ref[...] = jnp.zeros(state_ref.shape, F32)

    xn = _rms_rows(x_ref[...], g_ref[...]).astype(BF16)
    att_ref[...] = lax.dot_general(wat_ref[...], xn, (((1,), (1,)), ((), ())),
                                   preferred_element_type=F32)
    proj = jnp.dot(xn, wm_ref[...], preferred_element_type=F32)
    q0, k0, v0, g0 = SSM_WIDTH, SSM_WIDTH + RET_QK_WIDTH, SSM_WIDTH + 2 * RET_QK_WIDTH, N_MAIN - RET_V_WIDTH
    for chunk in range(tm // RET_CHUNK):
        rows = slice(chunk * RET_CHUNK, (chunk + 1) * RET_CHUNK)
        yr_ref[rows, :] = _retention_chunk(
            proj[rows, q0:k0], proj[rows, k0:v0], proj[rows, v0:g0], proj[rows, g0:],
            rcos_ref[rows, :], rsin_ref[rows, :], decay_ref, zeta_ref[...], xi_ref[...], cdec_ref[...],
            gn_ref[...], state_ref)
    blocks_per_tile = tm // MOBA_BLOCK
    for blk in range(blocks_per_tile):
        _moba_prep_block(att_ref, slice(blk * MOBA_BLOCK, (blk + 1) * MOBA_BLOCK),
                         tile_in_seq * blocks_per_tile + blk, qg_ref[...], kg_ref[...], cos_ref, sin_ref,
                         qt_ref, kn_ref, vt_ref, sel_ref, kmean_ref)
    for chunk in range(tm // LS_S5):
        for s in range(SUBLANES):
            for t0 in range(0, S5_TSUB, SUBLANES):
                r = chunk * LS_S5 + s * S5_TSUB + t0
                for c in range(SSM_WIDTH // LANES):
                    u_ref[c, pl.ds(chunk * LS_S5 + SUBLANES * t0 + s, SUBLANES, stride=SUBLANES), :] = (
                        proj[r:r + SUBLANES, c * LANES:(c + 1) * LANES])


def _in_proj(x2d, g, w_main, w_att_t, qn_g, kn_g, tables, gn_g, ret_tables, bsz, seq):
    t = x2d.shape[0]
    tm = TM_PROJ
    tps = seq // tm
    nb = seq // MOBA_BLOCK
    cos_t, sin_t = tables
    rcos, rsin, decay, zeta, xi, cdec = ret_tables
    const = lambda i: (0, 0)
    return pl.pallas_call(
        functools.partial(_in_proj_kernel, tiles_per_seq=tps),
        grid=(t // tm,),
        in_specs=[
            pl.BlockSpec((tm, D_MODEL), lambda i: (i, 0)),
            pl.BlockSpec((1, D_MODEL), const),
            pl.BlockSpec((D_MODEL, N_MAIN), const),
            pl.BlockSpec((N_ATT, D_MODEL), const),
            pl.BlockSpec((ATT_HEAD_DIM, 1), const),
            pl.BlockSpec((ATT_HEAD_DIM, 1), const),
            pl.BlockSpec((ROPE_HALF, tm), lambda i: (0, i % tps)),
            pl.BlockSpec((ROPE_HALF, tm), lambda i: (0, i % tps)),
            pl.BlockSpec((tm, RET_QK_WIDTH), lambda i: (i % tps, 0)),
            pl.BlockSpec((tm, RET_QK_WIDTH), lambda i: (i % tps, 0)),
            pl.BlockSpec((RET_HEADS, RET_CHUNK, RET_CHUNK), lambda i: (0, 0, 0)),
            pl.BlockSpec((RET_CHUNK, RET_QK_WIDTH), const),
            pl.BlockSpec((RET_CHUNK, RET_QK_WIDTH), const),
            pl.BlockSpec((1, RET_V_WIDTH), const),
            pl.BlockSpec((1, RET_V_WIDTH), const),
        ],
        out_specs=(
            pl.BlockSpec((SSM_WIDTH // LANES, tm, LANES), lambda i: (0, i, 0)),
            pl.BlockSpec((tm, RET_V_WIDTH), lambda i: (i, 0)),
            pl.BlockSpec((1, ATT_WIDTH, tm), lambda i: (i // tps, 0, i % tps)),
            pl.BlockSpec((1, ATT_HEADS, tm, ATT_HEAD_DIM), lambda i: (i // tps, 0, i % tps, 0)),
            pl.BlockSpec((1, ATT_HEADS * ATT_V_ROWS, tm), lambda i: (i // tps, 0, i % tps)),
            pl.BlockSpec((1, ATT_HEADS, nb, tm), lambda i: (i // tps, 0, 0, i % tps)),
        ),
        out_shape=(
            jax.ShapeDtypeStruct((SSM_WIDTH // LANES, t, LANES), F32),
            jax.ShapeDtypeStruct((t, RET_V_WIDTH), F32),
            jax.ShapeDtypeStruct((bsz, ATT_WIDTH, seq), BF16),
            jax.ShapeDtypeStruct((bsz, ATT_HEADS, seq, ATT_HEAD_DIM), BF16),
            jax.ShapeDtypeStruct((bsz, ATT_HEADS * ATT_V_ROWS, seq), BF16),
            jax.ShapeDtypeStruct((bsz, ATT_HEADS, nb, seq), F32),
        ),
        scratch_shapes=[pltpu.VMEM((N_ATT, tm), F32),
                        pltpu.VMEM((ATT_HEADS, nb, ATT_HEAD_DIM), F32),
                        pltpu.VMEM((RET_HEADS, RET_QK_DIM, RET_V_DIM), F32)],
        compiler_params=_cparams(1),
        name="in_proj",
    )(x2d, g, w_main, w_att_t, qn_g.reshape(ATT_HEAD_DIM, 1), kn_g.reshape(ATT_HEAD_DIM, 1), cos_t, sin_t,
      rcos, rsin, decay, zeta, xi, cdec, gn_g)


def _s5_kernel(u_ref, bmat_ref, ca_ref, cs_ref, pt_ref, cmat_ref, d_ref, wglu_ref, o_ref,
               st_ref, xb_ref, cin_ref):
    ls = st_ref.shape[0]
    tsub = ls // SUBLANES
    nslab = SSM_WIDTH // LANES

    @pl.when(pl.program_id(1) == 0)
    def _():
        cin_ref[...] = jnp.zeros(cin_ref.shape, F32)

    u = jnp.concatenate([u_ref[c] for c in range(nslab)], axis=1)
    ub = u.astype(BF16)
    for bk in range(S5_BLOCKS):
        st_ref[:, bk * 2 * S5_BLK_STATES:(bk + 1) * 2 * S5_BLK_STATES] = jnp.dot(
            ub[:, bk * S5_BLK_CH:(bk + 1) * S5_BLK_CH], bmat_ref[bk], preferred_element_type=F32)

    row = lax.broadcasted_iota(jnp.int32, (SUBLANES, LANES), 0)
    tile = (SUBLANES, LANES)

    def columns(cb):
        per_blk = S5_BLK_STATES // LANES
        base = (cb // per_blk) * 2 * S5_BLK_STATES + (cb % per_blk) * LANES
        return (slice(base, base + LANES), slice(base + S5_BLK_STATES, base + S5_BLK_STATES + LANES),
                slice(cb * LANES, (cb + 1) * LANES))

    def cmul(ar, ai, xr, xi):
        return ar * xr - ai * xi, ar * xi + ai * xr

    def scan_columns(cbs):
        sl = [columns(cb) for cb in cbs]
        a = [(ca_ref[0:8, s], ca_ref[8:16, s]) for (_, _, s) in sl]
        first = []
        for re, im, _ in sl:
            first += [st_ref[0:SUBLANES, re], st_ref[0:SUBLANES, im]]

        def local_step(t, carry):
            r0 = t * SUBLANES
            out = []
            for k, (re, im, _) in enumerate(sl):
                pr, pi = cmul(a[k][0], a[k][1], carry[2 * k], carry[2 * k + 1])
                nr = st_ref[pl.ds(r0, SUBLANES), re] + pr
                ni = st_ref[pl.ds(r0, SUBLANES), im] + pi
                st_ref[pl.ds(r0, SUBLANES), re] = nr
                st_ref[pl.ds(r0, SUBLANES), im] = ni
                out += [nr, ni]
            return tuple(out)

        ends = tuple(first)
        for t in range(1, tsub):
            ends = local_step(t, ends)

        entering = []
        for k, (re, im, s) in enumerate(sl):
            fr, fi = ends[2 * k], ends[2 * k + 1]
            gr = jnp.where(row == 0, cin_ref[:, re], pltpu.roll(fr, 1, 0))
            gi = jnp.where(row == 0, cin_ref[:, im], pltpu.roll(fi, 1, 0))
            for idx, kk in enumerate((1, 2, 4)):
                pr, pi = cmul(cs_ref[idx * 16:idx * 16 + 8, s], cs_ref[idx * 16 + 8:idx * 16 + 16, s],
                              pltpu.roll(gr, kk, 0), pltpu.roll(gi, kk, 0))
                gr, gi = gr + pr, gi + pi
            pr, pi = cmul(cs_ref[48:56, s], cs_ref[56:64, s], gr, gi)
            tr, ti = fr + pr, fi + pi
            cin_ref[:, re] = jnp.broadcast_to(tr[SUBLANES - 1:], tile)
            cin_ref[:, im] = jnp.broadcast_to(ti[SUBLANES - 1:], tile)
            entering += [gr, gi]

        def fix_step(tp):
            r16 = tp * 2 * SUBLANES
            for k, (re, im, s) in enumerate(sl):
                halves_r, halves_i = [], []
                for half in range(2):
                    r0 = r16 + half * SUBLANES
                    pr, pi = cmul(pt_ref[0, pl.ds(r0, SUBLANES), s], pt_ref[1, pl.ds(r0, SUBLANES), s],
                                  entering[2 * k], entering[2 * k + 1])
                    halves_r.append(st_ref[pl.ds(r0, SUBLANES), re] + pr)
                    halves_i.append(st_ref[pl.ds(r0, SUBLANES), im] + pi)
                xb_ref[pl.ds(r16, 2 * SUBLANES), re] = jnp.concatenate(halves_r, axis=0).astype(BF16)
                xb_ref[pl.ds(r16, 2 * SUBLANES), im] = jnp.concatenate(halves_i, axis=0).astype(BF16)

        for tp in range(tsub // 2):
            fix_step(tp)

    groups_per_blk = S5_BLK_STATES // LANES // S5_SCAN_COLS
    y_blocks = []
    for bk in range(S5_BLOCKS):
        for grp in range(bk * groups_per_blk, (bk + 1) * groups_per_blk):
            scan_columns(range(grp * S5_SCAN_COLS, (grp + 1) * S5_SCAN_COLS))
        y_blocks.append(jnp.dot(xb_ref[:, bk * 2 * S5_BLK_STATES:(bk + 1) * 2 * S5_BLK_STATES], cmat_ref[bk],
                                preferred_element_type=F32))
    y = jnp.concatenate(y_blocks, axis=1)
    y = y + d_ref[...] * u
    y = 0.5 * y * (1.0 + jnp.tanh(0.7978845608028654 * (y + 0.044715 * (y * y * y))))
    glu = jnp.dot(y.astype(BF16), wglu_ref[...], preferred_element_type=F32)
    out = y * _sigmoid(glu)
    for c in range(nslab):
        o_ref[c] = out[:, c * LANES:(c + 1) * LANES]


def _s5(u3, bmat, coef_a, coef_s, ptab, cmat, d_skip, w_glu, bsz, seq):
    nchunk = seq // LS_S5
    nslab = SSM_WIDTH // LANES
    const2 = lambda b, c: (0, 0)
    const3 = lambda b, c: (0, 0, 0)
    return pl.pallas_call(
        _s5_kernel,
        grid=(bsz, nchunk),
        in_specs=[
            pl.BlockSpec((nslab, LS_S5, LANES), lambda b, c: (0, b * nchunk + c, 0)),
            pl.BlockSpec((S5_BLOCKS, S5_BLK_CH, 2 * S5_BLK_STATES), const3),
            pl.BlockSpec((16, SSM_NSTATE), const2),
            pl.BlockSpec((64, SSM_NSTATE), const2),
            pl.BlockSpec((2, LS_S5, SSM_NSTATE), const3),
            pl.BlockSpec((S5_BLOCKS, 2 * S5_BLK_STATES, S5_BLK_CH), const3),
            pl.BlockSpec((1, SSM_WIDTH), const2),
            pl.BlockSpec((SSM_WIDTH, SSM_WIDTH), const2),
        ],
        out_specs=pl.BlockSpec((nslab, LS_S5, LANES), lambda b, c: (0, b * nchunk + c, 0)),
        out_shape=jax.ShapeDtypeStruct((nslab, bsz * seq, LANES), F32),
        scratch_shapes=[pltpu.VMEM((LS_S5, 2 * SSM_NSTATE), F32),
                        pltpu.VMEM((LS_S5, 2 * SSM_NSTATE), BF16),
                        pltpu.VMEM((SUBLANES, 2 * SSM_NSTATE), F32)],
        compiler_params=_cparams(2),
        name="s5",
    )(u3, bmat, coef_a, coef_s, ptab, cmat, d_skip, w_glu)


def _ret_tables(seq):
    c = RET_CHUNK
    half = RET_QK_DIM // 2
    inv = 1.0 / (RET_THETA ** np.linspace(0.0, 1.0, half))
    ang = np.arange(seq)[:, None] * inv[None, :]
    cos, sin = np.cos(ang), np.sin(ang)
    cos_full = np.tile(np.concatenate([cos, cos], axis=1), (1, RET_HEADS))
    sin_sgn = np.tile(np.concatenate([-sin, sin], axis=1), (1, RET_HEADS))
    log_gamma = np.log1p(-np.exp2(-5.0 - np.arange(RET_HEADS)))
    pos = np.arange(c)
    rel = pos[:, None] - pos[None, :]
    decay = np.where(rel >= 0, np.exp(log_gamma[:, None, None] * np.maximum(rel, 0)[None]), 0.0)
    zeta = np.exp(log_gamma[None, :] * (c - 1.0 - pos)[:, None])
    xi = np.exp(log_gamma[None, :] * (pos + 1.0)[:, None])
    zeta = np.repeat(zeta, RET_QK_DIM, axis=1)
    xi = np.repeat(xi, RET_QK_DIM, axis=1)
    cdec = np.repeat(np.exp(log_gamma * c), RET_V_DIM)[None, :]
    as32 = lambda a: jnp.asarray(a, dtype=F32)
    return as32(cos_full), as32(sin_sgn), as32(decay), as32(zeta), as32(xi), as32(cdec)


def _retention_chunk(q_in, k_in, v_in, g, cos, sin, decay_ref, zeta, xi, cdec, gn, state_ref):
    lane = lax.broadcasted_iota(jnp.int32, cos.shape, 1)
    first_half = (lane % RET_QK_DIM) < (RET_QK_DIM // 2)

    def rot(x):
        swapped = jnp.where(first_half, pltpu.roll(x, RET_QK_WIDTH - RET_QK_DIM // 2, 1),
                            pltpu.roll(x, RET_QK_DIM // 2, 1))
        return x * cos + swapped * sin

    q = rot(q_in)
    k = rot(k_in) * (RET_QK_DIM ** -0.5)
    qx = (q * xi).astype(BF16)
    kz = (k * zeta).astype(BF16)
    qb = q.astype(BF16)
    kb = k.astype(BF16)
    vb = v_in.astype(BF16)
    outs = []
    for h in range(RET_HEADS):
        qs = slice(h * RET_QK_DIM, (h + 1) * RET_QK_DIM)
        vs = slice(h * RET_V_DIM, (h + 1) * RET_V_DIM)
        s = lax.dot_general(qb[:, qs], kb[:, qs], (((1,), (1,)), ((), ())),
                            preferred_element_type=F32) * decay_ref[h]
        state = state_ref[h]
        o = (jnp.dot(s.astype(BF16), vb[:, vs], preferred_element_type=F32)
             + jnp.dot(qx[:, qs], state.astype(BF16), preferred_element_type=F32))
        kv = lax.dot_general(kz[:, qs], vb[:, vs], (((0,), (0,)), ((), ())),
                             preferred_element_type=F32)
        state_ref[h] = cdec[:, vs] * state + kv
        mu = jnp.mean(o, axis=-1, keepdims=True)
        oc = o - mu
        var = jnp.mean(oc * oc, axis=-1, keepdims=True)
        on = oc * lax.rsqrt(var + NORM_EPS) * gn[:, vs]
        gh = g[:, vs]
        outs.append(gh * _sigmoid(gh) * on)
    return jnp.concatenate(outs, axis=1)


def _moba_tables(seq):
    inv = ROPE_THETA ** (-np.arange(ROPE_HALF) / ROPE_HALF)
    ang = inv[:, None] * np.arange(seq)[None, :]
    return jnp.asarray(np.cos(ang), dtype=F32), jnp.asarray(np.sin(ang), dtype=F32)


def _moba_prep_block(att_ref, cols, i_blk, qg, kg, cos_ref, sin_ref, qt_ref, kn_ref, vt_ref, sel_ref, kmean_ref):
    nb = kmean_ref.shape[1]
    cos = cos_ref[:, cols]
    sin = sin_ref[:, cols]

    def norm_rot(x, g):
        ms = jnp.mean(x * x, axis=0, keepdims=True)
        xn = x * lax.rsqrt(ms + NORM_EPS) * g
        x1 = xn[0:ROPE_HALF]
        x2 = xn[ROPE_HALF:2 * ROPE_HALF]
        return jnp.concatenate([x1 * cos - x2 * sin, x1 * sin + x2 * cos, xn[2 * ROPE_HALF:]], axis=0)

    def split(a):
        hi = a.astype(BF16)
        return hi, (a - hi.astype(F32)).astype(BF16)

    row = lax.broadcasted_iota(jnp.int32, (nb, MOBA_BLOCK), 0)
    past = row < i_blk
    for h in range(ATT_HEADS):
        hs = slice(h * ATT_HEAD_DIM, (h + 1) * ATT_HEAD_DIM)
        q = norm_rot(att_ref[hs, cols], qg)
        k = norm_rot(att_ref[ATT_WIDTH + h * ATT_HEAD_DIM:ATT_WIDTH + (h + 1) * ATT_HEAD_DIM, cols], kg)
        kn = k.T
        kn_ref[0, h, cols, :] = kn.astype(BF16)
        qt_ref[0, hs, cols] = (q * ATT_Q_SCALE).astype(BF16)
        v0 = 2 * ATT_WIDTH + h * ATT_HEAD_DIM
        vt_ref[0, h * ATT_V_ROWS:h * ATT_V_ROWS + ATT_HEAD_DIM, cols] = att_ref[v0:v0 + ATT_HEAD_DIM, cols].astype(BF16)
        vt_ref[0, h * ATT_V_ROWS + ATT_HEAD_DIM:(h + 1) * ATT_V_ROWS, cols] = jnp.ones(
            (ATT_V_ROWS - ATT_HEAD_DIM, MOBA_BLOCK), BF16)
        kmean_ref[h, pl.ds(i_blk, 1), :] = jnp.mean(kn, axis=0, keepdims=True)
        mh, ml = split(kmean_ref[h])
        qh, ql = split(q)
        gate = (jnp.dot(mh, qh, preferred_element_type=F32) + jnp.dot(mh, ql, preferred_element_type=F32)
                + jnp.dot(ml, qh, preferred_element_type=F32))
        gate = jnp.where(past, gate, NEG_INF)
        beaten = jnp.zeros(gate.shape, F32)
        for j in range(nb):
            gj = jnp.broadcast_to(gate[j:j + 1, :], gate.shape)
            ahead = jnp.where(gj > gate, 1.0, jnp.where(gj == gate, jnp.where(row > j, 1.0, 0.0), 0.0))
            beaten = beaten + ahead
        sel_ref[0, h, :, cols] = jnp.where(past, jnp.where(beaten < MOBA_TOPK, 1.0, 0.0), 0.0)


def _moba_attn_kernel(qt_ref, k_ref, vt_ref, sel_ref, o_ref, m_ref, acc_ref, s_ref, pv_ref, mb_ref):
    blk = MOBA_BLOCK
    hd = ATT_HEAD_DIM
    nb = sel_ref.shape[2]
    heads = k_ref.shape[1]
    vrows = ATT_V_ROWS
    kpos = lax.broadcasted_iota(jnp.int32, (blk, blk), 0)
    qpos = lax.broadcasted_iota(jnp.int32, (blk, blk), 1)
    causal = kpos <= qpos

    def q_block(i, carry):
        q0 = pl.multiple_of(i * blk, blk)

        def scores(h, j0):
            return jnp.dot(k_ref[0, h, pl.ds(j0, blk), :], qt_ref[0, h * hd:(h + 1) * hd, pl.ds(q0, blk)],
                           preferred_element_type=F32)

        def values(h, j0):
            return vt_ref[0, h * vrows:(h + 1) * vrows, pl.ds(j0, blk)]

        def local_softmax(s, m_blk, picked):
            shift = m_blk if picked is None else jnp.where(picked, m_blk, -NEG_INF)
            p = jnp.exp2((s - shift).astype(BF16))
            return p, (m_blk if picked is None else jnp.where(picked, m_blk, NEG_INF))

        own = [scores(h, q0) for h in range(heads)]
        for h in range(heads):
            s_ref[0, h] = scores(h, 0)
        for h in range(heads):
            s = jnp.where(causal, own[h], NEG_INF)
            p, m = local_softmax(s, jnp.max(s, axis=0, keepdims=True), None)
            mb_ref[1, h] = m
            pv_ref[1, h] = jnp.dot(values(h, q0), p, preferred_element_type=F32)
            m_ref[h] = jnp.full((1, blk), NEG_INF, F32)
            acc_ref[h] = jnp.zeros((vrows, blk), F32)

        def fold(slot):
            for h in range(heads):
                m_blk = mb_ref[slot, h]
                m_old = m_ref[h]
                m_new = jnp.maximum(m_old, m_blk)
                m_ref[h] = m_new
                acc_ref[h] = jnp.exp2(m_old - m_new) * acc_ref[h] + jnp.exp2(m_blk - m_new) * pv_ref[slot, h]

        def step(j, cur, nxt, live):
            jn0 = pl.multiple_of(jnp.minimum(j + 1, nb - 1) * blk, blk)
            for h in range(heads):
                s_ref[nxt, h] = scores(h, jn0)
            jc = jnp.minimum(j, nb - 1)
            jc0 = pl.multiple_of(jc * blk, blk)
            for h in range(heads):
                picked = jnp.logical_and(sel_ref[0, h, pl.ds(jc, 1), pl.ds(q0, blk)] > 0.5, live)
                s = s_ref[cur, h]
                p, m = local_softmax(s, jnp.max(s, axis=0, keepdims=True), picked)
                pv_ref[cur, h] = jnp.dot(values(h, jc0), p, preferred_element_type=F32)
                mb_ref[cur, h] = m
            fold(nxt)

        def block_pair(t, carry):
            step(2 * t, 0, 1, True)
            step(2 * t + 1, 1, 0, 2 * t + 1 < i)
            return carry

        lax.fori_loop(0, (i + 1) // 2, block_pair, 0)
        fold(1)
        for h in range(heads):
            acc = acc_ref[h]
            o_ref[0, h * hd:(h + 1) * hd, pl.ds(q0, blk)] = acc[:hd] / acc[hd:hd + 1]
        return carry

    lax.fori_loop(0, nb, q_block, 0)


def _moba_attn(qt, kn, vt, sel):
    bsz, _, seq = qt.shape
    nb = seq // MOBA_BLOCK
    hg = ATT_HEADS_PER_STEP
    return pl.pallas_call(
        _moba_attn_kernel,
        grid=(bsz, ATT_HEADS // hg),
        in_specs=[
            pl.BlockSpec((1, hg * ATT_HEAD_DIM, seq), lambda b, h: (b, h, 0)),
            pl.BlockSpec((1, hg, seq, ATT_HEAD_DIM), lambda b, h: (b, h, 0, 0)),
            pl.BlockSpec((1, hg * ATT_V_ROWS, seq), lambda b, h: (b, h, 0)),
            pl.BlockSpec((1, hg, nb, seq), lambda b, h: (b, h, 0, 0)),
        ],
        out_specs=pl.BlockSpec((1, hg * ATT_HEAD_DIM, seq), lambda b, h: (b, h, 0)),
        out_shape=jax.ShapeDtypeStruct((bsz, ATT_WIDTH, seq), F32),
        scratch_shapes=[pltpu.VMEM((hg, 1, MOBA_BLOCK), F32),
                        pltpu.VMEM((hg, ATT_V_ROWS, MOBA_BLOCK), F32),
                        pltpu.VMEM((2, hg, MOBA_BLOCK, MOBA_BLOCK), F32),
                        pltpu.VMEM((2, hg, ATT_V_ROWS, MOBA_BLOCK), F32),
                        pltpu.VMEM((2, hg, 1, MOBA_BLOCK), F32)],
        compiler_params=_cparams(2),
        name="moba_attn",
    )(qt, kn, vt, sel)


def _merge_kernel(x_ref, g_ref, ys_ref, yr_ref, at_ref, wg_ref, ws_ref, wr_ref, wa_ref, wo_ref, o_ref,
                  ysp_ref):
    x = x_ref[...]
    tm = x.shape[0]
    xn = _rms_rows(x, g_ref[...]).astype(BF16)
    for chunk in range(tm // LS_S5):
        for s in range(SUBLANES):
            for t0 in range(0, S5_TSUB, SUBLANES):
                r = chunk * LS_S5 + s * S5_TSUB + t0
                for c in range(SSM_WIDTH // LANES):
                    ysp_ref[r:r + SUBLANES, c * LANES:(c + 1) * LANES] = ys_ref[
                        c, pl.ds(chunk * LS_S5 + SUBLANES * t0 + s, SUBLANES, stride=SUBLANES), :]

    def gate(k):
        return _sigmoid(jnp.dot(xn, wg_ref[:, k * D_MODEL:(k + 1) * D_MODEL], preferred_element_type=F32))

    merged = gate(0) * jnp.dot(ysp_ref[...].astype(BF16), ws_ref[...], preferred_element_type=F32)
    merged += gate(1) * jnp.dot(yr_ref[...].astype(BF16), wr_ref[...], preferred_element_type=F32)
    merged += gate(2) * lax.dot_general(at_ref[0].astype(BF16), wa_ref[...], (((0,), (0,)), ((), ())),
                                        preferred_element_type=F32)
    o_ref[...] = x + jnp.dot(merged.astype(BF16), wo_ref[...], preferred_element_type=F32)


def _merge(x2d, g, y_s, y_r, o_t, w_gates, w_s, w_r, w_a, w_o, bsz, seq):
    tm = TM_MERGE
    per_seq = seq // tm
    row = lambda i: (i, 0)
    const = lambda i: (0, 0)
    return pl.pallas_call(
        _merge_kernel,
        grid=(bsz * per_seq,),
        in_specs=[
            pl.BlockSpec((tm, D_MODEL), row),
            pl.BlockSpec((1, D_MODEL), const),
            pl.BlockSpec((SSM_WIDTH // LANES, tm, LANES), lambda i: (0, i, 0)),
            pl.BlockSpec((tm, RET_V_WIDTH), row),
            pl.BlockSpec((1, ATT_WIDTH, tm), lambda i: (i // per_seq, 0, i % per_seq)),
            pl.BlockSpec((D_MODEL, N_GATES), const),
            pl.BlockSpec((SSM_WIDTH, D_MODEL), const),
            pl.BlockSpec((RET_V_WIDTH, D_MODEL), const),
            pl.BlockSpec((ATT_WIDTH, D_MODEL), const),
            pl.BlockSpec((D_MODEL, D_MODEL), const),
        ],
        out_specs=pl.BlockSpec((tm, D_MODEL), row),
        out_shape=jax.ShapeDtypeStruct(x2d.shape, F32),
        scratch_shapes=[pltpu.VMEM((tm, SSM_WIDTH), F32)],
        compiler_params=_cparams(1),
        name="merge",
    )(x2d, g, y_s, y_r, o_t, w_gates, w_s, w_r, w_a, w_o)


def _ffn_kernel(x_ref, halo_ref, g_ref, wg_ref, wu_ref, cwg_ref, cwu_ref, cbg_ref, cbu_ref,
                wd_ref, o_ref, xn_ref, acc_ref, *, tiles_per_seq):
    i = pl.program_id(0)
    c = pl.program_id(1)
    tm = x_ref.shape[0]

    @pl.when(c == 0)
    def _():
        g = g_ref[...]
        xn_ref[0:HALO, :] = _rms_rows(halo_ref[...], g).astype(BF16)
        xn_ref[HALO:, :] = _rms_rows(x_ref[...], g).astype(BF16)
        acc_ref[...] = jnp.zeros(acc_ref.shape, F32)

    inside_seq = i % tiles_per_seq != 0

    def conv(w_ref, cw_ref, cb_ref):
        h = jnp.dot(xn_ref[...], w_ref[...], preferred_element_type=F32)
        h = jnp.concatenate([jnp.where(inside_seq, h[:HALO], 0.0), h[HALO:]], axis=0)
        cw = cw_ref[...]
        return (cb_ref[...] + h[HALO - 2:tm + HALO - 2] * cw[0:1] + h[HALO - 1:tm + HALO - 1] * cw[1:2]
                + h[HALO:] * cw[2:3])

    hg = conv(wg_ref, cwg_ref, cbg_ref)
    hu = conv(wu_ref, cwu_ref, cbu_ref)
    act = (hg * _sigmoid(hg) * hu).astype(BF16)
    acc_ref[...] += jnp.dot(act, wd_ref[...], preferred_element_type=F32)

    @pl.when(c == pl.num_programs(1) - 1)
    def _():
        o_ref[...] = x_ref[...] + acc_ref[...]


def _ffn(x2d, g, w_up, conv_w, conv_b, w_down, seq):
    t = x2d.shape[0]
    tm, fc = TM_FFN, FC_FFN
    nfc = FFN_HIDDEN // fc
    halo_blocks = tm // HALO
    wmode = dict(pipeline_mode=pl.Buffered(1)) if nfc == 1 else {}
    return pl.pallas_call(
        functools.partial(_ffn_kernel, tiles_per_seq=seq // tm),
        grid=(t // tm, nfc),
        in_specs=[
            pl.BlockSpec((tm, D_MODEL), lambda i, c: (i, 0)),
            pl.BlockSpec((HALO, D_MODEL), lambda i, c: (jnp.maximum(i * halo_blocks - 1, 0), 0)),
            pl.BlockSpec((1, D_MODEL), lambda i, c: (0, 0)),
            pl.BlockSpec((D_MODEL, fc), lambda i, c: (0, c), **wmode),
            pl.BlockSpec((D_MODEL, fc), lambda i, c: (0, c + nfc), **wmode),
            pl.BlockSpec((CONV_WIDTH, fc), lambda i, c: (0, c)),
            pl.BlockSpec((CONV_WIDTH, fc), lambda i, c: (0, c + nfc)),
            pl.BlockSpec((1, fc), lambda i, c: (0, c)),
            pl.BlockSpec((1, fc), lambda i, c: (0, c + nfc)),
            pl.BlockSpec((fc, D_MODEL), lambda i, c: (c, 0), **wmode),
        ],
        out_specs=pl.BlockSpec((tm, D_MODEL), lambda i, c: (i, 0)),
        out_shape=jax.ShapeDtypeStruct(x2d.shape, F32),
        scratch_shapes=[pltpu.VMEM((tm + HALO, D_MODEL), BF16), pltpu.VMEM((tm, D_MODEL), F32)],
        compiler_params=_cparams(2),
        name="ffn",
    )(x2d, x2d, g, w_up, w_up, conv_w, conv_w, conv_b, conv_b, w_down)


def kernel(x, norm1_g, w_in, ssm_lambda_re, ssm_lambda_im, ssm_log_dt, ssm_b_re, ssm_b_im, ssm_c_re, ssm_c_im, ssm_d, ssm_w_glu, ret_gn_g, attn_qn_g, attn_kn_g, w_br_ssm, w_br_ret, w_br_att, w_o, norm2_g, ffn_w_up, ffn_conv_w, ffn_conv_b, ffn_w_down):
    bsz, seq, _ = x.shape
    depth = w_in.shape[0]
    assert seq % MOBA_BLOCK == 0 and seq % TM_PROJ == 0 and seq % TM_FFN == 0
    ret_tables = _ret_tables(seq)
    moba_tables = _moba_tables(seq)
    gpb = SSM_GROUPS // S5_BLOCKS
    eye = jnp.eye(gpb, dtype=F32)

    def embed_c(c):
        c4 = c.reshape(S5_BLOCKS, gpb, SSM_GROUP, SSM_STATE)
        return jnp.einsum('bghp,gk->bgpkh', c4, eye).reshape(S5_BLOCKS, S5_BLK_STATES, S5_BLK_CH)

    x2d = x.reshape(bsz * seq, D_MODEL)
    for l in range(depth):
        w = w_in[l]
        w_main = w[:, :N_MAIN].astype(BF16)
        w_att_t = w[:, N_MAIN:N_MAIN + N_ATT].T.astype(BF16)
        w_gates = w[:, N_MAIN + N_ATT:].astype(BF16)
        norm1 = norm1_g[l].reshape(1, D_MODEL)
        cmat = jnp.concatenate([embed_c(ssm_c_re[l]), -embed_c(ssm_c_im[l])], axis=1).astype(BF16)

        bmat, coef_a, coef_s, ptab = _s5_prep(ssm_lambda_re[l], ssm_lambda_im[l], ssm_log_dt[l],
                                              ssm_b_re[l], ssm_b_im[l])
        u3, y_r, qt, kn, vt, sel = _in_proj(x2d, norm1, w_main, w_att_t, attn_qn_g[l], attn_kn_g[l],
                                            moba_tables, ret_gn_g[l].reshape(1, RET_V_WIDTH), ret_tables,
                                            bsz, seq)
        y_s = _s5(u3, bmat, coef_a, coef_s, ptab, cmat, ssm_d[l].reshape(1, SSM_WIDTH),
                  ssm_w_glu[l].astype(BF16), bsz, seq)
        o_t = _moba_attn(qt, kn, vt, sel)
        x2d = _merge(x2d, norm1, y_s, y_r, o_t, w_gates, w_br_ssm[l].astype(BF16), w_br_ret[l].astype(BF16),
                     w_br_att[l].astype(BF16), w_o[l].astype(BF16), bsz, seq)
        x2d = _ffn(x2d, norm2_g[l].reshape(1, D_MODEL), ffn_w_up[l].astype(BF16), ffn_conv_w[l],
                   ffn_conv_b[l].reshape(1, 2 * FFN_HIDDEN), ffn_w_down[l].astype(BF16), seq)
    return x2d.reshape(bsz, seq, D_MODEL)
```

```python
import functools
import math

import numpy as np
import jax
import jax.numpy as jnp
from jax import lax
from jax.experimental import pallas as pl
from jax.experimental.pallas import tpu as pltpu

F32 = jnp.float32
BF16 = jnp.bfloat16

D_MODEL = 1024
SSM_WIDTH = 512
SSM_GROUP = 16
SSM_GROUPS = 32
SSM_STATE = 64
SSM_NSTATE = SSM_GROUPS * SSM_STATE
RET_HEADS = 4
RET_QK_DIM = 64
RET_V_DIM = 128
RET_QK_WIDTH = 256
RET_V_WIDTH = 512
RET_THETA = 10000.0
ATT_HEADS = 8
ATT_HEAD_DIM = 64
ATT_WIDTH = 512
MOBA_BLOCK = 256
MOBA_TOPK = 3
ROPE_THETA = 500000.0
ROPE_HALF = 8
FFN_HIDDEN = 2816
CONV_WIDTH = 3
NORM_EPS = 1e-6
NEG_INF = -1e30

N_MAIN = SSM_WIDTH + 2 * RET_QK_WIDTH + 2 * RET_V_WIDTH
N_ATT = 3 * ATT_WIDTH
N_GATES = 3 * D_MODEL

SUBLANES = 8
LANES = 128
VMEM_LIMIT = 52 * 1024 * 1024

TM_PROJ = 512
LS_S5 = 256
S5_TSUB = LS_S5 // SUBLANES
S5_BLOCKS = 2
S5_BLK_CH = SSM_WIDTH // S5_BLOCKS
S5_BLK_STATES = SSM_NSTATE // S5_BLOCKS
S5_SCAN_COLS = 4
RET_CHUNK = 256
TM_MERGE = 512
TM_FFN = 512
FC_FFN = 2816
HALO = SUBLANES
ATT_HEADS_PER_STEP = 4
ATT_V_ROWS = ATT_HEAD_DIM + 16
ATT_Q_SCALE = math.log2(math.e) * ATT_HEAD_DIM ** -0.5


def _cparams(n_axes):
    return pltpu.CompilerParams(dimension_semantics=("arbitrary",) * n_axes,
                                vmem_limit_bytes=VMEM_LIMIT)


def _sigmoid(x):
    return 1.0 / (1.0 + jnp.exp(-x))


def _rms_rows(x, g):
    ms = jnp.mean(x * x, axis=-1, keepdims=True)
    return x * lax.rsqrt(ms + NORM_EPS) * g


def _s5_prep_kernel(lre_ref, lim_ref, ldt_ref, bre_ref, bim_ref, bmat_ref, ca_ref, cs_ref, pt_ref):
    lre = lre_ref[...]
    lim = lim_ref[...]
    dt = jnp.exp(ldt_ref[...])

    def lam_pow(k):
        mag = jnp.exp(k * lre * dt)
        ang = k * lim * dt
        return mag * jnp.cos(ang), mag * jnp.sin(ang)

    ar, ai = lam_pow(1.0)
    x = ar - 1.0
    den = lre * lre + lim * lim
    f_re = (x * lre + ai * lim) / den
    f_im = (ai * lre - x * lim) / den
    for bk in range(S5_BLOCKS):
        sl = slice(bk * S5_BLK_STATES, (bk + 1) * S5_BLK_STATES)
        bre = bre_ref[bk]
        bim = bim_ref[bk]
        bmat_ref[bk, :, :S5_BLK_STATES] = (f_re[:, sl] * bre - f_im[:, sl] * bim).astype(BF16)
        bmat_ref[bk, :, S5_BLK_STATES:] = (f_re[:, sl] * bim + f_im[:, sl] * bre).astype(BF16)

    tile = (SUBLANES, SSM_NSTATE)
    ca_ref[0:8, :] = jnp.broadcast_to(ar, tile)
    ca_ref[8:16, :] = jnp.broadcast_to(ai, tile)
    row = lax.broadcasted_iota(jnp.int32, tile, 0)
    for idx, k in enumerate((1, 2, 4)):
        pr, pi = lam_pow(float(k * S5_TSUB))
        keep = row >= k
        cs_ref[idx * 16:idx * 16 + 8, :] = jnp.where(keep, jnp.broadcast_to(pr, tile), 0.0)
        cs_ref[idx * 16 + 8:idx * 16 + 16, :] = jnp.where(keep, jnp.broadcast_to(pi, tile), 0.0)
    pr, pi = lam_pow(float(S5_TSUB))
    cs_ref[48:56, :] = jnp.broadcast_to(pr, tile)
    cs_ref[56:64, :] = jnp.broadcast_to(pi, tile)
    steps = (lax.broadcasted_iota(jnp.int32, (S5_TSUB, SSM_NSTATE), 0) + 1).astype(F32)
    pr, pi = lam_pow(steps)
    for t in range(S5_TSUB):
        pt_ref[0, t * SUBLANES:(t + 1) * SUBLANES, :] = jnp.broadcast_to(pr[t:t + 1], tile)
        pt_ref[1, t * SUBLANES:(t + 1) * SUBLANES, :] = jnp.broadcast_to(pi[t:t + 1], tile)


def _s5_prep(lam_re, lam_im, log_dt, b_re, b_im):
    gpb = SSM_GROUPS // S5_BLOCKS
    eye = jnp.eye(gpb, dtype=F32)

    def embed(b):
        b4 = b.reshape(S5_BLOCKS, gpb, SSM_STATE, SSM_GROUP)
        return jnp.einsum('bgph,gk->bghkp', b4, eye).reshape(S5_BLOCKS, S5_BLK_CH, S5_BLK_STATES)

    ldt = jnp.repeat(log_dt, SSM_STATE).reshape(1, SSM_NSTATE)
    return pl.pallas_call(
        _s5_prep_kernel,
        out_shape=(jax.ShapeDtypeStruct((S5_BLOCKS, S5_BLK_CH, 2 * S5_BLK_STATES), BF16),
                   jax.ShapeDtypeStruct((16, SSM_NSTATE), F32),
                   jax.ShapeDtypeStruct((64, SSM_NSTATE), F32),
                   jax.ShapeDtypeStruct((2, LS_S5, SSM_NSTATE), F32)),
        compiler_params=pltpu.CompilerParams(vmem_limit_bytes=VMEM_LIMIT),
        name="s5_prep",
    )(lam_re.reshape(1, SSM_NSTATE), lam_im.reshape(1, SSM_NSTATE), ldt, embed(b_re), embed(b_im))


def _in_proj_kernel(x_ref, g_ref, wm_ref, wat_ref, qg_ref, kg_ref, cos_ref, sin_ref,
                    rcos_ref, rsin_ref, decay_ref, zeta_ref, xi_ref, cdec_ref, gn_ref,
                    bmat_ref, ca_ref, cs_ref, pt_ref, cmat_ref, d_ref, wglu_ref,
                    ys_ref, yr_ref, qt_ref, kn_ref, vt_ref, sel_ref,
                    att_ref, kmean_ref, state_ref, up_ref, st_ref, xb_ref, cin_ref, *, tiles_per_seq):
    tm = x_ref.shape[0]
    tile_in_seq = pl.program_id(0) % tiles_per_seq

    @pl.when(tile_in_seq == 0)
    def _():
        kmean_ref[...] = jnp.zeros(kmean_ref.shape, F32)
        state_ref[...] = jnp.zeros(state_ref.shape, F32)
        cin_ref[...] = jnp.zeros(cin_ref.shape, F32)

    xn = _rms_rows(x_ref[...], g_ref[...]).astype(BF16)
    att_ref[...] = lax.dot_general(wat_ref[...], xn, (((1,), (1,)), ((), ())),
                                   preferred_element_type=F32)
    proj = jnp.dot(xn, wm_ref[...], preferred_element_type=F32)
    q0, k0, v0, g0 = SSM_WIDTH, SSM_WIDTH + RET_QK_WIDTH, SSM_WIDTH + 2 * RET_QK_WIDTH, N_MAIN - RET_V_WIDTH
    for chunk in range(tm // RET_CHUNK):
        rows = slice(chunk * RET_CHUNK, (chunk + 1) * RET_CHUNK)
        yr_ref[rows, :] = _retention_chunk(
            proj[rows, q0:k0], proj[rows, k0:v0], proj[rows, v0:g0], proj[rows, g0:],
            rcos_ref[rows, :], rsin_ref[rows, :], decay_ref, zeta_ref[...], xi_ref[...], cdec_ref[...],
            gn_ref[...], state_ref)
    blocks_per_tile = tm // MOBA_BLOCK
    for blk in range(blocks_per_tile):
        _moba_prep_block(att_ref, slice(blk * MOBA_BLOCK, (blk + 1) * MOBA_BLOCK),
                         tile_in_seq * blocks_per_tile + blk, qg_ref[...], kg_ref[...], cos_ref, sin_ref,
                         qt_ref, kn_ref, vt_ref, sel_ref, kmean_ref)
    nslab = SSM_WIDTH // LANES
    for chunk in range(tm // LS_S5):
        for s in range(SUBLANES):
            for t0 in range(0, S5_TSUB, SUBLANES):
                r = chunk * LS_S5 + s * S5_TSUB + t0
                for c in range(nslab):
                    up_ref[c, pl.ds(chunk * LS_S5 + SUBLANES * t0 + s, SUBLANES, stride=SUBLANES), :] = (
                        proj[r:r + SUBLANES, c * LANES:(c + 1) * LANES])
    for chunk in range(tm // LS_S5):
        rows = slice(chunk * LS_S5, (chunk + 1) * LS_S5)
        u = jnp.concatenate([up_ref[c, rows, :] for c in range(nslab)], axis=1)
        out = _s5_chunk(u, bmat_ref, ca_ref, cs_ref, pt_ref, cmat_ref, d_ref, wglu_ref, st_ref, xb_ref, cin_ref)
        for c in range(nslab):
            ys_ref[c, rows, :] = out[:, c * LANES:(c + 1) * LANES]


def _in_proj(x2d, g, w_main, w_att_t, qn_g, kn_g, tables, gn_g, ret_tables, s5_params, bsz, seq):
    t = x2d.shape[0]
    tm = TM_PROJ
    tps = seq // tm
    nb = seq // MOBA_BLOCK
    cos_t, sin_t = tables
    rcos, rsin, decay, zeta, xi, cdec = ret_tables
    bmat, coef_a, coef_s, ptab, cmat, d_skip, w_glu = s5_params
    const = lambda i: (0, 0)
    const3 = lambda i: (0, 0, 0)
    once = dict(pipeline_mode=pl.Buffered(1))
    return pl.pallas_call(
        functools.partial(_in_proj_kernel, tiles_per_seq=tps),
        grid=(t // tm,),
        in_specs=[
            pl.BlockSpec((tm, D_MODEL), lambda i: (i, 0)),
            pl.BlockSpec((1, D_MODEL), const),
            pl.BlockSpec((D_MODEL, N_MAIN), const, **once),
            pl.BlockSpec((N_ATT, D_MODEL), const, **once),
            pl.BlockSpec((ATT_HEAD_DIM, 1), const),
            pl.BlockSpec((ATT_HEAD_DIM, 1), const),
            pl.BlockSpec((ROPE_HALF, tm), lambda i: (0, i % tps)),
            pl.BlockSpec((ROPE_HALF, tm), lambda i: (0, i % tps)),
            pl.BlockSpec((tm, RET_QK_WIDTH), lambda i: (i % tps, 0)),
            pl.BlockSpec((tm, RET_QK_WIDTH), lambda i: (i % tps, 0)),
            pl.BlockSpec((RET_HEADS, RET_CHUNK, RET_CHUNK), const3, **once),
            pl.BlockSpec((RET_CHUNK, RET_QK_WIDTH), const),
            pl.BlockSpec((RET_CHUNK, RET_QK_WIDTH), const),
            pl.BlockSpec((1, RET_V_WIDTH), const),
            pl.BlockSpec((1, RET_V_WIDTH), const),
            pl.BlockSpec((S5_BLOCKS, S5_BLK_CH, 2 * S5_BLK_STATES), const3, **once),
            pl.BlockSpec((16, SSM_NSTATE), const),
            pl.BlockSpec((64, SSM_NSTATE), const),
            pl.BlockSpec((2, LS_S5, SSM_NSTATE), const3, **once),
            pl.BlockSpec((S5_BLOCKS, 2 * S5_BLK_STATES, S5_BLK_CH), const3, **once),
            pl.BlockSpec((1, SSM_WIDTH), const),
            pl.BlockSpec((SSM_WIDTH, SSM_WIDTH), const),
        ],
        out_specs=(
            pl.BlockSpec((SSM_WIDTH // LANES, tm, LANES), lambda i: (0, i, 0)),
            pl.BlockSpec((tm, RET_V_WIDTH), lambda i: (i, 0)),
            pl.BlockSpec((1, ATT_WIDTH, tm), lambda i: (i // tps, 0, i % tps)),
            pl.BlockSpec((1, ATT_HEADS, tm, ATT_HEAD_DIM), lambda i: (i // tps, 0, i % tps, 0)),
            pl.BlockSpec((1, ATT_HEADS * ATT_V_ROWS, tm), lambda i: (i // tps, 0, i % tps)),
            pl.BlockSpec((1, ATT_HEADS, nb, tm), lambda i: (i // tps, 0, 0, i % tps)),
        ),
        out_shape=(
            jax.ShapeDtypeStruct((SSM_WIDTH // LANES, t, LANES), F32),
            jax.ShapeDtypeStruct((t, RET_V_WIDTH), F32),
            jax.ShapeDtypeStruct((bsz, ATT_WIDTH, seq), BF16),
            jax.ShapeDtypeStruct((bsz, ATT_HEADS, seq, ATT_HEAD_DIM), BF16),
            jax.ShapeDtypeStruct((bsz, ATT_HEADS * ATT_V_ROWS, seq), BF16),
            jax.ShapeDtypeStruct((bsz, ATT_HEADS, nb, seq), F32),
        ),
        scratch_shapes=[pltpu.VMEM((N_ATT, tm), F32),
                        pltpu.VMEM((ATT_HEADS, nb, ATT_HEAD_DIM), F32),
                        pltpu.VMEM((RET_HEADS, RET_QK_DIM, RET_V_DIM), F32),
                        pltpu.VMEM((SSM_WIDTH // LANES, tm, LANES), F32),
                        pltpu.VMEM((LS_S5, 2 * SSM_NSTATE), F32),
                        pltpu.VMEM((LS_S5, 2 * SSM_NSTATE), BF16),
                        pltpu.VMEM((SUBLANES, 2 * SSM_NSTATE), F32)],
        compiler_params=_cparams(1),
        name="in_proj",
    )(x2d, g, w_main, w_att_t, qn_g.reshape(ATT_HEAD_DIM, 1), kn_g.reshape(ATT_HEAD_DIM, 1), cos_t, sin_t,
      rcos, rsin, decay, zeta, xi, cdec, gn_g, bmat, coef_a, coef_s, ptab, cmat, d_skip, w_glu)


def _s5_chunk(u, bmat_ref, ca_ref, cs_ref, pt_ref, cmat_ref, d_ref, wglu_ref, st_ref, xb_ref, cin_ref):
    ls = st_ref.shape[0]
    tsub = ls // SUBLANES
    ub = u.astype(BF16)
    for bk in range(S5_BLOCKS):
        st_ref[:, bk * 2 * S5_BLK_STATES:(bk + 1) * 2 * S5_BLK_STATES] = jnp.dot(
            ub[:, bk * S5_BLK_CH:(bk + 1) * S5_BLK_CH], bmat_ref[bk], preferred_element_type=F32)

    row = lax.broadcasted_iota(jnp.int32, (SUBLANES, LANES), 0)
    tile = (SUBLANES, LANES)

    def columns(cb):
        per_blk = S5_BLK_STATES // LANES
        base = (cb // per_blk) * 2 * S5_BLK_STATES + (cb % per_blk) * LANES
        return (slice(base, base + LANES), slice(base + S5_BLK_STATES, base + S5_BLK_STATES + LANES),
                slice(cb * LANES, (cb + 1) * LANES))

    def cmul(ar, ai, xr, xi):
        return ar * xr - ai * xi, ar * xi + ai * xr

    def scan_columns(cbs):
        sl = [columns(cb) for cb in cbs]
        a = [(ca_ref[0:8, s], ca_ref[8:16, s]) for (_, _, s) in sl]
        first = []
        for re, im, _ in sl:
            first += [st_ref[0:SUBLANES, re], st_ref[0:SUBLANES, im]]

        def local_step(t, carry):
            r0 = t * SUBLANES
            out = []
            for k, (re, im, _) in enumerate(sl):
                pr, pi = cmul(a[k][0], a[k][1], carry[2 * k], carry[2 * k + 1])
                nr = st_ref[pl.ds(r0, SUBLANES), re] + pr
                ni = st_ref[pl.ds(r0, SUBLANES), im] + pi
                st_ref[pl.ds(r0, SUBLANES), re] = nr
                st_ref[pl.ds(r0, SUBLANES), im] = ni
                out += [nr, ni]
            return tuple(out)

        ends = tuple(first)
        for t in range(1, tsub):
            ends = local_step(t, ends)

        entering = []
        for k, (re, im, s) in enumerate(sl):
            fr, fi = ends[2 * k], ends[2 * k + 1]
            gr = jnp.where(row == 0, cin_ref[:, re], pltpu.roll(fr, 1, 0))
            gi = jnp.where(row == 0, cin_ref[:, im], pltpu.roll(fi, 1, 0))
            for idx, kk in enumerate((1, 2, 4)):
                pr, pi = cmul(cs_ref[idx * 16:idx * 16 + 8, s], cs_ref[idx * 16 + 8:idx * 16 + 16, s],
                              pltpu.roll(gr, kk, 0), pltpu.roll(gi, kk, 0))
                gr, gi = gr + pr, gi + pi
            pr, pi = cmul(cs_ref[48:56, s], cs_ref[56:64, s], gr, gi)
            tr, ti = fr + pr, fi + pi
            cin_ref[:, re] = jnp.broadcast_to(tr[SUBLANES - 1:], tile)
            cin_ref[:, im] = jnp.broadcast_to(ti[SUBLANES - 1:], tile)
            entering += [gr, gi]

        def fix_step(tp):
            r16 = tp * 2 * SUBLANES
            for k, (re, im, s) in enumerate(sl):
                halves_r, halves_i = [], []
                for half in range(2):
                    r0 = r16 + half * SUBLANES
                    pr, pi = cmul(pt_ref[0, pl.ds(r0, SUBLANES), s], pt_ref[1, pl.ds(r0, SUBLANES), s],
                                  entering[2 * k], entering[2 * k + 1])
                    halves_r.append(st_ref[pl.ds(r0, SUBLANES), re] + pr)
                    halves_i.append(st_ref[pl.ds(r0, SUBLANES), im] + pi)
                xb_ref[pl.ds(r16, 2 * SUBLANES), re] = jnp.concatenate(halves_r, axis=0).astype(BF16)
                xb_ref[pl.ds(r16, 2 * SUBLANES), im] = jnp.concatenate(halves_i, axis=0).astype(BF16)

        for tp in range(tsub // 2):
            fix_step(tp)

    groups_per_blk = S5_BLK_STATES // LANES // S5_SCAN_COLS
    y_blocks = []
    for bk in range(S5_BLOCKS):
        for grp in range(bk * groups_per_blk, (bk + 1) * groups_per_blk):
            scan_columns(range(grp * S5_SCAN_COLS, (grp + 1) * S5_SCAN_COLS))
        y_blocks.append(jnp.dot(xb_ref[:, bk * 2 * S5_BLK_STATES:(bk + 1) * 2 * S5_BLK_STATES], cmat_ref[bk],
                                preferred_element_type=F32))
    y = jnp.concatenate(y_blocks, axis=1)
    y = y + d_ref[...] * u
    y = 0.5 * y * (1.0 + jnp.tanh(0.7978845608028654 * (y + 0.044715 * (y * y * y))))
    glu = jnp.dot(y.astype(BF16), wglu_ref[...], preferred_element_type=F32)
    return y * _sigmoid(glu)


def _ret_tables(seq):
    c = RET_CHUNK
    half = RET_QK_DIM // 2
    inv = 1.0 / (RET_THETA ** np.linspace(0.0, 1.0, half))
    ang = np.arange(seq)[:, None] * inv[None, :]
    cos, sin = np.cos(ang), np.sin(ang)
    cos_full = np.tile(np.concatenate([cos, cos], axis=1), (1, RET_HEADS))
    sin_sgn = np.tile(np.concatenate([-sin, sin], axis=1), (1, RET_HEADS))
    log_gamma = np.log1p(-np.exp2(-5.0 - np.arange(RET_HEADS)))
    pos = np.arange(c)
    rel = pos[:, None] - pos[None, :]
    decay = np.where(rel >= 0, np.exp(log_gamma[:, None, None] * np.maximum(rel, 0)[None]), 0.0)
    zeta = np.exp(log_gamma[None, :] * (c - 1.0 - pos)[:, None])
    xi = np.exp(log_gamma[None, :] * (pos + 1.0)[:, None])
    zeta = np.repeat(zeta, RET_QK_DIM, axis=1)
    xi = np.repeat(xi, RET_QK_DIM, axis=1)
    cdec = np.repeat(np.exp(log_gamma * c), RET_V_DIM)[None, :]
    as32 = lambda a: jnp.asarray(a, dtype=F32)
    return as32(cos_full), as32(sin_sgn), as32(decay), as32(zeta), as32(xi), as32(cdec)


def _retention_chunk(q_in, k_in, v_in, g, cos, sin, decay_ref, zeta, xi, cdec, gn, state_ref):
    lane = lax.broadcasted_iota(jnp.int32, cos.shape, 1)
    first_half = (lane % RET_QK_DIM) < (RET_QK_DIM // 2)

    def rot(x):
        swapped = jnp.where(first_half, pltpu.roll(x, RET_QK_WIDTH - RET_QK_DIM // 2, 1),
                            pltpu.roll(x, RET_QK_DIM // 2, 1))
        return x * cos + swapped * sin

    q = rot(q_in)
    k = rot(k_in) * (RET_QK_DIM ** -0.5)
    qx = (q * xi).astype(BF16)
    kz = (k * zeta).astype(BF16)
    qb = q.astype(BF16)
    kb = k.astype(BF16)
    vb = v_in.astype(BF16)
    outs = []
    for h in range(RET_HEADS):
        qs = slice(h * RET_QK_DIM, (h + 1) * RET_QK_DIM)
        vs = slice(h * RET_V_DIM, (h + 1) * RET_V_DIM)
        s = lax.dot_general(qb[:, qs], kb[:, qs], (((1,), (1,)), ((), ())),
                            preferred_element_type=F32) * decay_ref[h]
        state = state_ref[h]
        o = (jnp.dot(s.astype(BF16), vb[:, vs], preferred_element_type=F32)
             + jnp.dot(qx[:, qs], state.astype(BF16), preferred_element_type=F32))
        kv = lax.dot_general(kz[:, qs], vb[:, vs], (((0,), (0,)), ((), ())),
                             preferred_element_type=F32)
        state_ref[h] = cdec[:, vs] * state + kv
        mu = jnp.mean(o, axis=-1, keepdims=True)
        oc = o - mu
        var = jnp.mean(oc * oc, axis=-1, keepdims=True)
        on = oc * lax.rsqrt(var + NORM_EPS) * gn[:, vs]
        gh = g[:, vs]
        outs.append(gh * _sigmoid(gh) * on)
    return jnp.concatenate(outs, axis=1)


def _moba_tables(seq):
    inv = ROPE_THETA ** (-np.arange(ROPE_HALF) / ROPE_HALF)
    ang = inv[:, None] * np.arange(seq)[None, :]
    return jnp.asarray(np.cos(ang), dtype=F32), jnp.asarray(np.sin(ang), dtype=F32)


def _moba_prep_block(att_ref, cols, i_blk, qg, kg, cos_ref, sin_ref, qt_ref, kn_ref, vt_ref, sel_ref, kmean_ref):
    nb = kmean_ref.shape[1]
    cos = cos_ref[:, cols]
    sin = sin_ref[:, cols]

    def norm_rot(x, g):
        ms = jnp.mean(x * x, axis=0, keepdims=True)
        xn = x * lax.rsqrt(ms + NORM_EPS) * g
        x1 = xn[0:ROPE_HALF]
        x2 = xn[ROPE_HALF:2 * ROPE_HALF]
        return jnp.concatenate([x1 * cos - x2 * sin, x1 * sin + x2 * cos, xn[2 * ROPE_HALF:]], axis=0)

    def split(a):
        hi = a.astype(BF16)
        return hi, (a - hi.astype(F32)).astype(BF16)

    row = lax.broadcasted_iota(jnp.int32, (nb, MOBA_BLOCK), 0)
    past = row < i_blk
    for h in range(ATT_HEADS):
        hs = slice(h * ATT_HEAD_DIM, (h + 1) * ATT_HEAD_DIM)
        q = norm_rot(att_ref[hs, cols], qg)
        k = norm_rot(att_ref[ATT_WIDTH + h * ATT_HEAD_DIM:ATT_WIDTH + (h + 1) * ATT_HEAD_DIM, cols], kg)
        kn = k.T
        kn_ref[0, h, cols, :] = kn.astype(BF16)
        qt_ref[0, hs, cols] = (q * ATT_Q_SCALE).astype(BF16)
        v0 = 2 * ATT_WIDTH + h * ATT_HEAD_DIM
        vt_ref[0, h * ATT_V_ROWS:h * ATT_V_ROWS + ATT_HEAD_DIM, cols] = att_ref[v0:v0 + ATT_HEAD_DIM, cols].astype(BF16)
        vt_ref[0, h * ATT_V_ROWS + ATT_HEAD_DIM:(h + 1) * ATT_V_ROWS, cols] = jnp.ones(
            (ATT_V_ROWS - ATT_HEAD_DIM, MOBA_BLOCK), BF16)
        kmean_ref[h, pl.ds(i_blk, 1), :] = jnp.mean(kn, axis=0, keepdims=True)
        mh, ml = split(kmean_ref[h])
        qh, ql = split(q)
        gate = (jnp.dot(mh, qh, preferred_element_type=F32) + jnp.dot(mh, ql, preferred_element_type=F32)
                + jnp.dot(ml, qh, preferred_element_type=F32))
        gate = jnp.where(past, gate, NEG_INF)
        beaten = jnp.zeros(gate.shape, F32)
        for j in range(nb):
            gj = jnp.broadcast_to(gate[j:j + 1, :], gate.shape)
            ahead = jnp.where(gj > gate, 1.0, jnp.where(gj == gate, jnp.where(row > j, 1.0, 0.0), 0.0))
            beaten = beaten + ahead
        sel_ref[0, h, :, cols] = jnp.where(past, jnp.where(beaten < MOBA_TOPK, 1.0, 0.0), 0.0)


def _moba_attn_kernel(qt_ref, k_ref, vt_ref, sel_ref, o_ref, m_ref, acc_ref, s_ref, pv_ref, mb_ref):
    blk = MOBA_BLOCK
    hd = ATT_HEAD_DIM
    nb = sel_ref.shape[2]
    heads = k_ref.shape[1]
    vrows = ATT_V_ROWS
    kpos = lax.broadcasted_iota(jnp.int32, (blk, blk), 0)
    qpos = lax.broadcasted_iota(jnp.int32, (blk, blk), 1)
    causal = kpos <= qpos

    def q_block(i, carry):
        q0 = pl.multiple_of(i * blk, blk)

        def scores(h, j0):
            return jnp.dot(k_ref[0, h, pl.ds(j0, blk), :], qt_ref[0, h * hd:(h + 1) * hd, pl.ds(q0, blk)],
                           preferred_element_type=F32)

        def values(h, j0):
            return vt_ref[0, h * vrows:(h + 1) * vrows, pl.ds(j0, blk)]

        def local_softmax(s, m_blk, picked):
            shift = m_blk if picked is None else jnp.where(picked, m_blk, -NEG_INF)
            p = jnp.exp2((s - shift).astype(BF16))
            return p, (m_blk if picked is None else jnp.where(picked, m_blk, NEG_INF))

        own = [scores(h, q0) for h in range(heads)]
        for h in range(heads):
            s_ref[0, h] = scores(h, 0)
        for h in range(heads):
            s = jnp.where(causal, own[h], NEG_INF)
            p, m = local_softmax(s, jnp.max(s, axis=0, keepdims=True), None)
            mb_ref[1, h] = m
            pv_ref[1, h] = jnp.dot(values(h, q0), p, preferred_element_type=F32)
            m_ref[h] = jnp.full((1, blk), NEG_INF, F32)
            acc_ref[h] = jnp.zeros((vrows, blk), F32)

        def fold(slot):
            for h in range(heads):
                m_blk = mb_ref[slot, h]
                m_old = m_ref[h]
                m_new = jnp.maximum(m_old, m_blk)
                m_ref[h] = m_new
                acc_ref[h] = jnp.exp2(m_old - m_new) * acc_ref[h] + jnp.exp2(m_blk - m_new) * pv_ref[slot, h]

        def step(j, cur, nxt, live):
            jn0 = pl.multiple_of(jnp.minimum(j + 1, nb - 1) * blk, blk)
            for h in range(heads):
                s_ref[nxt, h] = scores(h, jn0)
            jc = jnp.minimum(j, nb - 1)
            jc0 = pl.multiple_of(jc * blk, blk)
            for h in range(heads):
                picked = jnp.logical_and(sel_ref[0, h, pl.ds(jc, 1), pl.ds(q0, blk)] > 0.5, live)
                s = s_ref[cur, h]
                p, m = local_softmax(s, jnp.max(s, axis=0, keepdims=True), picked)
                pv_ref[cur, h] = jnp.dot(values(h, jc0), p, preferred_element_type=F32)
                mb_ref[cur, h] = m
            fold(nxt)

        def block_pair(t, carry):
            step(2 * t, 0, 1, True)
            step(2 * t + 1, 1, 0, 2 * t + 1 < i)
            return carry

        lax.fori_loop(0, (i + 1) // 2, block_pair, 0)
        fold(1)
        for h in range(heads):
            acc = acc_ref[h]
            o_ref[0, h * hd:(h + 1) * hd, pl.ds(q0, blk)] = acc[:hd] / acc[hd:hd + 1]
        return carry

    lax.fori_loop(0, nb, q_block, 0)


def _moba_attn(qt, kn, vt, sel):
    bsz, _, seq = qt.shape
    nb = seq // MOBA_BLOCK
    hg = ATT_HEADS_PER_STEP
    return pl.pallas_call(
        _moba_attn_kernel,
        grid=(bsz, ATT_HEADS // hg),
        in_specs=[
            pl.BlockSpec((1, hg * ATT_HEAD_DIM, seq), lambda b, h: (b, h, 0)),
            pl.BlockSpec((1, hg, seq, ATT_HEAD_DIM), lambda b, h: (b, h, 0, 0)),
            pl.BlockSpec((1, hg * ATT_V_ROWS, seq), lambda b, h: (b, h, 0)),
            pl.BlockSpec((1, hg, nb, seq), lambda b, h: (b, h, 0, 0)),
        ],
        out_specs=pl.BlockSpec((1, hg * ATT_HEAD_DIM, seq), lambda b, h: (b, h, 0)),
        out_shape=jax.ShapeDtypeStruct((bsz, ATT_WIDTH, seq), F32),
        scratch_shapes=[pltpu.VMEM((hg, 1, MOBA_BLOCK), F32),
                        pltpu.VMEM((hg, ATT_V_ROWS, MOBA_BLOCK), F32),
                        pltpu.VMEM((2, hg, MOBA_BLOCK, MOBA_BLOCK), F32),
                        pltpu.VMEM((2, hg, ATT_V_ROWS, MOBA_BLOCK), F32),
                        pltpu.VMEM((2, hg, 1, MOBA_BLOCK), F32)],
        compiler_params=_cparams(2),
        name="moba_attn",
    )(qt, kn, vt, sel)


def _merge_kernel(x_ref, g_ref, ys_ref, yr_ref, at_ref, wg_ref, ws_ref, wr_ref, wa_ref, wo_ref, o_ref,
                  ysp_ref):
    x = x_ref[...]
    tm = x.shape[0]
    xn = _rms_rows(x, g_ref[...]).astype(BF16)
    for chunk in range(tm // LS_S5):
        for s in range(SUBLANES):
            for t0 in range(0, S5_TSUB, SUBLANES):
                r = chunk * LS_S5 + s * S5_TSUB + t0
                for c in range(SSM_WIDTH // LANES):
                    ysp_ref[r:r + SUBLANES, c * LANES:(c + 1) * LANES] = ys_ref[
                        c, pl.ds(chunk * LS_S5 + SUBLANES * t0 + s, SUBLANES, stride=SUBLANES), :]

    def gate(k):
        return _sigmoid(jnp.dot(xn, wg_ref[:, k * D_MODEL:(k + 1) * D_MODEL], preferred_element_type=F32))

    merged = gate(0) * jnp.dot(ysp_ref[...].astype(BF16), ws_ref[...], preferred_element_type=F32)
    merged += gate(1) * jnp.dot(yr_ref[...].astype(BF16), wr_ref[...], preferred_element_type=F32)
    merged += gate(2) * lax.dot_general(at_ref[0].astype(BF16), wa_ref[...], (((0,), (0,)), ((), ())),
                                        preferred_element_type=F32)
    o_ref[...] = x + jnp.dot(merged.astype(BF16), wo_ref[...], preferred_element_type=F32)


def _merge(x2d, g, y_s, y_r, o_t, w_gates, w_s, w_r, w_a, w_o, bsz, seq):
    tm = TM_MERGE
    per_seq = seq // tm
    row = lambda i: (i, 0)
    const = lambda i: (0, 0)
    return pl.pallas_call(
        _merge_kernel,
        grid=(bsz * per_seq,),
        in_specs=[
            pl.BlockSpec((tm, D_MODEL), row),
            pl.BlockSpec((1, D_MODEL), const),
            pl.BlockSpec((SSM_WIDTH // LANES, tm, LANES), lambda i: (0, i, 0)),
            pl.BlockSpec((tm, RET_V_WIDTH), row),
            pl.BlockSpec((1, ATT_WIDTH, tm), lambda i: (i // per_seq, 0, i % per_seq)),
            pl.BlockSpec((D_MODEL, N_GATES), const),
            pl.BlockSpec((SSM_WIDTH, D_MODEL), const),
            pl.BlockSpec((RET_V_WIDTH, D_MODEL), const),
            pl.BlockSpec((ATT_WIDTH, D_MODEL), const),
            pl.BlockSpec((D_MODEL, D_MODEL), const),
        ],
        out_specs=pl.BlockSpec((tm, D_MODEL), row),
        out_shape=jax.ShapeDtypeStruct(x2d.shape, F32),
        scratch_shapes=[pltpu.VMEM((tm, SSM_WIDTH), F32)],
        compiler_params=_cparams(1),
        name="merge",
    )(x2d, g, y_s, y_r, o_t, w_gates, w_s, w_r, w_a, w_o)


def _ffn_kernel(x_ref, halo_ref, g_ref, wg_ref, wu_ref, cwg_ref, cwu_ref, cbg_ref, cbu_ref,
                wd_ref, o_ref, xn_ref, acc_ref, *, tiles_per_seq):
    i = pl.program_id(0)
    c = pl.program_id(1)
    tm = x_ref.shape[0]

    @pl.when(c == 0)
    def _():
        g = g_ref[...]
        xn_ref[0:HALO, :] = _rms_rows(halo_ref[...], g).astype(BF16)
        xn_ref[HALO:, :] = _rms_rows(x_ref[...], g).astype(BF16)
        acc_ref[...] = jnp.zeros(acc_ref.shape, F32)

    inside_seq = i % tiles_per_seq != 0

    def conv(w_ref, cw_ref, cb_ref):
        h = jnp.dot(xn_ref[...], w_ref[...], preferred_element_type=F32)
        h = jnp.concatenate([jnp.where(inside_seq, h[:HALO], 0.0), h[HALO:]], axis=0)
        cw = cw_ref[...]
        return (cb_ref[...] + h[HALO - 2:tm + HALO - 2] * cw[0:1] + h[HALO - 1:tm + HALO - 1] * cw[1:2]
                + h[HALO:] * cw[2:3])

    hg = conv(wg_ref, cwg_ref, cbg_ref)
    hu = conv(wu_ref, cwu_ref, cbu_ref)
    act = (hg * _sigmoid(hg) * hu).astype(BF16)
    acc_ref[...] += jnp.dot(act, wd_ref[...], preferred_element_type=F32)

    @pl.when(c == pl.num_programs(1) - 1)
    def _():
        o_ref[...] = x_ref[...] + acc_ref[...]


def _ffn(x2d, g, w_up, conv_w, conv_b, w_down, seq):
    t = x2d.shape[0]
    tm, fc = TM_FFN, FC_FFN
    nfc = FFN_HIDDEN // fc
    halo_blocks = tm // HALO
    wmode = dict(pipeline_mode=pl.Buffered(1)) if nfc == 1 else {}
    return pl.pallas_call(
        functools.partial(_ffn_kernel, tiles_per_seq=seq // tm),
        grid=(t // tm, nfc),
        in_specs=[
            pl.BlockSpec((tm, D_MODEL), lambda i, c: (i, 0)),
            pl.BlockSpec((HALO, D_MODEL), lambda i, c: (jnp.maximum(i * halo_blocks - 1, 0), 0)),
            pl.BlockSpec((1, D_MODEL), lambda i, c: (0, 0)),
            pl.BlockSpec((D_MODEL, fc), lambda i, c: (0, c), **wmode),
            pl.BlockSpec((D_MODEL, fc), lambda i, c: (0, c + nfc), **wmode),
            pl.BlockSpec((CONV_WIDTH, fc), lambda i, c: (0, c)),
            pl.BlockSpec((CONV_WIDTH, fc), lambda i, c: (0, c + nfc)),
            pl.BlockSpec((1, fc), lambda i, c: (0, c)),
            pl.BlockSpec((1, fc), lambda i, c: (0, c + nfc)),
            pl.BlockSpec((fc, D_MODEL), lambda i, c: (c, 0), **wmode),
        ],
        out_specs=pl.BlockSpec((tm, D_MODEL), lambda i, c: (i, 0)),
        out_shape=jax.ShapeDtypeStruct(x2d.shape, F32),
        scratch_shapes=[pltpu.VMEM((tm + HALO, D_MODEL), BF16), pltpu.VMEM((tm, D_MODEL), F32)],
        compiler_params=_cparams(2),
        name="ffn",
    )(x2d, x2d, g, w_up, w_up, conv_w, conv_w, conv_b, conv_b, w_down)


def kernel(x, norm1_g, w_in, ssm_lambda_re, ssm_lambda_im, ssm_log_dt, ssm_b_re, ssm_b_im, ssm_c_re, ssm_c_im, ssm_d, ssm_w_glu, ret_gn_g, attn_qn_g, attn_kn_g, w_br_ssm, w_br_ret, w_br_att, w_o, norm2_g, ffn_w_up, ffn_conv_w, ffn_conv_b, ffn_w_down):
    bsz, seq, _ = x.shape
    depth = w_in.shape[0]
    assert seq % MOBA_BLOCK == 0 and seq % TM_PROJ == 0 and seq % TM_FFN == 0
    ret_tables = _ret_tables(seq)
    moba_tables = _moba_tables(seq)
    gpb = SSM_GROUPS // S5_BLOCKS
    eye = jnp.eye(gpb, dtype=F32)

    def embed_c(c):
        c4 = c.reshape(S5_BLOCKS, gpb, SSM_GROUP, SSM_STATE)
        return jnp.einsum('bghp,gk->bgpkh', c4, eye).reshape(S5_BLOCKS, S5_BLK_STATES, S5_BLK_CH)

    x2d = x.reshape(bsz * seq, D_MODEL)
    for l in range(depth):
        w = w_in[l]
        w_main = w[:, :N_MAIN].astype(BF16)
        w_att_t = w[:, N_MAIN:N_MAIN + N_ATT].T.astype(BF16)
        w_gates = w[:, N_MAIN + N_ATT:].astype(BF16)
        norm1 = norm1_g[l].reshape(1, D_MODEL)
        cmat = jnp.concatenate([embed_c(ssm_c_re[l]), -embed_c(ssm_c_im[l])], axis=1).astype(BF16)

        bmat, coef_a, coef_s, ptab = _s5_prep(ssm_lambda_re[l], ssm_lambda_im[l], ssm_log_dt[l],
                                              ssm_b_re[l], ssm_b_im[l])
        s5_params = (bmat, coef_a, coef_s, ptab, cmat, ssm_d[l].reshape(1, SSM_WIDTH), ssm_w_glu[l].astype(BF16))
        y_s, y_r, qt, kn, vt, sel = _in_proj(x2d, norm1, w_main, w_att_t, attn_qn_g[l], attn_kn_g[l],
                                             moba_tables, ret_gn_g[l].reshape(1, RET_V_WIDTH), ret_tables,
                                             s5_params, bsz, seq)
        o_t = _moba_attn(qt, kn, vt, sel)
        x2d = _merge(x2d, norm1, y_s, y_r, o_t, w_gates, w_br_ssm[l].astype(BF16), w_br_ret[l].astype(BF16),
                     w_br_att[l].astype(BF16), w_o[l].astype(BF16), bsz, seq)
        x2d = _ffn(x2d, norm2_g[l].reshape(1, D_MODEL), ffn_w_up[l].astype(BF16), ffn_conv_w[l],
                   ffn_conv_b[l].reshape(1, 2 * FFN_HIDDEN), ffn_w_down[l].astype(BF16), seq)
    return x2d.reshape(bsz, seq, D_MODEL)
```

```python
import functools
import math

import numpy as np
import jax
import jax.numpy as jnp
from jax import lax
from jax.experimental import pallas as pl
from jax.experimental.pallas import tpu as pltpu

F32 = jnp.float32
BF16 = jnp.bfloat16

D_MODEL = 1024
SSM_WIDTH = 512
SSM_GROUP = 16
SSM_GROUPS = 32
SSM_STATE = 64
SSM_NSTATE = SSM_GROUPS * SSM_STATE
RET_HEADS = 4
RET_QK_DIM = 64
RET_V_DIM = 128
RET_QK_WIDTH = 256
RET_V_WIDTH = 512
RET_THETA = 10000.0
ATT_HEADS = 8
ATT_HEAD_DIM = 64
ATT_WIDTH = 512
MOBA_BLOCK = 256
MOBA_TOPK = 3
ROPE_THETA = 500000.0
ROPE_HALF = 8
FFN_HIDDEN = 2816
CONV_WIDTH = 3
NORM_EPS = 1e-6
NEG_INF = -1e30

N_MAIN = SSM_WIDTH + 2 * RET_QK_WIDTH + 2 * RET_V_WIDTH
N_ATT = 3 * ATT_WIDTH
N_GATES = 3 * D_MODEL

SUBLANES = 8
LANES = 128
VMEM_LIMIT = 52 * 1024 * 1024

TM_PROJ = 512
LS_S5 = 256
S5_TSUB = LS_S5 // SUBLANES
S5_BLOCKS = 2
S5_BLK_CH = SSM_WIDTH // S5_BLOCKS
S5_BLK_STATES = SSM_NSTATE // S5_BLOCKS
S5_SCAN_COLS = 4
RET_CHUNK = 256
TM_MERGE = 512
TM_FFN = 512
FC_FFN = 2816
HALO = SUBLANES
ATT_HEADS_PER_STEP = 4
ATT_V_ROWS = ATT_HEAD_DIM + 16
ATT_Q_SCALE = math.log2(math.e) * ATT_HEAD_DIM ** -0.5


def _cparams(n_axes):
    return pltpu.CompilerParams(dimension_semantics=("arbitrary",) * n_axes,
                                vmem_limit_bytes=VMEM_LIMIT)


def _sigmoid(x):
    return 1.0 / (1.0 + jnp.exp(-x))


def _rms_rows(x, g):
    ms = jnp.mean(x * x, axis=-1, keepdims=True)
    return x * lax.rsqrt(ms + NORM_EPS) * g


def _s5_prep_kernel(lre_ref, lim_ref, ldt_ref, bre_ref, bim_ref, bmat_ref, ca_ref, cs_ref, pt_ref):
    lre = lre_ref[...]
    lim = lim_ref[...]
    dt = jnp.exp(ldt_ref[...])

    def lam_pow(k):
        mag = jnp.exp(k * lre * dt)
        ang = k * lim * dt
        return mag * jnp.cos(ang), mag * jnp.sin(ang)

    ar, ai = lam_pow(1.0)
    x = ar - 1.0
    den = lre * lre + lim * lim
    f_re = (x * lre + ai * lim) / den
    f_im = (ai * lre - x * lim) / den
    for bk in range(S5_BLOCKS):
        sl = slice(bk * S5_BLK_STATES, (bk + 1) * S5_BLK_STATES)
        bre = bre_ref[bk]
        bim = bim_ref[bk]
        bmat_ref[bk, :, :S5_BLK_STATES] = (f_re[:, sl] * bre - f_im[:, sl] * bim).astype(BF16)
        bmat_ref[bk, :, S5_BLK_STATES:] = (f_re[:, sl] * bim + f_im[:, sl] * bre).astype(BF16)

    tile = (SUBLANES, SSM_NSTATE)
    ca_ref[0:8, :] = jnp.broadcast_to(ar, tile)
    ca_ref[8:16, :] = jnp.broadcast_to(ai, tile)
    row = lax.broadcasted_iota(jnp.int32, tile, 0)
    for idx, k in enumerate((1, 2, 4)):
        pr, pi = lam_pow(float(k * S5_TSUB))
        keep = row >= k
        cs_ref[idx * 16:idx * 16 + 8, :] = jnp.where(keep, jnp.broadcast_to(pr, tile), 0.0)
        cs_ref[idx * 16 + 8:idx * 16 + 16, :] = jnp.where(keep, jnp.broadcast_to(pi, tile), 0.0)
    pr, pi = lam_pow(float(S5_TSUB))
    cs_ref[48:56, :] = jnp.broadcast_to(pr, tile)
    cs_ref[56:64, :] = jnp.broadcast_to(pi, tile)
    steps = (lax.broadcasted_iota(jnp.int32, (S5_TSUB, SSM_NSTATE), 0) + 1).astype(F32)
    pr, pi = lam_pow(steps)
    for t in range(S5_TSUB):
        pt_ref[0, t * SUBLANES:(t + 1) * SUBLANES, :] = jnp.broadcast_to(pr[t:t + 1], tile)
        pt_ref[1, t * SUBLANES:(t + 1) * SUBLANES, :] = jnp.broadcast_to(pi[t:t + 1], tile)


def _s5_embed(b, c):
    depth = b.shape[0]
    gpb = SSM_GROUPS // S5_BLOCKS
    eye = jnp.eye(gpb, dtype=F32)
    b5 = b.reshape(depth, S5_BLOCKS, gpb, SSM_STATE, SSM_GROUP)
    c5 = c.reshape(depth, S5_BLOCKS, gpb, SSM_GROUP, SSM_STATE)
    return (jnp.einsum('lbgph,gk->lbghkp', b5, eye).reshape(depth, S5_BLOCKS, S5_BLK_CH, S5_BLK_STATES),
            jnp.einsum('lbghp,gk->lbgpkh', c5, eye).reshape(depth, S5_BLOCKS, S5_BLK_STATES, S5_BLK_CH))


def _s5_prep(lam_re, lam_im, log_dt, bre_emb, bim_emb):
    ldt = jnp.repeat(log_dt, SSM_STATE).reshape(1, SSM_NSTATE)
    return pl.pallas_call(
        _s5_prep_kernel,
        out_shape=(jax.ShapeDtypeStruct((S5_BLOCKS, S5_BLK_CH, 2 * S5_BLK_STATES), BF16),
                   jax.ShapeDtypeStruct((16, SSM_NSTATE), F32),
                   jax.ShapeDtypeStruct((64, SSM_NSTATE), F32),
                   jax.ShapeDtypeStruct((2, LS_S5, SSM_NSTATE), F32)),
        compiler_params=pltpu.CompilerParams(vmem_limit_bytes=VMEM_LIMIT),
        name="s5_prep",
    )(lam_re.reshape(1, SSM_NSTATE), lam_im.reshape(1, SSM_NSTATE), ldt, bre_emb, bim_emb)


def _in_proj_kernel(x_ref, g_ref, wm_ref, wat_ref, qg_ref, kg_ref, cos_ref, sin_ref,
                    rcos_ref, rsin_ref, decay_ref, zeta_ref, xi_ref, cdec_ref, gn_ref,
                    bmat_ref, ca_ref, cs_ref, pt_ref, cmat_ref, d_ref, wglu_ref,
                    ys_ref, yr_ref, qt_ref, kn_ref, vt_ref, sel_ref,
                    att_ref, kmean_ref, state_ref, up_ref, st_ref, xb_ref, cin_ref, *, tiles_per_seq):
    tm = x_ref.shape[0]
    tile_in_seq = pl.program_id(0) % tiles_per_seq

    @pl.when(tile_in_seq == 0)
    def _():
        kmean_ref[...] = jnp.zeros(kmean_ref.shape, F32)
        state_ref[...] = jnp.zeros(state_ref.shape, F32)
        cin_ref[...] = jnp.zeros(cin_ref.shape, F32)

    xn = _rms_rows(x_ref[...], g_ref[...]).astype(BF16)
    att_ref[...] = lax.dot_general(wat_ref[...], xn, (((1,), (1,)), ((), ())),
                                   preferred_element_type=F32)
    proj = jnp.dot(xn, wm_ref[...], preferred_element_type=F32)
    q0, k0, v0, g0 = SSM_WIDTH, SSM_WIDTH + RET_QK_WIDTH, SSM_WIDTH + 2 * RET_QK_WIDTH, N_MAIN - RET_V_WIDTH
    for chunk in range(tm // RET_CHUNK):
        rows = slice(chunk * RET_CHUNK, (chunk + 1) * RET_CHUNK)
        yr_ref[rows, :] = _retention_chunk(
            proj[rows, q0:k0], proj[rows, k0:v0], proj[rows, v0:g0], proj[rows, g0:],
            rcos_ref[rows, :], rsin_ref[rows, :], decay_ref, zeta_ref[...], xi_ref[...], cdec_ref[...],
            gn_ref[...], state_ref)
    blocks_per_tile = tm // MOBA_BLOCK
    for blk in range(blocks_per_tile):
        _moba_prep_block(att_ref, slice(blk * MOBA_BLOCK, (blk + 1) * MOBA_BLOCK),
                         tile_in_seq * blocks_per_tile + blk, qg_ref[...], kg_ref[...], cos_ref, sin_ref,
                         qt_ref, kn_ref, vt_ref, sel_ref, kmean_ref)
    nslab = SSM_WIDTH // LANES
    for chunk in range(tm // LS_S5):
        for s in range(SUBLANES):
            for t0 in range(0, S5_TSUB, SUBLANES):
                r = chunk * LS_S5 + s * S5_TSUB + t0
                for c in range(nslab):
                    up_ref[c, pl.ds(chunk * LS_S5 + SUBLANES * t0 + s, SUBLANES, stride=SUBLANES), :] = (
                        proj[r:r + SUBLANES, c * LANES:(c + 1) * LANES])
    for chunk in range(tm // LS_S5):
        rows = slice(chunk * LS_S5, (chunk + 1) * LS_S5)
        u = jnp.concatenate([up_ref[c, rows, :] for c in range(nslab)], axis=1)
        out = _s5_chunk(u, bmat_ref, ca_ref, cs_ref, pt_ref, cmat_ref, d_ref, wglu_ref, st_ref, xb_ref, cin_ref)
        for c in range(nslab):
            ys_ref[c, rows, :] = out[:, c * LANES:(c + 1) * LANES]


def _in_proj(x2d, g, w_main, w_att_t, qn_g, kn_g, tables, gn_g, ret_tables, s5_params, bsz, seq):
    t = x2d.shape[0]
    tm = TM_PROJ
    tps = seq // tm
    nb = seq // MOBA_BLOCK
    cos_t, sin_t = tables
    rcos, rsin, decay, zeta, xi, cdec = ret_tables
    bmat, coef_a, coef_s, ptab, cmat, d_skip, w_glu = s5_params
    const = lambda i: (0, 0)
    const3 = lambda i: (0, 0, 0)
    once = dict(pipeline_mode=pl.Buffered(1))
    return pl.pallas_call(
        functools.partial(_in_proj_kernel, tiles_per_seq=tps),
        grid=(t // tm,),
        in_specs=[
            pl.BlockSpec((tm, D_MODEL), lambda i: (i, 0)),
            pl.BlockSpec((1, D_MODEL), const),
            pl.BlockSpec((D_MODEL, N_MAIN), const, **once),
            pl.BlockSpec((N_ATT, D_MODEL), const, **once),
            pl.BlockSpec((ATT_HEAD_DIM, 1), const),
            pl.BlockSpec((ATT_HEAD_DIM, 1), const),
            pl.BlockSpec((ROPE_HALF, tm), lambda i: (0, i % tps)),
            pl.BlockSpec((ROPE_HALF, tm), lambda i: (0, i % tps)),
            pl.BlockSpec((tm, RET_QK_WIDTH), lambda i: (i % tps, 0)),
            pl.BlockSpec((tm, RET_QK_WIDTH), lambda i: (i % tps, 0)),
            pl.BlockSpec((RET_HEADS, RET_CHUNK, RET_CHUNK), const3, **once),
            pl.BlockSpec((RET_CHUNK, RET_QK_WIDTH), const),
            pl.BlockSpec((RET_CHUNK, RET_QK_WIDTH), const),
            pl.BlockSpec((1, RET_V_WIDTH), const),
            pl.BlockSpec((1, RET_V_WIDTH), const),
            pl.BlockSpec((S5_BLOCKS, S5_BLK_CH, 2 * S5_BLK_STATES), const3, **once),
            pl.BlockSpec((16, SSM_NSTATE), const),
            pl.BlockSpec((64, SSM_NSTATE), const),
            pl.BlockSpec((2, LS_S5, SSM_NSTATE), const3, **once),
            pl.BlockSpec((S5_BLOCKS, 2 * S5_BLK_STATES, S5_BLK_CH), const3, **once),
            pl.BlockSpec((1, SSM_WIDTH), const),
            pl.BlockSpec((SSM_WIDTH, SSM_WIDTH), const),
        ],
        out_specs=(
            pl.BlockSpec((SSM_WIDTH // LANES, tm, LANES), lambda i: (0, i, 0)),
            pl.BlockSpec((tm, RET_V_WIDTH), lambda i: (i, 0)),
            pl.BlockSpec((1, ATT_WIDTH, tm), lambda i: (i // tps, 0, i % tps)),
            pl.BlockSpec((1, ATT_HEADS, tm, ATT_HEAD_DIM), lambda i: (i // tps, 0, i % tps, 0)),
            pl.BlockSpec((1, ATT_HEADS * ATT_V_ROWS, tm), lambda i: (i // tps, 0, i % tps)),
            pl.BlockSpec((1, ATT_HEADS, nb, tm), lambda i: (i // tps, 0, 0, i % tps)),
        ),
        out_shape=(
            jax.ShapeDtypeStruct((SSM_WIDTH // LANES, t, LANES), F32),
            jax.ShapeDtypeStruct((t, RET_V_WIDTH), F32),
            jax.ShapeDtypeStruct((bsz, ATT_WIDTH, seq), BF16),
            jax.ShapeDtypeStruct((bsz, ATT_HEADS, seq, ATT_HEAD_DIM), BF16),
            jax.ShapeDtypeStruct((bsz, ATT_HEADS * ATT_V_ROWS, seq), BF16),
            jax.ShapeDtypeStruct((bsz, ATT_HEADS, nb, seq), F32),
        ),
        scratch_shapes=[pltpu.VMEM((N_ATT, tm), F32),
                        pltpu.VMEM((ATT_HEADS, nb, ATT_HEAD_DIM), F32),
                        pltpu.VMEM((RET_HEADS, RET_QK_DIM, RET_V_DIM), F32),
                        pltpu.VMEM((SSM_WIDTH // LANES, tm, LANES), F32),
                        pltpu.VMEM((LS_S5, 2 * SSM_NSTATE), F32),
                        pltpu.VMEM((LS_S5, 2 * SSM_NSTATE), BF16),
                        pltpu.VMEM((SUBLANES, 2 * SSM_NSTATE), F32)],
        compiler_params=_cparams(1),
        name="in_proj",
    )(x2d, g, w_main, w_att_t, qn_g.reshape(ATT_HEAD_DIM, 1), kn_g.reshape(ATT_HEAD_DIM, 1), cos_t, sin_t,
      rcos, rsin, decay, zeta, xi, cdec, gn_g, bmat, coef_a, coef_s, ptab, cmat, d_skip, w_glu)


def _s5_chunk(u, bmat_ref, ca_ref, cs_ref, pt_ref, cmat_ref, d_ref, wglu_ref, st_ref, xb_ref, cin_ref):
    ls = st_ref.shape[0]
    tsub = ls // SUBLANES
    ub = u.astype(BF16)
    for bk in range(S5_BLOCKS):
        st_ref[:, bk * 2 * S5_BLK_STATES:(bk + 1) * 2 * S5_BLK_STATES] = jnp.dot(
            ub[:, bk * S5_BLK_CH:(bk + 1) * S5_BLK_CH], bmat_ref[bk], preferred_element_type=F32)

    row = lax.broadcasted_iota(jnp.int32, (SUBLANES, LANES), 0)
    tile = (SUBLANES, LANES)

    def columns(cb):
        per_blk = S5_BLK_STATES // LANES
        base = (cb // per_blk) * 2 * S5_BLK_STATES + (cb % per_blk) * LANES
        return (slice(base, base + LANES), slice(base + S5_BLK_STATES, base + S5_BLK_STATES + LANES),
                slice(cb * LANES, (cb + 1) * LANES))

    def cmul(ar, ai, xr, xi):
        return ar * xr - ai * xi, ar * xi + ai * xr

    def scan_columns(cbs):
        sl = [columns(cb) for cb in cbs]
        a = [(ca_ref[0:8, s], ca_ref[8:16, s]) for (_, _, s) in sl]
        first = []
        for re, im, _ in sl:
            first += [st_ref[0:SUBLANES, re], st_ref[0:SUBLANES, im]]

        def local_step(t, carry):
            r0 = pl.multiple_of(t * SUBLANES, SUBLANES)
            out = []
            for k, (re, im, _) in enumerate(sl):
                pr, pi = cmul(a[k][0], a[k][1], carry[2 * k], carry[2 * k + 1])
                nr = st_ref[pl.ds(r0, SUBLANES), re] + pr
                ni = st_ref[pl.ds(r0, SUBLANES), im] + pi
                st_ref[pl.ds(r0, SUBLANES), re] = nr
                st_ref[pl.ds(r0, SUBLANES), im] = ni
                out += [nr, ni]
            return tuple(out)

        ends = lax.fori_loop(1, tsub, local_step, tuple(first))

        entering = []
        for k, (re, im, s) in enumerate(sl):
            fr, fi = ends[2 * k], ends[2 * k + 1]
            gr = jnp.where(row == 0, cin_ref[:, re], pltpu.roll(fr, 1, 0))
            gi = jnp.where(row == 0, cin_ref[:, im], pltpu.roll(fi, 1, 0))
            for idx, kk in enumerate((1, 2, 4)):
                pr, pi = cmul(cs_ref[idx * 16:idx * 16 + 8, s], cs_ref[idx * 16 + 8:idx * 16 + 16, s],
                              pltpu.roll(gr, kk, 0), pltpu.roll(gi, kk, 0))
                gr, gi = gr + pr, gi + pi
            pr, pi = cmul(cs_ref[48:56, s], cs_ref[56:64, s], gr, gi)
            tr, ti = fr + pr, fi + pi
            cin_ref[:, re] = jnp.broadcast_to(tr[SUBLANES - 1:], tile)
            cin_ref[:, im] = jnp.broadcast_to(ti[SUBLANES - 1:], tile)
            entering += [gr, gi]

        def fix_step(tp):
            r16 = tp * 2 * SUBLANES
            for k, (re, im, s) in enumerate(sl):
                halves_r, halves_i = [], []
                for half in range(2):
                    r0 = r16 + half * SUBLANES
                    pr, pi = cmul(pt_ref[0, pl.ds(r0, SUBLANES), s], pt_ref[1, pl.ds(r0, SUBLANES), s],
                                  entering[2 * k], entering[2 * k + 1])
                    halves_r.append(st_ref[pl.ds(r0, SUBLANES), re] + pr)
                    halves_i.append(st_ref[pl.ds(r0, SUBLANES), im] + pi)
                xb_ref[pl.ds(r16, 2 * SUBLANES), re] = jnp.concatenate(halves_r, axis=0).astype(BF16)
                xb_ref[pl.ds(r16, 2 * SUBLANES), im] = jnp.concatenate(halves_i, axis=0).astype(BF16)

        for tp in range(tsub // 2):
            fix_step(tp)

    groups_per_blk = S5_BLK_STATES // LANES // S5_SCAN_COLS
    y_blocks = []
    for bk in range(S5_BLOCKS):
        for grp in range(bk * groups_per_blk, (bk + 1) * groups_per_blk):
            scan_columns(range(grp * S5_SCAN_COLS, (grp + 1) * S5_SCAN_COLS))
        y_blocks.append(jnp.dot(xb_ref[:, bk * 2 * S5_BLK_STATES:(bk + 1) * 2 * S5_BLK_STATES], cmat_ref[bk],
                                preferred_element_type=F32))
    y = jnp.concatenate(y_blocks, axis=1)
    y = y + d_ref[...] * u
    y = 0.5 * y * (1.0 + jnp.tanh(0.7978845608028654 * (y + 0.044715 * (y * y * y))))
    glu = jnp.dot(y.astype(BF16), wglu_ref[...], preferred_element_type=F32)
    return y * _sigmoid(glu)


def _ret_tables(seq):
    c = RET_CHUNK
    half = RET_QK_DIM // 2
    inv = 1.0 / (RET_THETA ** np.linspace(0.0, 1.0, half))
    ang = np.arange(seq)[:, None] * inv[None, :]
    cos, sin = np.cos(ang), np.sin(ang)
    cos_full = np.tile(np.concatenate([cos, cos], axis=1), (1, RET_HEADS))
    sin_sgn = np.tile(np.concatenate([-sin, sin], axis=1), (1, RET_HEADS))
    log_gamma = np.log1p(-np.exp2(-5.0 - np.arange(RET_HEADS)))
    pos = np.arange(c)
    rel = pos[:, None] - pos[None, :]
    decay = np.where(rel >= 0, np.exp(log_gamma[:, None, None] * np.maximum(rel, 0)[None]), 0.0)
    zeta = np.exp(log_gamma[None, :] * (c - 1.0 - pos)[:, None])
    xi = np.exp(log_gamma[None, :] * (pos + 1.0)[:, None])
    zeta = np.repeat(zeta, RET_QK_DIM, axis=1)
    xi = np.repeat(xi, RET_QK_DIM, axis=1)
    cdec = np.repeat(np.exp(log_gamma * c), RET_V_DIM)[None, :]
    as32 = lambda a: jnp.asarray(a, dtype=F32)
    return as32(cos_full), as32(sin_sgn), as32(decay), as32(zeta), as32(xi), as32(cdec)


def _retention_chunk(q_in, k_in, v_in, g, cos, sin, decay_ref, zeta, xi, cdec, gn, state_ref):
    lane = lax.broadcasted_iota(jnp.int32, cos.shape, 1)
    first_half = (lane % RET_QK_DIM) < (RET_QK_DIM // 2)

    def rot(x):
        swapped = jnp.where(first_half, pltpu.roll(x, RET_QK_WIDTH - RET_QK_DIM // 2, 1),
                            pltpu.roll(x, RET_QK_DIM // 2, 1))
        return x * cos + swapped * sin

    q = rot(q_in)
    k = rot(k_in) * (RET_QK_DIM ** -0.5)
    qx = (q * xi).astype(BF16)
    kz = (k * zeta).astype(BF16)
    qb = q.astype(BF16)
    kb = k.astype(BF16)
    vb = v_in.astype(BF16)
    outs = []
    for h in range(RET_HEADS):
        qs = slice(h * RET_QK_DIM, (h + 1) * RET_QK_DIM)
        vs = slice(h * RET_V_DIM, (h + 1) * RET_V_DIM)
        s = lax.dot_general(qb[:, qs], kb[:, qs], (((1,), (1,)), ((), ())),
                            preferred_element_type=F32) * decay_ref[h]
        state = state_ref[h]
        o = (jnp.dot(s.astype(BF16), vb[:, vs], preferred_element_type=F32)
             + jnp.dot(qx[:, qs], state.astype(BF16), preferred_element_type=F32))
        kv = lax.dot_general(kz[:, qs], vb[:, vs], (((0,), (0,)), ((), ())),
                             preferred_element_type=F32)
        state_ref[h] = cdec[:, vs] * state + kv
        mu = jnp.mean(o, axis=-1, keepdims=True)
        oc = o - mu
        var = jnp.mean(oc * oc, axis=-1, keepdims=True)
        on = oc * lax.rsqrt(var + NORM_EPS) * gn[:, vs]
        gh = g[:, vs]
        outs.append(gh * _sigmoid(gh) * on)
    return jnp.concatenate(outs, axis=1)


def _moba_tables(seq):
    inv = ROPE_THETA ** (-np.arange(ROPE_HALF) / ROPE_HALF)
    ang = inv[:, None] * np.arange(seq)[None, :]
    return jnp.asarray(np.cos(ang), dtype=F32), jnp.asarray(np.sin(ang), dtype=F32)


def _moba_prep_block(att_ref, cols, i_blk, qg, kg, cos_ref, sin_ref, qt_ref, kn_ref, vt_ref, sel_ref, kmean_ref):
    nb = kmean_ref.shape[1]
    cos = cos_ref[:, cols]
    sin = sin_ref[:, cols]

    def norm_rot(x, g):
        ms = jnp.mean(x * x, axis=0, keepdims=True)
        xn = x * lax.rsqrt(ms + NORM_EPS) * g
        x1 = xn[0:ROPE_HALF]
        x2 = xn[ROPE_HALF:2 * ROPE_HALF]
        return jnp.concatenate([x1 * cos - x2 * sin, x1 * sin + x2 * cos, xn[2 * ROPE_HALF:]], axis=0)

    def split(a):
        hi = a.astype(BF16)
        return hi, (a - hi.astype(F32)).astype(BF16)

    row = lax.broadcasted_iota(jnp.int32, (nb, MOBA_BLOCK), 0)
    past = row < i_blk
    for h in range(ATT_HEADS):
        hs = slice(h * ATT_HEAD_DIM, (h + 1) * ATT_HEAD_DIM)
        q = norm_rot(att_ref[hs, cols], qg)
        k = norm_rot(att_ref[ATT_WIDTH + h * ATT_HEAD_DIM:ATT_WIDTH + (h + 1) * ATT_HEAD_DIM, cols], kg)
        kn = k.T
        kn_ref[0, h, cols, :] = kn.astype(BF16)
        qt_ref[0, hs, cols] = (q * ATT_Q_SCALE).astype(BF16)
        v0 = 2 * ATT_WIDTH + h * ATT_HEAD_DIM
        vt_ref[0, h * ATT_V_ROWS:h * ATT_V_ROWS + ATT_HEAD_DIM, cols] = att_ref[v0:v0 + ATT_HEAD_DIM, cols].astype(BF16)
        vt_ref[0, h * ATT_V_ROWS + ATT_HEAD_DIM:(h + 1) * ATT_V_ROWS, cols] = jnp.ones(
            (ATT_V_ROWS - ATT_HEAD_DIM, MOBA_BLOCK), BF16)
        kmean_ref[h, pl.ds(i_blk, 1), :] = jnp.mean(kn, axis=0, keepdims=True)
        mh, ml = split(kmean_ref[h])
        qh, ql = split(q)
        gate = (jnp.dot(mh, qh, preferred_element_type=F32) + jnp.dot(mh, ql, preferred_element_type=F32)
                + jnp.dot(ml, qh, preferred_element_type=F32))
        gate = jnp.where(past, gate, NEG_INF)
        beaten = jnp.zeros(gate.shape, F32)
        for j in range(nb):
            gj = jnp.broadcast_to(gate[j:j + 1, :], gate.shape)
            ahead = jnp.where(gj > gate, 1.0, jnp.where(gj == gate, jnp.where(row > j, 1.0, 0.0), 0.0))
            beaten = beaten + ahead
        sel_ref[0, h, :, cols] = jnp.where(past, jnp.where(beaten < MOBA_TOPK, 1.0, 0.0), 0.0)


def _moba_attn_kernel(qt_ref, k_ref, vt_ref, sel_ref, o_ref, m_ref, acc_ref, s_ref, pv_ref, mb_ref):
    blk = MOBA_BLOCK
    hd = ATT_HEAD_DIM
    nb = sel_ref.shape[2]
    heads = k_ref.shape[1]
    vrows = ATT_V_ROWS
    kpos = lax.broadcasted_iota(jnp.int32, (blk, blk), 0)
    qpos = lax.broadcasted_iota(jnp.int32, (blk, blk), 1)
    causal = kpos <= qpos

    def q_block(i, carry):
        q0 = pl.multiple_of(i * blk, blk)

        def scores(h, j0):
            return jnp.dot(k_ref[0, h, pl.ds(j0, blk), :], qt_ref[0, h * hd:(h + 1) * hd, pl.ds(q0, blk)],
                           preferred_element_type=F32)

        def values(h, j0):
            return vt_ref[0, h * vrows:(h + 1) * vrows, pl.ds(j0, blk)]

        def local_softmax(s, m_blk, picked):
            shift = m_blk if picked is None else jnp.where(picked, m_blk, -NEG_INF)
            p = jnp.exp2((s - shift).astype(BF16))
            return p, (m_blk if picked is None else jnp.where(picked, m_blk, NEG_INF))

        own = [scores(h, q0) for h in range(heads)]
        for h in range(heads):
            s_ref[0, h] = scores(h, 0)
        for h in range(heads):
            s = jnp.where(causal, own[h], NEG_INF)
            p, m = local_softmax(s, jnp.max(s, axis=0, keepdims=True), None)
            mb_ref[1, h] = m
            pv_ref[1, h] = jnp.dot(values(h, q0), p, preferred_element_type=F32)
            m_ref[h] = jnp.full((1, blk), NEG_INF, F32)
            acc_ref[h] = jnp.zeros((vrows, blk), F32)

        def fold(slot):
            for h in range(heads):
                m_blk = mb_ref[slot, h]
                m_old = m_ref[h]
                m_new = jnp.maximum(m_old, m_blk)
                m_ref[h] = m_new
                acc_ref[h] = jnp.exp2(m_old - m_new) * acc_ref[h] + jnp.exp2(m_blk - m_new) * pv_ref[slot, h]

        def step(j, cur, nxt, live):
            jn0 = pl.multiple_of(jnp.minimum(j + 1, nb - 1) * blk, blk)
            for h in range(heads):
                s_ref[nxt, h] = scores(h, jn0)
            jc = jnp.minimum(j, nb - 1)
            jc0 = pl.multiple_of(jc * blk, blk)
            for h in range(heads):
                picked = jnp.logical_and(sel_ref[0, h, pl.ds(jc, 1), pl.ds(q0, blk)] > 0.5, live)
                s = s_ref[cur, h]
                p, m = local_softmax(s, jnp.max(s, axis=0, keepdims=True), picked)
                pv_ref[cur, h] = jnp.dot(values(h, jc0), p, preferred_element_type=F32)
                mb_ref[cur, h] = m
            fold(nxt)

        def block_pair(t, carry):
            step(2 * t, 0, 1, True)
            step(2 * t + 1, 1, 0, 2 * t + 1 < i)
            return carry

        lax.fori_loop(0, (i + 1) // 2, block_pair, 0)
        fold(1)
        for h in range(heads):
            acc = acc_ref[h]
            o_ref[0, h * hd:(h + 1) * hd, pl.ds(q0, blk)] = acc[:hd] / acc[hd:hd + 1]
        return carry

    lax.fori_loop(0, nb, q_block, 0)


def _moba_attn(qt, kn, vt, sel):
    bsz, _, seq = qt.shape
    nb = seq // MOBA_BLOCK
    hg = ATT_HEADS_PER_STEP
    return pl.pallas_call(
        _moba_attn_kernel,
        grid=(bsz, ATT_HEADS // hg),
        in_specs=[
            pl.BlockSpec((1, hg * ATT_HEAD_DIM, seq), lambda b, h: (b, h, 0)),
            pl.BlockSpec((1, hg, seq, ATT_HEAD_DIM), lambda b, h: (b, h, 0, 0)),
            pl.BlockSpec((1, hg * ATT_V_ROWS, seq), lambda b, h: (b, h, 0)),
            pl.BlockSpec((1, hg, nb, seq), lambda b, h: (b, h, 0, 0)),
        ],
        out_specs=pl.BlockSpec((1, hg * ATT_HEAD_DIM, seq), lambda b, h: (b, h, 0)),
        out_shape=jax.ShapeDtypeStruct((bsz, ATT_WIDTH, seq), F32),
        scratch_shapes=[pltpu.VMEM((hg, 1, MOBA_BLOCK), F32),
                        pltpu.VMEM((hg, ATT_V_ROWS, MOBA_BLOCK), F32),
                        pltpu.VMEM((2, hg, MOBA_BLOCK, MOBA_BLOCK), F32),
                        pltpu.VMEM((2, hg, ATT_V_ROWS, MOBA_BLOCK), F32),
                        pltpu.VMEM((2, hg, 1, MOBA_BLOCK), F32)],
        compiler_params=_cparams(2),
        name="moba_attn",
    )(qt, kn, vt, sel)


def _merge_kernel(x_ref, g_ref, ys_ref, yr_ref, at_ref, wg_ref, ws_ref, wr_ref, wa_ref, wo_ref, o_ref,
                  ysp_ref):
    x = x_ref[...]
    tm = x.shape[0]
    xn = _rms_rows(x, g_ref[...]).astype(BF16)
    for chunk in range(tm // LS_S5):
        for s in range(SUBLANES):
            for t0 in range(0, S5_TSUB, SUBLANES):
                r = chunk * LS_S5 + s * S5_TSUB + t0
                for c in range(SSM_WIDTH // LANES):
                    ysp_ref[r:r + SUBLANES, c * LANES:(c + 1) * LANES] = ys_ref[
                        c, pl.ds(chunk * LS_S5 + SUBLANES * t0 + s, SUBLANES, stride=SUBLANES), :]

    def gate(k):
        return _sigmoid(jnp.dot(xn, wg_ref[:, k * D_MODEL:(k + 1) * D_MODEL], preferred_element_type=F32))

    merged = gate(0) * jnp.dot(ysp_ref[...].astype(BF16), ws_ref[...], preferred_element_type=F32)
    merged += gate(1) * jnp.dot(yr_ref[...].astype(BF16), wr_ref[...], preferred_element_type=F32)
    merged += gate(2) * lax.dot_general(at_ref[0].astype(BF16), wa_ref[...], (((0,), (0,)), ((), ())),
                                        preferred_element_type=F32)
    o_ref[...] = x + jnp.dot(merged.astype(BF16), wo_ref[...], preferred_element_type=F32)


def _merge(x2d, g, y_s, y_r, o_t, w_gates, w_s, w_r, w_a, w_o, bsz, seq):
    tm = TM_MERGE
    per_seq = seq // tm
    row = lambda i: (i, 0)
    const = lambda i: (0, 0)
    return pl.pallas_call(
        _merge_kernel,
        grid=(bsz * per_seq,),
        in_specs=[
            pl.BlockSpec((tm, D_MODEL), row),
            pl.BlockSpec((1, D_MODEL), const),
            pl.BlockSpec((SSM_WIDTH // LANES, tm, LANES), lambda i: (0, i, 0)),
            pl.BlockSpec((tm, RET_V_WIDTH), row),
            pl.BlockSpec((1, ATT_WIDTH, tm), lambda i: (i // per_seq, 0, i % per_seq)),
            pl.BlockSpec((D_MODEL, N_GATES), const),
            pl.BlockSpec((SSM_WIDTH, D_MODEL), const),
            pl.BlockSpec((RET_V_WIDTH, D_MODEL), const),
            pl.BlockSpec((ATT_WIDTH, D_MODEL), const),
            pl.BlockSpec((D_MODEL, D_MODEL), const),
        ],
        out_specs=pl.BlockSpec((tm, D_MODEL), row),
        out_shape=jax.ShapeDtypeStruct(x2d.shape, F32),
        scratch_shapes=[pltpu.VMEM((tm, SSM_WIDTH), F32)],
        compiler_params=_cparams(1),
        name="merge",
    )(x2d, g, y_s, y_r, o_t, w_gates, w_s, w_r, w_a, w_o)


def _ffn_kernel(x_ref, halo_ref, g_ref, wg_ref, wu_ref, cwg_ref, cwu_ref, cbg_ref, cbu_ref,
                wd_ref, o_ref, xn_ref, acc_ref, *, tiles_per_seq):
    i = pl.program_id(0)
    c = pl.program_id(1)
    tm = x_ref.shape[0]

    @pl.when(c == 0)
    def _():
        g = g_ref[...]
        xn_ref[0:HALO, :] = _rms_rows(halo_ref[...], g).astype(BF16)
        xn_ref[HALO:, :] = _rms_rows(x_ref[...], g).astype(BF16)
        acc_ref[...] = jnp.zeros(acc_ref.shape, F32)

    inside_seq = i % tiles_per_seq != 0

    def conv(w_ref, cw_ref, cb_ref):
        h = jnp.dot(xn_ref[...], w_ref[...], preferred_element_type=F32)
        h = jnp.concatenate([jnp.where(inside_seq, h[:HALO], 0.0), h[HALO:]], axis=0)
        cw = cw_ref[...]
        return (cb_ref[...] + h[HALO - 2:tm + HALO - 2] * cw[0:1] + h[HALO - 1:tm + HALO - 1] * cw[1:2]
                + h[HALO:] * cw[2:3])

    hg = conv(wg_ref, cwg_ref, cbg_ref)
    hu = conv(wu_ref, cwu_ref, cbu_ref)
    act = (hg * _sigmoid(hg) * hu).astype(BF16)
    acc_ref[...] += jnp.dot(act, wd_ref[...], preferred_element_type=F32)

    @pl.when(c == pl.num_programs(1) - 1)
    def _():
        o_ref[...] = x_ref[...] + acc_ref[...]


def _ffn(x2d, g, w_up, conv_w, conv_b, w_down, seq):
    t = x2d.shape[0]
    tm, fc = TM_FFN, FC_FFN
    nfc = FFN_HIDDEN // fc
    halo_blocks = tm // HALO
    wmode = dict(pipeline_mode=pl.Buffered(1)) if nfc == 1 else {}
    return pl.pallas_call(
        functools.partial(_ffn_kernel, tiles_per_seq=seq // tm),
        grid=(t // tm, nfc),
        in_specs=[
            pl.BlockSpec((tm, D_MODEL), lambda i, c: (i, 0)),
            pl.BlockSpec((HALO, D_MODEL), lambda i, c: (jnp.maximum(i * halo_blocks - 1, 0), 0)),
            pl.BlockSpec((1, D_MODEL), lambda i, c: (0, 0)),
            pl.BlockSpec((D_MODEL, fc), lambda i, c: (0, c), **wmode),
            pl.BlockSpec((D_MODEL, fc), lambda i, c: (0, c + nfc), **wmode),
            pl.BlockSpec((CONV_WIDTH, fc), lambda i, c: (0, c)),
            pl.BlockSpec((CONV_WIDTH, fc), lambda i, c: (0, c + nfc)),
            pl.BlockSpec((1, fc), lambda i, c: (0, c)),
            pl.BlockSpec((1, fc), lambda i, c: (0, c + nfc)),
            pl.BlockSpec((fc, D_MODEL), lambda i, c: (c, 0), **wmode),
        ],
        out_specs=pl.BlockSpec((tm, D_MODEL), lambda i, c: (i, 0)),
        out_shape=jax.ShapeDtypeStruct(x2d.shape, F32),
        scratch_shapes=[pltpu.VMEM((tm + HALO, D_MODEL), BF16), pltpu.VMEM((tm, D_MODEL), F32)],
        compiler_params=_cparams(2),
        name="ffn",
    )(x2d, x2d, g, w_up, w_up, conv_w, conv_w, conv_b, conv_b, w_down)


def kernel(x, norm1_g, w_in, ssm_lambda_re, ssm_lambda_im, ssm_log_dt, ssm_b_re, ssm_b_im, ssm_c_re, ssm_c_im, ssm_d, ssm_w_glu, ret_gn_g, attn_qn_g, attn_kn_g, w_br_ssm, w_br_ret, w_br_att, w_o, norm2_g, ffn_w_up, ffn_conv_w, ffn_conv_b, ffn_w_down):
    bsz, seq, _ = x.shape
    depth = w_in.shape[0]
    assert seq % MOBA_BLOCK == 0 and seq % TM_PROJ == 0 and seq % TM_FFN == 0
    ret_tables = _ret_tables(seq)
    moba_tables = _moba_tables(seq)
    bre_emb, cre_emb = _s5_embed(ssm_b_re, ssm_c_re)
    bim_emb, cim_emb = _s5_embed(ssm_b_im, ssm_c_im)
    cmat_all = jnp.concatenate([cre_emb, -cim_emb], axis=2).astype(BF16)
    x2d = x.reshape(bsz * seq, D_MODEL)
    for l in range(depth):
        w = w_in[l]
        w_main = w[:, :N_MAIN].astype(BF16)
        w_att_t = w[:, N_MAIN:N_MAIN + N_ATT].T.astype(BF16)
        w_gates = w[:, N_MAIN + N_ATT:].astype(BF16)
        norm1 = norm1_g[l].reshape(1, D_MODEL)
        bmat, coef_a, coef_s, ptab = _s5_prep(ssm_lambda_re[l], ssm_lambda_im[l], ssm_log_dt[l],
                                              bre_emb[l], bim_emb[l])
        s5_params = (bmat, coef_a, coef_s, ptab, cmat_all[l], ssm_d[l].reshape(1, SSM_WIDTH),
                     ssm_w_glu[l].astype(BF16))
        y_s, y_r, qt, kn, vt, sel = _in_proj(x2d, norm1, w_main, w_att_t, attn_qn_g[l], attn_kn_g[l],
                                             moba_tables, ret_gn_g[l].reshape(1, RET_V_WIDTH), ret_tables,
                                             s5_params, bsz, seq)
        o_t = _moba_attn(qt, kn, vt, sel)
        x2d = _merge(x2d, norm1, y_s, y_r, o_t, w_gates, w_br_ssm[l].astype(BF16), w_br_ret[l].astype(BF16),
                     w_br_att[l].astype(BF16), w_o[l].astype(BF16), bsz, seq)
        x2d = _ffn(x2d, norm2_g[l].reshape(1, D_MODEL), ffn_w_up[l].astype(BF16), ffn_conv_w[l],
                   ffn_conv_b[l].reshape(1, 2 * FFN_HIDDEN), ffn_w_down[l].astype(BF16), seq)
    return x2d.reshape(bsz, seq, D_MODEL)
```
